```python
import jax
import jax.numpy as jnp
from jax import lax
import numpy as np

D_MODEL = 1024
BATCH = 8
SEQ = 2048
DEPTH = 4

CHUNK = 64
EPS = 1e-6

A_HEADS = 8
A_HEAD_DIM = 64
A_WIDTH = A_HEADS * A_HEAD_DIM
N_PREV_CHUNKS = 8
BAND = (N_PREV_CHUNKS + 1) * CHUNK
REL_CLIP = 128
N_REL = 2 * REL_CLIP + 1

B_HEADS = 4
B_KEY_DIM = 64
B_VAL_DIM = 128
B_KEY_WIDTH = B_HEADS * B_KEY_DIM
B_WIDTH = B_HEADS * B_VAL_DIM
GATE_RANK = 16
GATE_TAU = 16.0

N_BRANCH = 2
BRANCH_WIDTH = 512

D_FF = 4 * D_MODEL

IN_SPLITS = (A_WIDTH, A_WIDTH, A_WIDTH, B_KEY_WIDTH, B_KEY_WIDTH, B_WIDTH, B_WIDTH, GATE_RANK, D_MODEL, D_MODEL)
IN_COLS = 5136

kernel_name = "hybrid_bandattn_gla_gated_merge"


def rmsnorm(x, g):
    xf = x.astype(jnp.float32)
    y = xf * lax.rsqrt(jnp.mean(xf * xf, axis=-1, keepdims=True) + EPS)
    return (y * g.astype(jnp.float32)).astype(x.dtype)


def _rel_index():
    i = np.arange(CHUNK)[:, None]
    kk = np.arange(BAND)[None, :]
    dist = N_PREV_CHUNKS * CHUNK + i - kk
    return (np.clip(dist, -REL_CLIP, REL_CLIP) + REL_CLIP).astype(np.int32)


def band_attention(q, k, v, rel_bias):
    Bsz, S, H, Dh = q.shape
    n_c = S // CHUNK
    qc = q.reshape(Bsz, n_c, CHUNK, H, Dh)
    pad = ((0, 0), (N_PREV_CHUNKS, 0), (0, 0), (0, 0), (0, 0))
    kp = jnp.pad(k.reshape(Bsz, n_c, CHUNK, H, Dh), pad)
    vp = jnp.pad(v.reshape(Bsz, n_c, CHUNK, H, Dh), pad)
    band_idx = jnp.arange(n_c)[:, None] + jnp.arange(N_PREV_CHUNKS + 1)[None, :]
    kb = kp[:, band_idx].reshape(Bsz, n_c, BAND, H, Dh)
    vb = vp[:, band_idx].reshape(Bsz, n_c, BAND, H, Dh)
    s = jnp.einsum('bcqhd,bckhd->bhcqk', qc, kb).astype(jnp.float32) * (Dh ** -0.5)
    bias = rel_bias.astype(jnp.float32)[:, _rel_index()]
    s = s + bias[:, None, :, :]
    key_chunk = jnp.arange(n_c)[:, None] - N_PREV_CHUNKS + (jnp.arange(BAND) // CHUNK)[None, :]
    valid = key_chunk >= 0
    s = jnp.where(valid[None, None, :, None, :], s, -1e30)
    p = jax.nn.softmax(s, axis=-1).astype(v.dtype)
    o = jnp.einsum('bhcqk,bckhd->bcqhd', p, vb)
    return o.reshape(Bsz, S, H * Dh)


def gla(q, k, v, log_a):
    f32 = jnp.float32
    Bsz, S, H, Dk = q.shape
    Dv = v.shape[-1]
    n_c = S // CHUNK
    qc = q.reshape(Bsz, n_c, CHUNK, H, Dk).astype(f32) * (Dk ** -0.5)
    kc = k.reshape(Bsz, n_c, CHUNK, H, Dk).astype(f32)
    vc = v.reshape(Bsz, n_c, CHUNK, H, Dv).astype(f32)
    b = jnp.cumsum(log_a.reshape(Bsz, n_c, CHUNK, H, Dk).astype(f32), axis=2)
    b_last = b[:, :, -1]
    q_t = qc * jnp.exp(b)
    k_t = kc * jnp.exp(-b)
    k_end = kc * jnp.exp(b_last[:, :, None] - b)
    att = jnp.einsum('bcihd,bcjhd->bchij', q_t, k_t)
    causal = jnp.tril(jnp.ones((CHUNK, CHUNK), dtype=bool))
    att = jnp.where(causal, att, 0.0)
    o_intra = jnp.einsum('bchij,bcjhv->bcihv', att, vc)
    d_state = jnp.einsum('bcjhd,bcjhv->bchdv', k_end, vc)

    def step(state, inp):
        ds_c, decay_c = inp
        return decay_c[..., None] * state + ds_c, state

    s0 = jnp.zeros((Bsz, H, Dk, Dv), f32)
    _, s_prev = lax.scan(step, s0, (jnp.swapaxes(d_state, 0, 1), jnp.swapaxes(jnp.exp(b_last), 0, 1)))
    s_prev = jnp.swapaxes(s_prev, 0, 1)
    o_inter = jnp.einsum('bcihd,bchdv->bcihv', q_t, s_prev)
    return (o_intra + o_inter).reshape(Bsz, S, H, Dv)


def hybrid_mixer(h, w_in, rel_bias, w_gate_lr, b_gate, gla_norm_g, w_branch, w_out):
    Bsz, S, _ = h.shape
    proj = h @ w_in
    split_at = np.cumsum(np.array(IN_SPLITS))[:-1].tolist()
    qa, ka, va, qb, kb, vb, rb, lrb, ga, gb = jnp.split(proj, split_at, axis=-1)
    y_a = band_attention(qa.reshape(Bsz, S, A_HEADS, A_HEAD_DIM),
                         ka.reshape(Bsz, S, A_HEADS, A_HEAD_DIM),
                         va.reshape(Bsz, S, A_HEADS, A_HEAD_DIM), rel_bias)
    log_a = jax.nn.log_sigmoid((lrb @ w_gate_lr + b_gate).astype(jnp.float32)) / GATE_TAU
    o_b = gla(qb.reshape(Bsz, S, B_HEADS, B_KEY_DIM),
              kb.reshape(Bsz, S, B_HEADS, B_KEY_DIM),
              vb.reshape(Bsz, S, B_HEADS, B_VAL_DIM),
              log_a.reshape(Bsz, S, B_HEADS, B_KEY_DIM))
    o_b = o_b * lax.rsqrt(jnp.mean(o_b * o_b, axis=-1, keepdims=True) + EPS)
    o_b = o_b * gla_norm_g.astype(jnp.float32).reshape(B_HEADS, B_VAL_DIM)
    y_b = (o_b.reshape(Bsz, S, B_WIDTH) * jax.nn.silu(rb.astype(jnp.float32))).astype(h.dtype)
    u_a = y_a @ w_branch[0]
    u_b = y_b @ w_branch[1]
    merged = jax.nn.sigmoid(ga) * u_a + jax.nn.sigmoid(gb) * u_b
    return merged @ w_out


def sqrelu_mlp(h, w_up, w_down):
    return jnp.square(jax.nn.relu(h @ w_up)) @ w_down


def setup_inputs(seed: int = 0) -> dict:
    key = jax.random.key(seed)
    ks = jax.random.split(key, 13)

    def nrm(k, shape, scale):
        return scale * jax.random.normal(k, shape, jnp.float32)

    return {
        "x": nrm(ks[0], (BATCH, SEQ, D_MODEL), 1.0),
        "mix_norm_g": 1.0 + nrm(ks[1], (DEPTH, D_MODEL), 0.02),
        "w_in": nrm(ks[2], (DEPTH, D_MODEL, IN_COLS), D_MODEL ** -0.5),
        "rel_bias": nrm(ks[3], (DEPTH, A_HEADS, N_REL), 0.1),
        "w_gate_lr": nrm(ks[4], (DEPTH, GATE_RANK, B_KEY_WIDTH), GATE_RANK ** -0.5),
        "b_gate": nrm(ks[5], (DEPTH, B_KEY_WIDTH), 0.5),
        "gla_norm_g": 1.0 + nrm(ks[6], (DEPTH, B_WIDTH), 0.02),
        "w_branch": nrm(ks[7], (DEPTH, N_BRANCH, BRANCH_WIDTH, D_MODEL), BRANCH_WIDTH ** -0.5),
        "w_out": nrm(ks[8], (DEPTH, D_MODEL, D_MODEL), D_MODEL ** -0.5),
        "mlp_norm_g": 1.0 + nrm(ks[9], (DEPTH, D_MODEL), 0.02),
        "w_up": nrm(ks[10], (DEPTH, D_MODEL, D_FF), D_MODEL ** -0.5),
        "w_down": nrm(ks[11], (DEPTH, D_FF, D_MODEL), D_FF ** -0.5),
        "final_norm_g": 1.0 + nrm(ks[12], (D_MODEL,), 0.02),
    }


def reference(x, mix_norm_g, w_in, rel_bias, w_gate_lr, b_gate, gla_norm_g, w_branch, w_out,
              mlp_norm_g, w_up, w_down, final_norm_g):
    res = x
    for l in range(DEPTH):
        h = rmsnorm(res, mix_norm_g[l])
        res = res + hybrid_mixer(h, w_in[l], rel_bias[l], w_gate_lr[l], b_gate[l], gla_norm_g[l],
                                 w_branch[l], w_out[l])
        h = rmsnorm(res, mlp_norm_g[l])
        res = res + sqrelu_mlp(h, w_up[l], w_down[l])
    return rmsnorm(res, final_norm_g)
```

```python
import functools

import jax
import jax.numpy as jnp
from jax import lax
from jax.experimental import pallas as pl
from jax.experimental.pallas import tpu as pltpu

D_MODEL = 1024
DEPTH = 4
CHUNK = 64
EPS = 1e-6

A_HEADS = 8
A_HEAD_DIM = 64
A_WIDTH = A_HEADS * A_HEAD_DIM
N_PREV_CHUNKS = 8
PREV = N_PREV_CHUNKS * CHUNK
REL_CLIP = 128
N_REL = 2 * REL_CLIP + 1

B_HEADS = 4
B_KEY_DIM = 64
B_VAL_DIM = 128
B_KEY_WIDTH = B_HEADS * B_KEY_DIM
B_WIDTH = B_HEADS * B_VAL_DIM
GATE_RANK = 16
GATE_TAU = 16.0
D_FF = 4 * D_MODEL

LANES = 128
Q_BLOCK = 2 * CHUNK
K_WINDOW = PREV + Q_BLOCK
TOEPLITZ_W = 768
VMEM_LIMIT = 56 * 1024 * 1024

F32 = jnp.float32
BF16 = jnp.bfloat16
NEG = -1e30


def _rmsnorm_bf16(x, g):
    ms = jnp.mean(x * x, axis=-1, keepdims=True)
    return ((x * lax.rsqrt(ms + EPS)) * g).astype(BF16)


def _sigmoid(x):
    return 1.0 / (1.0 + jnp.exp(-x))


def _dot(a, b):
    return jnp.dot(a, b, preferred_element_type=F32)


def _dot_nt(a, b):
    return lax.dot_general(a, b, (((1,), (1,)), ((), ())), preferred_element_type=F32)


def _dot_tn(a, b):
    return lax.dot_general(a, b, (((0,), (0,)), ((), ())), preferred_element_type=F32)


def _const_spec(shape):
    nd = len(shape)
    return pl.BlockSpec(shape, lambda *_: (0,) * nd)


def _bias_kernel(t_ref, o_ref):
    x = jnp.broadcast_to(t_ref[...], (Q_BLOCK, TOEPLITZ_W))
    row = lax.broadcasted_iota(jnp.int32, (Q_BLOCK, TOEPLITZ_W), 0)
    shift = 1
    while shift < Q_BLOCK:
        x = jnp.where((row & shift) != 0, pltpu.roll(x, shift, axis=1), x)
        shift *= 2
    x = x[:, :K_WINDOW]
    r = lax.broadcasted_iota(jnp.int32, (Q_BLOCK, K_WINDOW), 0)
    c = lax.broadcasted_iota(jnp.int32, (Q_BLOCK, K_WINDOW), 1)
    band_lo = (r // CHUNK) * CHUNK
    in_band = (c >= band_lo) & (c < band_lo + PREV + CHUNK)
    o_ref[...] = jnp.where(in_band, x, NEG)


def _bias_tables(rel_bias):
    far = rel_bias[..., N_REL - 1:]
    n_far_lo = PREV - REL_CLIP + 1
    t0 = jnp.concatenate([
        jnp.broadcast_to(far, rel_bias.shape[:-1] + (n_far_lo,)),
        rel_bias[..., N_REL - 2:0:-1],
        jnp.broadcast_to(far, rel_bias.shape[:-1] + (TOEPLITZ_W - n_far_lo - (N_REL - 2),)),
    ], axis=-1)
    n = DEPTH * A_HEADS
    t0 = t0.reshape(n, 1, TOEPLITZ_W)
    return pl.pallas_call(
        _bias_kernel,
        grid=(n,),
        in_specs=[pl.BlockSpec((None, 1, TOEPLITZ_W), lambda i: (i, 0, 0))],
        out_specs=pl.BlockSpec((None, Q_BLOCK, K_WINDOW), lambda i: (i, 0, 0)),
        out_shape=jax.ShapeDtypeStruct((n, Q_BLOCK, K_WINDOW), F32),
        name="bias_table",
    )(t0)


def _inproj_kernel(res_ref, g_ref, wa_ref, wb_ref, wlr_ref, wg_ref, bg_ref,
                   qkva_ref, bq_ref, loga_ref):
    h = _rmsnorm_bf16(res_ref[...], g_ref[...])
    qkva_ref[...] = _dot(h, wa_ref[...]).astype(BF16)
    bq_ref[...] = _dot(h, wb_ref[...])
    lr = _dot(h, wlr_ref[...])
    z = _dot(lr.astype(BF16), wg_ref[...]) + bg_ref[...]
    log_sig = jnp.minimum(z, 0.0) - jnp.log(1.0 + jnp.exp(-jnp.abs(z)))
    loga_ref[...] = log_sig / GATE_TAU


def _inproj(res, g, wa, wb, wlr, wg, bg, tm):
    t = res.shape[0]
    nb = wb.shape[1]
    return pl.pallas_call(
        _inproj_kernel,
        grid=(t // tm,),
        in_specs=[
            pl.BlockSpec((tm, D_MODEL), lambda i: (i, 0)),
            _const_spec((1, D_MODEL)),
            _const_spec(wa.shape),
            _const_spec(wb.shape),
            _const_spec(wlr.shape),
            _const_spec(wg.shape),
            _const_spec((1, B_KEY_WIDTH)),
        ],
        out_specs=[
            pl.BlockSpec((tm, 3 * A_WIDTH), lambda i: (i, 0)),
            pl.BlockSpec((tm, nb), lambda i: (i, 0)),
            pl.BlockSpec((tm, B_KEY_WIDTH), lambda i: (i, 0)),
        ],
        out_shape=[
            jax.ShapeDtypeStruct((t, 3 * A_WIDTH), BF16),
            jax.ShapeDtypeStruct((t, nb), F32),
            jax.ShapeDtypeStruct((t, B_KEY_WIDTH), F32),
        ],
        compiler_params=pltpu.CompilerParams(
            dimension_semantics=("parallel",), vmem_limit_bytes=VMEM_LIMIT),
        name="inproj",
    )(res, g, wa, wb, wlr, wg, bg)


def _attn_kernel(q_ref, k_ref, v_ref, bias_ref, o_ref, kpad, vpad):
    seq = q_ref.shape[0]
    kpad[0:PREV, :] = jnp.zeros((PREV, LANES), BF16)
    vpad[0:PREV, :] = jnp.zeros((PREV, LANES), BF16)
    kpad[PREV:, :] = k_ref[...]
    vpad[PREV:, :] = v_ref[...]
    lo = lax.broadcasted_iota(jnp.int32, (1, LANES), 1) < A_HEAD_DIM
    col = lax.broadcasted_iota(jnp.int32, (1, K_WINDOW), 1)

    def body(t, carry):
        bias = bias_ref[...].reshape(2 * Q_BLOCK, K_WINDOW)
        qs = pl.multiple_of(t * Q_BLOCK, Q_BLOCK)
        q2 = q_ref[pl.ds(qs, Q_BLOCK), :].astype(F32) * (A_HEAD_DIM ** -0.5)
        q4 = jnp.concatenate(
            [jnp.where(lo, q2, 0.0), jnp.where(lo, 0.0, q2)], axis=0).astype(BF16)
        kw = kpad[pl.ds(qs, K_WINDOW), :]
        vw = vpad[pl.ds(qs, K_WINDOW), :]
        s = _dot_nt(q4, kw) + bias
        s = jnp.where(col >= PREV - qs, s, NEG)
        m = jnp.max(s, axis=-1, keepdims=True)
        e = jnp.exp(s - m)
        l = jnp.sum(e, axis=-1, keepdims=True)
        e = e.astype(BF16)
        o_even = _dot(e[:Q_BLOCK], vw) * (1.0 / l[:Q_BLOCK])
        o_odd = _dot(e[Q_BLOCK:], vw) * (1.0 / l[Q_BLOCK:])
        o_ref[pl.ds(qs, Q_BLOCK), :] = jnp.where(lo, o_even, o_odd).astype(BF16)
        return carry

    lax.fori_loop(0, seq // Q_BLOCK, body, 0)


def _attention(qkva, bias, layer):
    bsz, seq, _ = qkva.shape
    n_pair = A_HEADS // 2
    blk = (None, seq, LANES)
    return pl.pallas_call(
        _attn_kernel,
        grid=(bsz, n_pair),
        in_specs=[
            pl.BlockSpec(blk, lambda b, p: (b, 0, p)),
            pl.BlockSpec(blk, lambda b, p: (b, 0, n_pair + p)),
            pl.BlockSpec(blk, lambda b, p: (b, 0, 2 * n_pair + p)),
            pl.BlockSpec((2, Q_BLOCK, K_WINDOW), lambda b, p: (layer * n_pair + p, 0, 0)),
        ],
        out_specs=pl.BlockSpec(blk, lambda b, p: (b, 0, p)),
        out_shape=jax.ShapeDtypeStruct((bsz, seq, A_WIDTH), BF16),
        scratch_shapes=[pltpu.VMEM((PREV + seq, LANES), BF16),
                        pltpu.VMEM((PREV + seq, LANES), BF16)],
        compiler_params=pltpu.CompilerParams(
            dimension_semantics=("parallel", "parallel"), vmem_limit_bytes=VMEM_LIMIT),
        name="band_attention",
    )(qkva, qkva, qkva, bias)


def _chunk_cumsum(x):
    row = lax.broadcasted_iota(jnp.int32, x.shape, 0)
    d = 1
    while d < x.shape[0]:
        x = x + jnp.where(row >= d, pltpu.roll(x, d, axis=0), 0.0)
        d *= 2
    return x


def _gla_kernel(q_ref, k_ref, v_ref, r_ref, la_ref, g_ref, o_ref):
    seq = q_ref.shape[0]
    lo = lax.broadcasted_iota(jnp.int32, (1, LANES), 1) < B_KEY_DIM
    ri = lax.broadcasted_iota(jnp.int32, (2 * CHUNK, CHUNK), 0)
    ci = lax.broadcasted_iota(jnp.int32, (2 * CHUNK, CHUNK), 1)
    causal = (ri % CHUNK) >= ci
    g = g_ref[...]

    def head_out(o, gh, r):
        o = o * lax.rsqrt(jnp.mean(o * o, axis=-1, keepdims=True) + EPS) * gh
        return (o * (r * _sigmoid(r))).astype(BF16)

    def body(c, st):
        cs = pl.multiple_of(c * CHUNK, CHUNK)
        rows = pl.ds(cs, CHUNK)
        b = _chunk_cumsum(la_ref[rows, :])
        b_last = b[CHUNK - 1:CHUNK, :]
        qt = q_ref[rows, :] * (B_KEY_DIM ** -0.5) * jnp.exp(b)
        k = k_ref[rows, :]
        kt = (k * jnp.exp(-b)).astype(BF16)
        kend = (k * jnp.exp(b_last - b)).astype(BF16)
        q4 = jnp.concatenate(
            [jnp.where(lo, qt, 0.0), jnp.where(lo, 0.0, qt)], axis=0).astype(BF16)
        v = v_ref[rows, :].astype(BF16)
        v_even, v_odd = v[:, :B_VAL_DIM], v[:, B_VAL_DIM:]
        att = jnp.where(causal, _dot_nt(q4, kt), 0.0).astype(BF16)
        inter = _dot_nt(q4, st.astype(BF16))
        o_even = _dot(att[:CHUNK], v_even) + inter[:CHUNK]
        o_odd = _dot(att[CHUNK:], v_odd) + inter[CHUNK:]
        r = r_ref[rows, :]
        o_ref[rows, :B_VAL_DIM] = head_out(o_even, g[:, :B_VAL_DIM], r[:, :B_VAL_DIM])
        o_ref[rows, B_VAL_DIM:] = head_out(o_odd, g[:, B_VAL_DIM:], r[:, B_VAL_DIM:])
        d_st = jnp.where(lo, _dot_tn(v_even, kend), _dot_tn(v_odd, kend))
        return st * jnp.exp(b_last) + d_st

    lax.fori_loop(0, seq // CHUNK, body, jnp.zeros((B_VAL_DIM, LANES), F32))


def _gla(bq, loga, g):
    bsz, seq, _ = bq.shape
    n_pair = B_HEADS // 2
    kblk = (None, seq, LANES)
    vblk = (None, seq, 2 * B_VAL_DIM)
    k_off = B_KEY_WIDTH // LANES
    v_off = 2 * B_KEY_WIDTH // (2 * B_VAL_DIM)
    r_off = (2 * B_KEY_WIDTH + B_WIDTH) // (2 * B_VAL_DIM)
    return pl.pallas_call(
        _gla_kernel,
        grid=(bsz, n_pair),
        in_specs=[
            pl.BlockSpec(kblk, lambda b, p: (b, 0, p)),
            pl.BlockSpec(kblk, lambda b, p: (b, 0, k_off + p)),
            pl.BlockSpec(vblk, lambda b, p: (b, 0, v_off + p)),
            pl.BlockSpec(vblk, lambda b, p: (b, 0, r_off + p)),
            pl.BlockSpec(kblk, lambda b, p: (b, 0, p)),
            pl.BlockSpec((1, 2 * B_VAL_DIM), lambda b, p: (0, p)),
        ],
        out_specs=pl.BlockSpec(vblk, lambda b, p: (b, 0, p)),
        out_shape=jax.ShapeDtypeStruct((bsz, seq, B_WIDTH), BF16),
        compiler_params=pltpu.CompilerParams(
            dimension_semantics=("parallel", "parallel"), vmem_limit_bytes=VMEM_LIMIT),
        name="gla",
    )(bq, bq, bq, bq, loga, g)


def _merge_kernel(res_ref, g_ref, ya_ref, yb_ref, wgate_ref, wbr_ref, wout_ref, o_ref):
    res = res_ref[...]
    h = _rmsnorm_bf16(res, g_ref[...])
    gates = _dot(h, wgate_ref[...])
    u_a = _dot(ya_ref[...], wbr_ref[0])
    u_b = _dot(yb_ref[...], wbr_ref[1])
    merged = _sigmoid(gates[:, :D_MODEL]) * u_a + _sigmoid(gates[:, D_MODEL:]) * u_b
    o_ref[...] = res + _dot(merged.astype(BF16), wout_ref[...])


def _merge(res, g, ya, yb, wgate, wbr, wout, tm):
    t = res.shape[0]
    return pl.pallas_call(
        _merge_kernel,
        grid=(t // tm,),
        in_specs=[
            pl.BlockSpec((tm, D_MODEL), lambda i: (i, 0)),
            _const_spec((1, D_MODEL)),
            pl.BlockSpec((tm, A_WIDTH), lambda i: (i, 0)),
            pl.BlockSpec((tm, B_WIDTH), lambda i: (i, 0)),
            _const_spec(wgate.shape),
            _const_spec(wbr.shape),
            _const_spec(wout.shape),
        ],
        out_specs=pl.BlockSpec((tm, D_MODEL), lambda i: (i, 0)),
        out_shape=jax.ShapeDtypeStruct((t, D_MODEL), F32),
        compiler_params=pltpu.CompilerParams(
            dimension_semantics=("parallel",), vmem_limit_bytes=VMEM_LIMIT),
        name="merge",
    )(res, g, ya, yb, wgate, wbr, wout)


def _mlp_kernel(res_ref, g_ref, wup_ref, wdn_ref, gf_ref, o_ref, *, final_norm):
    res = res_ref[...]
    h = _rmsnorm_bf16(res, g_ref[...])
    acc = res
    for c in range(D_FF // D_MODEL):
        cols = slice(c * D_MODEL, (c + 1) * D_MODEL)
        up = jnp.maximum(_dot(h, wup_ref[:, cols]), 0.0)
        acc = acc + _dot((up * up).astype(BF16), wdn_ref[cols, :])
    if final_norm:
        ms = jnp.mean(acc * acc, axis=-1, keepdims=True)
        acc = (acc * lax.rsqrt(ms + EPS)) * gf_ref[...]
    o_ref[...] = acc


def _mlp(res, g, wup, wdn, gf, tm, final_norm):
    t = res.shape[0]
    return pl.pallas_call(
        functools.partial(_mlp_kernel, final_norm=final_norm),
        grid=(t // tm,),
        in_specs=[
            pl.BlockSpec((tm, D_MODEL), lambda i: (i, 0)),
            _const_spec((1, D_MODEL)),
            _const_spec(wup.shape),
            _const_spec(wdn.shape),
            _const_spec((1, D_MODEL)),
        ],
        out_specs=pl.BlockSpec((tm, D_MODEL), lambda i: (i, 0)),
        out_shape=jax.ShapeDtypeStruct((t, D_MODEL), F32),
        compiler_params=pltpu.CompilerParams(
            dimension_semantics=("parallel",), vmem_limit_bytes=VMEM_LIMIT),
        name="mlp",
    )(res, g, wup, wdn, gf)


def kernel(x, mix_norm_g, w_in, rel_bias, w_gate_lr, b_gate, gla_norm_g, w_branch, w_out,
           mlp_norm_g, w_up, w_down, final_norm_g):
    bsz, seq, d = x.shape
    t = bsz * seq
    tm = 512
    a_end = 3 * A_WIDTH
    b_end = a_end + 2 * B_KEY_WIDTH + 2 * B_WIDTH
    lr_end = b_end + GATE_RANK
    w_in_bf = w_in.astype(BF16)
    w_a = w_in_bf[:, :, :a_end]
    w_b = w_in_bf[:, :, a_end:b_end]
    w_lr = w_in_bf[:, :, b_end:lr_end]
    w_gates = w_in_bf[:, :, lr_end:]
    w_glr = w_gate_lr.astype(BF16)
    w_br = w_branch.astype(BF16)
    w_o = w_out.astype(BF16)
    w_u = w_up.astype(BF16)
    w_d = w_down.astype(BF16)
    bias = _bias_tables(rel_bias)

    res = x.reshape(t, d)
    for l in range(DEPTH):
        qkva, bq, loga = _inproj(res, mix_norm_g[l][None], w_a[l], w_b[l], w_lr[l], w_glr[l],
                                 b_gate[l][None], tm)
        ya = _attention(qkva.reshape(bsz, seq, -1), bias, l)
        yb = _gla(bq.reshape(bsz, seq, -1), loga.reshape(bsz, seq, -1), gla_norm_g[l][None])
        res = _merge(res, mix_norm_g[l][None], ya.reshape(t, -1), yb.reshape(t, -1),
                     w_gates[l], w_br[l], w_o[l], tm)
        res = _mlp(res, mlp_norm_g[l][None], w_u[l], w_d[l], final_norm_g[None], tm,
                   final_norm=(l == DEPTH - 1))
    return res.reshape(bsz, seq, d)
```

```python
import functools

import jax
import jax.numpy as jnp
from jax import lax
from jax.experimental import pallas as pl
from jax.experimental.pallas import tpu as pltpu

D_MODEL = 1024
DEPTH = 4
CHUNK = 64
EPS = 1e-6

A_HEADS = 8
A_HEAD_DIM = 64
A_WIDTH = A_HEADS * A_HEAD_DIM
N_PREV_CHUNKS = 8
PREV = N_PREV_CHUNKS * CHUNK
REL_CLIP = 128
N_REL = 2 * REL_CLIP + 1

B_HEADS = 4
B_KEY_DIM = 64
B_VAL_DIM = 128
B_KEY_WIDTH = B_HEADS * B_KEY_DIM
B_WIDTH = B_HEADS * B_VAL_DIM
GATE_RANK = 16
GATE_TAU = 16.0
D_FF = 4 * D_MODEL

LANES = 128
Q_BLOCK = 2 * CHUNK
K_WINDOW = PREV + Q_BLOCK
N_WIN_TILES = K_WINDOW // 128
TOEPLITZ_W = 768
VMEM_LIMIT = 56 * 1024 * 1024

F32 = jnp.float32
BF16 = jnp.bfloat16
NEG = -1e30


def _rmsnorm_bf16(x, g):
    ms = jnp.mean(x * x, axis=-1, keepdims=True)
    return ((x * lax.rsqrt(ms + EPS)) * g).astype(BF16)


def _sigmoid(x):
    return 1.0 / (1.0 + jnp.exp(-x))


def _dot(a, b):
    return jnp.dot(a, b, preferred_element_type=F32)


def _dot_nt(a, b):
    return lax.dot_general(a, b, (((1,), (1,)), ((), ())), preferred_element_type=F32)


def _dot_tn(a, b):
    return lax.dot_general(a, b, (((0,), (0,)), ((), ())), preferred_element_type=F32)


def _const_spec(shape):
    nd = len(shape)
    return pl.BlockSpec(shape, lambda *_: (0,) * nd)


def _bias_kernel(t_ref, o_ref):
    x = jnp.broadcast_to(t_ref[...], (Q_BLOCK, TOEPLITZ_W))
    row = lax.broadcasted_iota(jnp.int32, (Q_BLOCK, TOEPLITZ_W), 0)
    shift = 1
    while shift < Q_BLOCK:
        x = jnp.where((row & shift) != 0, pltpu.roll(x, shift, axis=1), x)
        shift *= 2
    x = x[:, :K_WINDOW]
    r = lax.broadcasted_iota(jnp.int32, (Q_BLOCK, K_WINDOW), 0)
    c = lax.broadcasted_iota(jnp.int32, (Q_BLOCK, K_WINDOW), 1)
    band_lo = (r // CHUNK) * CHUNK
    in_band = (c >= band_lo) & (c < band_lo + PREV + CHUNK)
    x = jnp.where(in_band, x, NEG)
    for j in range(K_WINDOW // LANES):
        o_ref[j * LANES:(j + 1) * LANES, :] = x[:, j * LANES:(j + 1) * LANES].T


def _bias_tables(rel_bias):
    far = rel_bias[..., N_REL - 1:]
    n_far_lo = PREV - REL_CLIP + 1
    t0 = jnp.concatenate([
        jnp.broadcast_to(far, rel_bias.shape[:-1] + (n_far_lo,)),
        rel_bias[..., N_REL - 2:0:-1],
        jnp.broadcast_to(far, rel_bias.shape[:-1] + (TOEPLITZ_W - n_far_lo - (N_REL - 2),)),
    ], axis=-1)
    n = DEPTH * A_HEADS
    t0 = t0.reshape(n, 1, TOEPLITZ_W)
    return pl.pallas_call(
        _bias_kernel,
        grid=(n,),
        in_specs=[pl.BlockSpec((None, 1, TOEPLITZ_W), lambda i: (i, 0, 0))],
        out_specs=pl.BlockSpec((None, K_WINDOW, Q_BLOCK), lambda i: (i // 2, 0, i % 2)),
        out_shape=jax.ShapeDtypeStruct((n // 2, K_WINDOW, 2 * Q_BLOCK), F32),
        name="bias_table",
    )(t0)


def _inproj_kernel(res_ref, g_ref, wa_ref, wb_ref, wlr_ref, wg_ref, bg_ref,
                   qkva_ref, bq_ref, loga_ref):
    h = _rmsnorm_bf16(res_ref[...], g_ref[...])
    qkva_ref[...] = _dot(h, wa_ref[...]).astype(BF16)
    bq_ref[...] = _dot(h, wb_ref[...])
    lr = _dot(h, wlr_ref[...])
    z = _dot(lr.astype(BF16), wg_ref[...]) + bg_ref[...]
    log_sig = jnp.minimum(z, 0.0) - jnp.log(1.0 + jnp.exp(-jnp.abs(z)))
    loga_ref[...] = log_sig / GATE_TAU


def _inproj(res, g, wa, wb, wlr, wg, bg, tm):
    t = res.shape[0]
    nb = wb.shape[1]
    return pl.pallas_call(
        _inproj_kernel,
        grid=(t // tm,),
        in_specs=[
            pl.BlockSpec((tm, D_MODEL), lambda i: (i, 0)),
            _const_spec((1, D_MODEL)),
            _const_spec(wa.shape),
            _const_spec(wb.shape),
            _const_spec(wlr.shape),
            _const_spec(wg.shape),
            _const_spec((1, B_KEY_WIDTH)),
        ],
        out_specs=[
            pl.BlockSpec((tm, 3 * A_WIDTH), lambda i: (i, 0)),
            pl.BlockSpec((tm, nb), lambda i: (i, 0)),
            pl.BlockSpec((tm, B_KEY_WIDTH), lambda i: (i, 0)),
        ],
        out_shape=[
            jax.ShapeDtypeStruct((t, 3 * A_WIDTH), BF16),
            jax.ShapeDtypeStruct((t, nb), F32),
            jax.ShapeDtypeStruct((t, B_KEY_WIDTH), F32),
        ],
        compiler_params=pltpu.CompilerParams(
            dimension_semantics=("parallel",), vmem_limit_bytes=VMEM_LIMIT),
        name="inproj",
    )(res, g, wa, wb, wlr, wg, bg)


def _attn_kernel(q_ref, k_ref, v_ref, bias_ref, o_ref, vt_scr, s0_scr, s1_scr, m0_scr, m1_scr):
    seq = q_ref.shape[0]
    n_blk = seq // Q_BLOCK
    n_prev = PREV // Q_BLOCK
    lo = lax.broadcasted_iota(jnp.int32, (1, LANES), 1) < A_HEAD_DIM

    for r in range(n_blk):
        vt_scr[r] = v_ref[r * LANES:(r + 1) * LANES, :].astype(F32).T.astype(BF16)

    def scores(qs, ks, n_tiles):
        q2 = q_ref[pl.ds(qs, Q_BLOCK), :].astype(F32) * (A_HEAD_DIM ** -0.5)
        q4 = jnp.concatenate(
            [jnp.where(lo, q2, 0.0), jnp.where(lo, 0.0, q2)], axis=0).astype(BF16)
        kw = k_ref[pl.ds(ks, n_tiles * LANES), :]
        s = _dot_nt(kw, q4) + bias_ref[(N_WIN_TILES - n_tiles) * LANES:, :]
        return s, jnp.max(s, axis=0, keepdims=True)

    def finish(s, m, qs, kt, n_tiles):
        e = jnp.exp(s - m)
        inv = 1.0 / jnp.sum(e, axis=0, keepdims=True)
        vt = jnp.concatenate([vt_scr[kt + i] for i in range(n_tiles)], axis=1)
        ot = _dot(vt, e.astype(BF16))
        o_pair = jnp.concatenate(
            [ot[:A_HEAD_DIM, :Q_BLOCK] * inv[:, :Q_BLOCK],
             ot[A_HEAD_DIM:, Q_BLOCK:] * inv[:, Q_BLOCK:]], axis=0)
        o_ref[pl.ds(qs, Q_BLOCK), :] = o_pair.T.astype(BF16)

    s, m = scores(0, 0, 1)
    for t in range(1, n_prev):
        s_next, m_next = scores(t * Q_BLOCK, 0, t + 1)
        finish(s, m, (t - 1) * Q_BLOCK, 0, t)
        s, m = s_next, m_next
    s0_scr[...], m0_scr[...] = scores(n_prev * Q_BLOCK, 0, N_WIN_TILES)
    finish(s, m, (n_prev - 1) * Q_BLOCK, 0, n_prev)

    def pair(qs0, kt0, score_next):
        s1_scr[...], m1_scr[...] = scores(qs0 + Q_BLOCK, qs0 + Q_BLOCK - PREV, N_WIN_TILES)
        finish(s0_scr[...], m0_scr[...], qs0, kt0, N_WIN_TILES)
        if score_next:
            s0_scr[...], m0_scr[...] = scores(
                qs0 + 2 * Q_BLOCK, qs0 + 2 * Q_BLOCK - PREV, N_WIN_TILES)
        finish(s1_scr[...], m1_scr[...], qs0 + Q_BLOCK, kt0 + 1, N_WIN_TILES)

    n_pairs = (n_blk - n_prev) // 2

    def body(u, carry):
        pair(pl.multiple_of((n_prev + 2 * u) * Q_BLOCK, Q_BLOCK), 2 * u, True)
        return carry

    lax.fori_loop(0, n_pairs - 1, body, 0)
    pair((n_blk - 2) * Q_BLOCK, n_blk - 2 - n_prev, False)


def _attention(qkva, bias, layer):
    bsz, seq, _ = qkva.shape
    n_pair = A_HEADS // 2
    blk = (None, seq, LANES)
    return pl.pallas_call(
        _attn_kernel,
        grid=(bsz, n_pair),
        in_specs=[
            pl.BlockSpec(blk, lambda b, p: (b, 0, p)),
            pl.BlockSpec(blk, lambda b, p: (b, 0, n_pair + p)),
            pl.BlockSpec(blk, lambda b, p: (b, 0, 2 * n_pair + p)),
            pl.BlockSpec((None, K_WINDOW, 2 * Q_BLOCK), lambda b, p: (layer * n_pair + p, 0, 0)),
        ],
        out_specs=pl.BlockSpec(blk, lambda b, p: (b, 0, p)),
        out_shape=jax.ShapeDtypeStruct((bsz, seq, A_WIDTH), BF16),
        scratch_shapes=[pltpu.VMEM((seq // LANES, LANES, LANES), BF16),
                        pltpu.VMEM((K_WINDOW, 2 * Q_BLOCK), F32),
                        pltpu.VMEM((K_WINDOW, 2 * Q_BLOCK), F32),
                        pltpu.VMEM((1, 2 * Q_BLOCK), F32),
                        pltpu.VMEM((1, 2 * Q_BLOCK), F32)],
        compiler_params=pltpu.CompilerParams(
            dimension_semantics=("parallel", "parallel"), vmem_limit_bytes=VMEM_LIMIT),
        name="band_attention",
    )(qkva, qkva, qkva, bias)


def _chunk_cumsum(x):
    row = lax.broadcasted_iota(jnp.int32, x.shape, 0)
    d = 1
    while d < x.shape[0]:
        x = x + jnp.where(row >= d, pltpu.roll(x, d, axis=0), 0.0)
        d *= 2
    return x


def _gla_kernel(q_ref, k_ref, v_ref, r_ref, la_ref, g_ref, o_ref):
    seq = q_ref.shape[0]
    lo = lax.broadcasted_iota(jnp.int32, (1, LANES), 1) < B_KEY_DIM
    ri = lax.broadcasted_iota(jnp.int32, (2 * CHUNK, CHUNK), 0)
    ci = lax.broadcasted_iota(jnp.int32, (2 * CHUNK, CHUNK), 1)
    causal = (ri % CHUNK) >= ci
    g = g_ref[...]

    def head_out(o, gh, r):
        o = o * lax.rsqrt(jnp.mean(o * o, axis=-1, keepdims=True) + EPS) * gh
        return (o * (r * _sigmoid(r))).astype(BF16)

    def body(c, st):
        cs = pl.multiple_of(c * CHUNK, CHUNK)
        rows = pl.ds(cs, CHUNK)
        b = _chunk_cumsum(la_ref[rows, :])
        b_last = b[CHUNK - 1:CHUNK, :]
        qt = q_ref[rows, :] * (B_KEY_DIM ** -0.5) * jnp.exp(b)
        k = k_ref[rows, :]
        kt = (k * jnp.exp(-b)).astype(BF16)
        kend = (k * jnp.exp(b_last - b)).astype(BF16)
        q4 = jnp.concatenate(
            [jnp.where(lo, qt, 0.0), jnp.where(lo, 0.0, qt)], axis=0).astype(BF16)
        v = v_ref[rows, :].astype(BF16)
        v_even, v_odd = v[:, :B_VAL_DIM], v[:, B_VAL_DIM:]
        att = jnp.where(causal, _dot_nt(q4, kt), 0.0).astype(BF16)
        inter = _dot_nt(q4, st.astype(BF16))
        o_even = _dot(att[:CHUNK], v_even) + inter[:CHUNK]
        o_odd = _dot(att[CHUNK:], v_odd) + inter[CHUNK:]
        r = r_ref[rows, :]
        o_ref[rows, :B_VAL_DIM] = head_out(o_even, g[:, :B_VAL_DIM], r[:, :B_VAL_DIM])
        o_ref[rows, B_VAL_DIM:] = head_out(o_odd, g[:, B_VAL_DIM:], r[:, B_VAL_DIM:])
        d_st = jnp.where(lo, _dot_tn(v_even, kend), _dot_tn(v_odd, kend))
        return st * jnp.exp(b_last) + d_st

    lax.fori_loop(0, seq // CHUNK, body, jnp.zeros((B_VAL_DIM, LANES), F32))


def _gla(bq, loga, g):
    bsz, seq, _ = bq.shape
    n_pair = B_HEADS // 2
    kblk = (None, seq, LANES)
    vblk = (None, seq, 2 * B_VAL_DIM)
    k_off = B_KEY_WIDTH // LANES
    v_off = 2 * B_KEY_WIDTH // (2 * B_VAL_DIM)
    r_off = (2 * B_KEY_WIDTH + B_WIDTH) // (2 * B_VAL_DIM)
    return pl.pallas_call(
        _gla_kernel,
        grid=(bsz, n_pair),
        in_specs=[
            pl.BlockSpec(kblk, lambda b, p: (b, 0, p)),
            pl.BlockSpec(kblk, lambda b, p: (b, 0, k_off + p)),
            pl.BlockSpec(vblk, lambda b, p: (b, 0, v_off + p)),
            pl.BlockSpec(vblk, lambda b, p: (b, 0, r_off + p)),
            pl.BlockSpec(kblk, lambda b, p: (b, 0, p)),
            pl.BlockSpec((1, 2 * B_VAL_DIM), lambda b, p: (0, p)),
        ],
        out_specs=pl.BlockSpec(vblk, lambda b, p: (b, 0, p)),
        out_shape=jax.ShapeDtypeStruct((bsz, seq, B_WIDTH), BF16),
        compiler_params=pltpu.CompilerParams(
            dimension_semantics=("parallel", "parallel"), vmem_limit_bytes=VMEM_LIMIT),
        name="gla",
    )(bq, bq, bq, bq, loga, g)


def _merge_kernel(res_ref, g_ref, ya_ref, yb_ref, wgate_ref, wbr_ref, wout_ref, o_ref):
    res = res_ref[...]
    h = _rmsnorm_bf16(res, g_ref[...])
    gates = _dot(h, wgate_ref[...])
    u_a = _dot(ya_ref[...], wbr_ref[0])
    u_b = _dot(yb_ref[...], wbr_ref[1])
    merged = _sigmoid(gates[:, :D_MODEL]) * u_a + _sigmoid(gates[:, D_MODEL:]) * u_b
    o_ref[...] = res + _dot(merged.astype(BF16), wout_ref[...])


def _merge(res, g, ya, yb, wgate, wbr, wout, tm):
    t = res.shape[0]
    return pl.pallas_call(
        _merge_kernel,
        grid=(t // tm,),
        in_specs=[
            pl.BlockSpec((tm, D_MODEL), lambda i: (i, 0)),
            _const_spec((1, D_MODEL)),
            pl.BlockSpec((tm, A_WIDTH), lambda i: (i, 0)),
            pl.BlockSpec((tm, B_WIDTH), lambda i: (i, 0)),
            _const_spec(wgate.shape),
            _const_spec(wbr.shape),
            _const_spec(wout.shape),
        ],
        out_specs=pl.BlockSpec((tm, D_MODEL), lambda i: (i, 0)),
        out_shape=jax.ShapeDtypeStruct((t, D_MODEL), F32),
        compiler_params=pltpu.CompilerParams(
            dimension_semantics=("parallel",), vmem_limit_bytes=VMEM_LIMIT),
        name="merge",
    )(res, g, ya, yb, wgate, wbr, wout)


def _mlp_kernel(res_ref, g_ref, wup_ref, wdn_ref, gf_ref, o_ref, *, final_norm):
    res = res_ref[...]
    h = _rmsnorm_bf16(res, g_ref[...])
    acc = res
    for c in range(D_FF // D_MODEL):
        cols = slice(c * D_MODEL, (c + 1) * D_MODEL)
        up = jnp.maximum(_dot(h, wup_ref[:, cols]), 0.0)
        acc = acc + _dot((up * up).astype(BF16), wdn_ref[cols, :])
    if final_norm:
        ms = jnp.mean(acc * acc, axis=-1, keepdims=True)
        acc = (acc * lax.rsqrt(ms + EPS)) * gf_ref[...]
    o_ref[...] = acc


def _mlp(res, g, wup, wdn, gf, tm, final_norm):
    t = res.shape[0]
    return pl.pallas_call(
        functools.partial(_mlp_kernel, final_norm=final_norm),
        grid=(t // tm,),
        in_specs=[
            pl.BlockSpec((tm, D_MODEL), lambda i: (i, 0)),
            _const_spec((1, D_MODEL)),
            _const_spec(wup.shape),
            _const_spec(wdn.shape),
            _const_spec((1, D_MODEL)),
        ],
        out_specs=pl.BlockSpec((tm, D_MODEL), lambda i: (i, 0)),
        out_shape=jax.ShapeDtypeStruct((t, D_MODEL), F32),
        compiler_params=pltpu.CompilerParams(
            dimension_semantics=("parallel",), vmem_limit_bytes=VMEM_LIMIT),
        name="mlp",
    )(res, g, wup, wdn, gf)


def kernel(x, mix_norm_g, w_in, rel_bias, w_gate_lr, b_gate, gla_norm_g, w_branch, w_out,
           mlp_norm_g, w_up, w_down, final_norm_g):
    bsz, seq, d = x.shape
    t = bsz * seq
    tm = 512
    a_end = 3 * A_WIDTH
    b_end = a_end + 2 * B_KEY_WIDTH + 2 * B_WIDTH
    lr_end = b_end + GATE_RANK
    w_in_bf = w_in.astype(BF16)
    w_a = w_in_bf[:, :, :a_end]
    w_b = w_in_bf[:, :, a_end:b_end]
    w_lr = w_in_bf[:, :, b_end:lr_end]
    w_gates = w_in_bf[:, :, lr_end:]
    w_glr = w_gate_lr.astype(BF16)
    w_br = w_branch.astype(BF16)
    w_o = w_out.astype(BF16)
    w_u = w_up.astype(BF16)
    w_d = w_down.astype(BF16)
    bias = _bias_tables(rel_bias)

    res = x.reshape(t, d)
    for l in range(DEPTH):
        qkva, bq, loga = _inproj(res, mix_norm_g[l][None], w_a[l], w_b[l], w_lr[l], w_glr[l],
                                 b_gate[l][None], tm)
        ya = _attention(qkva.reshape(bsz, seq, -1), bias, l)
        yb = _gla(bq.reshape(bsz, seq, -1), loga.reshape(bsz, seq, -1), gla_norm_g[l][None])
        res = _merge(res, mix_norm_g[l][None], ya.reshape(t, -1), yb.reshape(t, -1),
                     w_gates[l], w_br[l], w_o[l], tm)
        res = _mlp(res, mlp_norm_g[l][None], w_u[l], w_d[l], final_norm_g[None], tm,
                   final_norm=(l == DEPTH - 1))
    return res.reshape(bsz, seq, d)
```

```python
import functools

import jax
import jax.numpy as jnp
from jax import lax
from jax.experimental import pallas as pl
from jax.experimental.pallas import tpu as pltpu

D_MODEL = 1024
DEPTH = 4
CHUNK = 64
EPS = 1e-6

A_HEADS = 8
A_HEAD_DIM = 64
A_WIDTH = A_HEADS * A_HEAD_DIM
N_PREV_CHUNKS = 8
PREV = N_PREV_CHUNKS * CHUNK
REL_CLIP = 128
N_REL = 2 * REL_CLIP + 1

B_HEADS = 4
B_KEY_DIM = 64
B_VAL_DIM = 128
B_KEY_WIDTH = B_HEADS * B_KEY_DIM
B_WIDTH = B_HEADS * B_VAL_DIM
GATE_RANK = 16
GATE_TAU = 16.0
D_FF = 4 * D_MODEL

LANES = 128
Q_BLOCK = 2 * CHUNK
K_WINDOW = PREV + Q_BLOCK
N_WIN_TILES = K_WINDOW // 128
GLA_GROUP = 8
TOEPLITZ_W = 768
VMEM_LIMIT = 56 * 1024 * 1024

F32 = jnp.float32
BF16 = jnp.bfloat16
NEG = -1e30


def _rmsnorm_bf16(x, g):
    ms = jnp.mean(x * x, axis=-1, keepdims=True)
    return ((x * lax.rsqrt(ms + EPS)) * g).astype(BF16)


def _sigmoid(x):
    return 1.0 / (1.0 + jnp.exp(-x))


def _dot(a, b):
    return jnp.dot(a, b, preferred_element_type=F32)


def _dot_nt(a, b):
    return lax.dot_general(a, b, (((1,), (1,)), ((), ())), preferred_element_type=F32)


def _dot_tn(a, b):
    return lax.dot_general(a, b, (((0,), (0,)), ((), ())), preferred_element_type=F32)


def _const_spec(shape):
    nd = len(shape)
    return pl.BlockSpec(shape, lambda *_: (0,) * nd)


def _layer_spec(stacked, layer):
    rest = stacked.shape[1:]
    return pl.BlockSpec((None,) + rest, lambda *_: (layer,) + (0,) * len(rest))


def _bias_kernel(t_ref, o_ref):
    x = jnp.broadcast_to(t_ref[...], (Q_BLOCK, TOEPLITZ_W))
    row = lax.broadcasted_iota(jnp.int32, (Q_BLOCK, TOEPLITZ_W), 0)
    shift = 1
    while shift < Q_BLOCK:
        x = jnp.where((row & shift) != 0, pltpu.roll(x, shift, axis=1), x)
        shift *= 2
    x = x[:, :K_WINDOW]
    r = lax.broadcasted_iota(jnp.int32, (Q_BLOCK, K_WINDOW), 0)
    c = lax.broadcasted_iota(jnp.int32, (Q_BLOCK, K_WINDOW), 1)
    band_lo = (r // CHUNK) * CHUNK
    in_band = (c >= band_lo) & (c < band_lo + PREV + CHUNK)
    x = jnp.where(in_band, x, NEG)
    for j in range(K_WINDOW // LANES):
        o_ref[j * LANES:(j + 1) * LANES, :] = x[:, j * LANES:(j + 1) * LANES].T


def _bias_tables(rel_bias):
    far = rel_bias[..., N_REL - 1:]
    n_far_lo = PREV - REL_CLIP + 1
    t0 = jnp.concatenate([
        jnp.broadcast_to(far, rel_bias.shape[:-1] + (n_far_lo,)),
        rel_bias[..., N_REL - 2:0:-1],
        jnp.broadcast_to(far, rel_bias.shape[:-1] + (TOEPLITZ_W - n_far_lo - (N_REL - 2),)),
    ], axis=-1)
    n = DEPTH * A_HEADS
    t0 = t0.reshape(n, 1, TOEPLITZ_W)
    return pl.pallas_call(
        _bias_kernel,
        grid=(n,),
        in_specs=[pl.BlockSpec((None, 1, TOEPLITZ_W), lambda i: (i, 0, 0))],
        out_specs=pl.BlockSpec((None, K_WINDOW, Q_BLOCK), lambda i: (i // 2, 0, i % 2)),
        out_shape=jax.ShapeDtypeStruct((n // 2, K_WINDOW, 2 * Q_BLOCK), F32),
        name="bias_table",
    )(t0)


def _inproj_kernel(res_ref, g_ref, wa_ref, wb_ref, wlr_ref, wg_ref, bg_ref,
                   qkva_ref, bq_ref, loga_ref):
    h = _rmsnorm_bf16(res_ref[...], g_ref[...])
    qkva_ref[...] = _dot(h, wa_ref[...]).astype(BF16)
    bq_ref[...] = _dot(h, wb_ref[...])
    lr = _dot(h, wlr_ref[...])
    z = _dot(lr.astype(BF16), wg_ref[...]) + bg_ref[...]
    log_sig = jnp.minimum(z, 0.0) - jnp.log(1.0 + jnp.exp(-jnp.abs(z)))
    loga_ref[...] = log_sig / GATE_TAU


def _inproj(res, g, wa, wb, wlr, wg, bg, tm, layer):
    t = res.shape[0]
    nb = wb.shape[-1]
    return pl.pallas_call(
        _inproj_kernel,
        grid=(t // tm,),
        in_specs=[
            pl.BlockSpec((tm, D_MODEL), lambda i: (i, 0)),
            _const_spec((1, D_MODEL)),
            _layer_spec(wa, layer),
            _layer_spec(wb, layer),
            _layer_spec(wlr, layer),
            _layer_spec(wg, layer),
            _const_spec((1, B_KEY_WIDTH)),
        ],
        out_specs=[
            pl.BlockSpec((tm, 3 * A_WIDTH), lambda i: (i, 0)),
            pl.BlockSpec((tm, nb), lambda i: (i, 0)),
            pl.BlockSpec((tm, B_KEY_WIDTH), lambda i: (i, 0)),
        ],
        out_shape=[
            jax.ShapeDtypeStruct((t, 3 * A_WIDTH), BF16),
            jax.ShapeDtypeStruct((t, nb), F32),
            jax.ShapeDtypeStruct((t, B_KEY_WIDTH), F32),
        ],
        compiler_params=pltpu.CompilerParams(
            dimension_semantics=("parallel",), vmem_limit_bytes=VMEM_LIMIT),
        name="inproj",
    )(res, g, wa, wb, wlr, wg, bg)


def _attn_kernel(q_ref, k_ref, v_ref, bias_ref, o_ref, vt_scr, s0_scr, s1_scr, m0_scr, m1_scr):
    seq = q_ref.shape[0]
    n_blk = seq // Q_BLOCK
    n_prev = PREV // Q_BLOCK
    lo = lax.broadcasted_iota(jnp.int32, (1, LANES), 1) < A_HEAD_DIM

    for r in range(n_blk):
        vt_scr[r] = v_ref[r * LANES:(r + 1) * LANES, :].astype(F32).T.astype(BF16)

    def scores(qs, ks, n_tiles):
        q2 = q_ref[pl.ds(qs, Q_BLOCK), :].astype(F32) * (A_HEAD_DIM ** -0.5)
        q4 = jnp.concatenate(
            [jnp.where(lo, q2, 0.0), jnp.where(lo, 0.0, q2)], axis=0).astype(BF16)
        kw = k_ref[pl.ds(ks, n_tiles * LANES), :]
        s = _dot_nt(kw, q4) + bias_ref[(N_WIN_TILES - n_tiles) * LANES:, :]
        return s, jnp.max(s, axis=0, keepdims=True)

    def finish(s, m, qs, kt, n_tiles):
        e = jnp.exp(s - m)
        inv = 1.0 / jnp.sum(e, axis=0, keepdims=True)
        vt = jnp.concatenate([vt_scr[kt + i] for i in range(n_tiles)], axis=1)
        ot = _dot(vt, e.astype(BF16))
        o_pair = jnp.concatenate(
            [ot[:A_HEAD_DIM, :Q_BLOCK] * inv[:, :Q_BLOCK],
             ot[A_HEAD_DIM:, Q_BLOCK:] * inv[:, Q_BLOCK:]], axis=0)
        o_ref[pl.ds(qs, Q_BLOCK), :] = o_pair.T.astype(BF16)

    s, m = scores(0, 0, 1)
    for t in range(1, n_prev):
        s_next, m_next = scores(t * Q_BLOCK, 0, t + 1)
        finish(s, m, (t - 1) * Q_BLOCK, 0, t)
        s, m = s_next, m_next
    s0_scr[...], m0_scr[...] = scores(n_prev * Q_BLOCK, 0, N_WIN_TILES)
    finish(s, m, (n_prev - 1) * Q_BLOCK, 0, n_prev)

    def pair(qs0, kt0, score_next):
        s1_scr[...], m1_scr[...] = scores(qs0 + Q_BLOCK, qs0 + Q_BLOCK - PREV, N_WIN_TILES)
        finish(s0_scr[...], m0_scr[...], qs0, kt0, N_WIN_TILES)
        if score_next:
            s0_scr[...], m0_scr[...] = scores(
                qs0 + 2 * Q_BLOCK, qs0 + 2 * Q_BLOCK - PREV, N_WIN_TILES)
        finish(s1_scr[...], m1_scr[...], qs0 + Q_BLOCK, kt0 + 1, N_WIN_TILES)

    n_pairs = (n_blk - n_prev) // 2

    def body(u, carry):
        pair(pl.multiple_of((n_prev + 2 * u) * Q_BLOCK, Q_BLOCK), 2 * u, True)
        return carry

    lax.fori_loop(0, n_pairs - 1, body, 0)
    pair((n_blk - 2) * Q_BLOCK, n_blk - 2 - n_prev, False)


def _attention(qkva, bias, layer):
    bsz, seq, _ = qkva.shape
    n_pair = A_HEADS // 2
    blk = (None, seq, LANES)
    return pl.pallas_call(
        _attn_kernel,
        grid=(bsz, n_pair),
        in_specs=[
            pl.BlockSpec(blk, lambda b, p: (b, 0, p)),
            pl.BlockSpec(blk, lambda b, p: (b, 0, n_pair + p)),
            pl.BlockSpec(blk, lambda b, p: (b, 0, 2 * n_pair + p)),
            pl.BlockSpec((None, K_WINDOW, 2 * Q_BLOCK), lambda b, p: (layer * n_pair + p, 0, 0)),
        ],
        out_specs=pl.BlockSpec(blk, lambda b, p: (b, 0, p)),
        out_shape=jax.ShapeDtypeStruct((bsz, seq, A_WIDTH), BF16),
        scratch_shapes=[pltpu.VMEM((seq // LANES, LANES, LANES), BF16),
                        pltpu.VMEM((K_WINDOW, 2 * Q_BLOCK), F32),
                        pltpu.VMEM((K_WINDOW, 2 * Q_BLOCK), F32),
                        pltpu.VMEM((1, 2 * Q_BLOCK), F32),
                        pltpu.VMEM((1, 2 * Q_BLOCK), F32)],
        compiler_params=pltpu.CompilerParams(
            dimension_semantics=("parallel", "parallel"), vmem_limit_bytes=VMEM_LIMIT),
        name="band_attention",
    )(qkva, qkva, qkva, bias)


def _chunk_cumsum(x):
    row = lax.broadcasted_iota(jnp.int32, x.shape, 0) % CHUNK
    d = 1
    while d < CHUNK:
        x = x + jnp.where(row >= d, pltpu.roll(x, d, axis=0), 0.0)
        d *= 2
    return x


def _gla_kernel(q_ref, k_ref, v_ref, r_ref, la_ref, g_ref, o_ref):
    seq = q_ref.shape[0]
    lo = lax.broadcasted_iota(jnp.int32, (1, LANES), 1) < B_KEY_DIM
    ri = lax.broadcasted_iota(jnp.int32, (2 * CHUNK, CHUNK), 0)
    ci = lax.broadcasted_iota(jnp.int32, (2 * CHUNK, CHUNK), 1)
    causal = (ri % CHUNK) >= ci
    g = g_ref[...]

    def head_out(o, gh, gate):
        o = o * lax.rsqrt(jnp.mean(o * o, axis=-1, keepdims=True) + EPS) * gh
        return (o * gate).astype(BF16)

    group = GLA_GROUP * CHUNK

    def body(i, st):
        rows = pl.ds(pl.multiple_of(i * group, group), group)
        b = _chunk_cumsum(la_ref[rows, :])
        qt = q_ref[rows, :] * (B_KEY_DIM ** -0.5) * jnp.exp(b)
        k = k_ref[rows, :]
        kt = (k * jnp.exp(-b)).astype(BF16)
        q_even = jnp.where(lo, qt, 0.0).astype(BF16)
        q_odd = jnp.where(lo, 0.0, qt).astype(BF16)
        v = v_ref[rows, :].astype(BF16)
        r = r_ref[rows, :]
        gate = r * _sigmoid(r)
        outs = []
        for c in range(GLA_GROUP):
            sl = slice(c * CHUNK, (c + 1) * CHUNK)
            b_last = b[(c + 1) * CHUNK - 1:(c + 1) * CHUNK, :]
            kend = (k[sl] * jnp.exp(b_last - b[sl])).astype(BF16)
            q4 = jnp.concatenate([q_even[sl], q_odd[sl]], axis=0)
            v_even, v_odd = v[sl, :B_VAL_DIM], v[sl, B_VAL_DIM:]
            att = jnp.where(causal, _dot_nt(q4, kt[sl]), 0.0).astype(BF16)
            inter = _dot_nt(q4, st.astype(BF16))
            outs.append(jnp.concatenate(
                [_dot(att[:CHUNK], v_even) + inter[:CHUNK],
                 _dot(att[CHUNK:], v_odd) + inter[CHUNK:]], axis=1))
            d_st = jnp.where(lo, _dot_tn(v_even, kend), _dot_tn(v_odd, kend))
            st = st * jnp.exp(b_last) + d_st
        o = jnp.concatenate(outs, axis=0)
        o_ref[rows, :B_VAL_DIM] = head_out(
            o[:, :B_VAL_DIM], g[:, :B_VAL_DIM], gate[:, :B_VAL_DIM])
        o_ref[rows, B_VAL_DIM:] = head_out(
            o[:, B_VAL_DIM:], g[:, B_VAL_DIM:], gate[:, B_VAL_DIM:])
        return st

    lax.fori_loop(0, seq // group, body, jnp.zeros((B_VAL_DIM, LANES), F32))


def _gla(bq, loga, g):
    bsz, seq, _ = bq.shape
    n_pair = B_HEADS // 2
    kblk = (None, seq, LANES)
    vblk = (None, seq, 2 * B_VAL_DIM)
    k_off = B_KEY_WIDTH // LANES
    v_off = 2 * B_KEY_WIDTH // (2 * B_VAL_DIM)
    r_off = (2 * B_KEY_WIDTH + B_WIDTH) // (2 * B_VAL_DIM)
    return pl.pallas_call(
        _gla_kernel,
        grid=(bsz, n_pair),
        in_specs=[
            pl.BlockSpec(kblk, lambda b, p: (b, 0, p)),
            pl.BlockSpec(kblk, lambda b, p: (b, 0, k_off + p)),
            pl.BlockSpec(vblk, lambda b, p: (b, 0, v_off + p)),
            pl.BlockSpec(vblk, lambda b, p: (b, 0, r_off + p)),
            pl.BlockSpec(kblk, lambda b, p: (b, 0, p)),
            pl.BlockSpec((1, 2 * B_VAL_DIM), lambda b, p: (0, p)),
        ],
        out_specs=pl.BlockSpec(vblk, lambda b, p: (b, 0, p)),
        out_shape=jax.ShapeDtypeStruct((bsz, seq, B_WIDTH), BF16),
        compiler_params=pltpu.CompilerParams(
            dimension_semantics=("parallel", "parallel"), vmem_limit_bytes=VMEM_LIMIT),
        name="gla",
    )(bq, bq, bq, bq, loga, g)


def _merge_kernel(res_ref, g_ref, ya_ref, yb_ref, wgate_ref, wbr_ref, wout_ref, o_ref):
    res = res_ref[...]
    h = _rmsnorm_bf16(res, g_ref[...])
    gates = _dot(h, wgate_ref[...])
    u_a = _dot(ya_ref[...], wbr_ref[0])
    u_b = _dot(yb_ref[...], wbr_ref[1])
    merged = _sigmoid(gates[:, :D_MODEL]) * u_a + _sigmoid(gates[:, D_MODEL:]) * u_b
    o_ref[...] = res + _dot(merged.astype(BF16), wout_ref[...])


def _merge(res, g, ya, yb, wgate, wbr, wout, tm, layer):
    t = res.shape[0]
    return pl.pallas_call(
        _merge_kernel,
        grid=(t // tm,),
        in_specs=[
            pl.BlockSpec((tm, D_MODEL), lambda i: (i, 0)),
            _const_spec((1, D_MODEL)),
            pl.BlockSpec((tm, A_WIDTH), lambda i: (i, 0)),
            pl.BlockSpec((tm, B_WIDTH), lambda i: (i, 0)),
            _layer_spec(wgate, layer),
            _layer_spec(wbr, layer),
            _layer_spec(wout, layer),
        ],
        out_specs=pl.BlockSpec((tm, D_MODEL), lambda i: (i, 0)),
        out_shape=jax.ShapeDtypeStruct((t, D_MODEL), F32),
        compiler_params=pltpu.CompilerParams(
            dimension_semantics=("parallel",), vmem_limit_bytes=VMEM_LIMIT),
        name="merge",
    )(res, g, ya, yb, wgate, wbr, wout)


def _mlp_kernel(res_ref, g_ref, wup_ref, wdn_ref, gf_ref, o_ref, *, final_norm):
    res = res_ref[...]
    h = _rmsnorm_bf16(res, g_ref[...])
    acc = res
    for c in range(D_FF // D_MODEL):
        cols = slice(c * D_MODEL, (c + 1) * D_MODEL)
        up = jnp.maximum(_dot(h, wup_ref[:, cols]), 0.0)
        acc = acc + _dot((up * up).astype(BF16), wdn_ref[cols, :])
    if final_norm:
        ms = jnp.mean(acc * acc, axis=-1, keepdims=True)
        acc = (acc * lax.rsqrt(ms + EPS)) * gf_ref[...]
    o_ref[...] = acc


def _mlp(res, g, wup, wdn, gf, tm, layer, final_norm):
    t = res.shape[0]
    return pl.pallas_call(
        functools.partial(_mlp_kernel, final_norm=final_norm),
        grid=(t // tm,),
        in_specs=[
            pl.BlockSpec((tm, D_MODEL), lambda i: (i, 0)),
            _const_spec((1, D_MODEL)),
            _layer_spec(wup, layer),
            _layer_spec(wdn, layer),
            _const_spec((1, D_MODEL)),
        ],
        out_specs=pl.BlockSpec((tm, D_MODEL), lambda i: (i, 0)),
        out_shape=jax.ShapeDtypeStruct((t, D_MODEL), F32),
        compiler_params=pltpu.CompilerParams(
            dimension_semantics=("parallel",), vmem_limit_bytes=VMEM_LIMIT),
        name="mlp",
    )(res, g, wup, wdn, gf)


def kernel(x, mix_norm_g, w_in, rel_bias, w_gate_lr, b_gate, gla_norm_g, w_branch, w_out,
           mlp_norm_g, w_up, w_down, final_norm_g):
    bsz, seq, d = x.shape
    t = bsz * seq
    tm = 512
    a_end = 3 * A_WIDTH
    b_end = a_end + 2 * B_KEY_WIDTH + 2 * B_WIDTH
    lr_end = b_end + GATE_RANK
    w_a = w_in[:, :, :a_end].astype(BF16)
    w_b = w_in[:, :, a_end:b_end].astype(BF16)
    w_lr = w_in[:, :, b_end:lr_end].astype(BF16)
    w_gates = w_in[:, :, lr_end:].astype(BF16)
    w_glr = w_gate_lr.astype(BF16)
    w_br = w_branch.astype(BF16)
    w_o = w_out.astype(BF16)
    w_u = w_up.astype(BF16)
    w_d = w_down.astype(BF16)
    bias = _bias_tables(rel_bias)

    res = x.reshape(t, d)
    for l in range(DEPTH):
        qkva, bq, loga = _inproj(res, mix_norm_g[l][None], w_a, w_b, w_lr, w_glr,
                                 b_gate[l][None], tm, l)
        ya = _attention(qkva.reshape(bsz, seq, -1), bias, l)
        yb = _gla(bq.reshape(bsz, seq, -1), loga.reshape(bsz, seq, -1), gla_norm_g[l][None])
        res = _merge(res, mix_norm_g[l][None], ya.reshape(t, -1), yb.reshape(t, -1),
                     w_gates, w_br, w_o, tm, l)
        res = _mlp(res, mlp_norm_g[l][None], w_u, w_d, final_norm_g[None], tm, l,
                   final_norm=(l == DEPTH - 1))
    return res.reshape(bsz, seq, d)
```

```python
import functools

import jax
import jax.numpy as jnp
from jax import lax
from jax.experimental import pallas as pl
from jax.experimental.pallas import tpu as pltpu

D_MODEL = 1024
DEPTH = 4
CHUNK = 64
EPS = 1e-6

A_HEADS = 8
A_HEAD_DIM = 64
A_WIDTH = A_HEADS * A_HEAD_DIM
N_PREV_CHUNKS = 8
PREV = N_PREV_CHUNKS * CHUNK
REL_CLIP = 128
N_REL = 2 * REL_CLIP + 1

B_HEADS = 4
B_KEY_DIM = 64
B_VAL_DIM = 128
B_KEY_WIDTH = B_HEADS * B_KEY_DIM
B_WIDTH = B_HEADS * B_VAL_DIM
GATE_RANK = 16
GATE_TAU = 16.0
D_FF = 4 * D_MODEL

LANES = 128
Q_BLOCK = 2 * CHUNK
K_WINDOW = PREV + Q_BLOCK
N_WIN_TILES = K_WINDOW // 128
VT_ROWS = 128 + 16
N_SLOTS = 2
ZERO_BIAS_TILES = tuple(range(1, (PREV - REL_CLIP) // 128))
LOG2E = 1.4426950408889634
GLA_GROUP = 8
TOEPLITZ_W = 768
VMEM_LIMIT = 56 * 1024 * 1024

F32 = jnp.float32
BF16 = jnp.bfloat16
NEG = -1e30


def _rmsnorm_bf16(x, g):
    ms = jnp.mean(x * x, axis=-1, keepdims=True)
    return ((x * lax.rsqrt(ms + EPS)) * g).astype(BF16)


def _sigmoid(x):
    return 1.0 / (1.0 + jnp.exp(-x))


def _dot(a, b):
    return jnp.dot(a, b, preferred_element_type=F32)


def _dot_nt(a, b):
    return lax.dot_general(a, b, (((1,), (1,)), ((), ())), preferred_element_type=F32)


def _dot_tn(a, b):
    return lax.dot_general(a, b, (((0,), (0,)), ((), ())), preferred_element_type=F32)


def _const_spec(shape):
    nd = len(shape)
    return pl.BlockSpec(shape, lambda *_: (0,) * nd)


def _layer_spec(stacked, layer):
    rest = stacked.shape[1:]
    return pl.BlockSpec((None,) + rest, lambda *_: (layer,) + (0,) * len(rest))


def _bias_kernel(t_ref, o_ref):
    t = t_ref[...]
    t = (t - t[:, 0:1]) * LOG2E
    x = jnp.broadcast_to(t, (Q_BLOCK, TOEPLITZ_W))
    row = lax.broadcasted_iota(jnp.int32, (Q_BLOCK, TOEPLITZ_W), 0)
    shift = 1
    while shift < Q_BLOCK:
        x = jnp.where((row & shift) != 0, pltpu.roll(x, shift, axis=1), x)
        shift *= 2
    x = x[:, :K_WINDOW]
    r = lax.broadcasted_iota(jnp.int32, (Q_BLOCK, K_WINDOW), 0)
    c = lax.broadcasted_iota(jnp.int32, (Q_BLOCK, K_WINDOW), 1)
    band_lo = (r // CHUNK) * CHUNK
    in_band = (c >= band_lo) & (c < band_lo + PREV + CHUNK)
    x = jnp.where(in_band, x, NEG)
    for j in range(K_WINDOW // LANES):
        o_ref[j * LANES:(j + 1) * LANES, :] = x[:, j * LANES:(j + 1) * LANES].T


def _bias_tables(rel_bias):
    far = rel_bias[..., N_REL - 1:]
    n_far_lo = PREV - REL_CLIP + 1
    t0 = jnp.concatenate([
        jnp.broadcast_to(far, rel_bias.shape[:-1] + (n_far_lo,)),
        rel_bias[..., N_REL - 2:0:-1],
        jnp.broadcast_to(far, rel_bias.shape[:-1] + (TOEPLITZ_W - n_far_lo - (N_REL - 2),)),
    ], axis=-1)
    n = DEPTH * A_HEADS
    t0 = t0.reshape(n, 1, TOEPLITZ_W)
    return pl.pallas_call(
        _bias_kernel,
        grid=(n,),
        in_specs=[pl.BlockSpec((None, 1, TOEPLITZ_W), lambda i: (i, 0, 0))],
        out_specs=pl.BlockSpec((None, K_WINDOW, Q_BLOCK), lambda i: (i // 2, 0, i % 2)),
        out_shape=jax.ShapeDtypeStruct((n // 2, K_WINDOW, 2 * Q_BLOCK), F32),
        name="bias_table",
    )(t0)


def _inproj_kernel(res_ref, g_ref, wa_ref, wb_ref, wlr_ref, wg_ref, bg_ref,
                   qkva_ref, bq_ref, loga_ref):
    h = _rmsnorm_bf16(res_ref[...], g_ref[...])
    qkva_ref[...] = _dot(h, wa_ref[...]).astype(BF16)
    bq_ref[...] = _dot(h, wb_ref[...])
    lr = _dot(h, wlr_ref[...])
    z = _dot(lr.astype(BF16), wg_ref[...]) + bg_ref[...]
    log_sig = jnp.minimum(z, 0.0) - jnp.log(1.0 + jnp.exp(-jnp.abs(z)))
    loga_ref[...] = log_sig / GATE_TAU


def _inproj(res, g, wa, wb, wlr, wg, bg, tm, layer):
    t = res.shape[0]
    nb = wb.shape[-1]
    return pl.pallas_call(
        _inproj_kernel,
        grid=(t // tm,),
        in_specs=[
            pl.BlockSpec((tm, D_MODEL), lambda i: (i, 0)),
            _const_spec((1, D_MODEL)),
            _layer_spec(wa, layer),
            _layer_spec(wb, layer),
            _layer_spec(wlr, layer),
            _layer_spec(wg, layer),
            _const_spec((1, B_KEY_WIDTH)),
        ],
        out_specs=[
            pl.BlockSpec((tm, 3 * A_WIDTH), lambda i: (i, 0)),
            pl.BlockSpec((tm, nb), lambda i: (i, 0)),
            pl.BlockSpec((tm, B_KEY_WIDTH), lambda i: (i, 0)),
        ],
        out_shape=[
            jax.ShapeDtypeStruct((t, 3 * A_WIDTH), BF16),
            jax.ShapeDtypeStruct((t, nb), F32),
            jax.ShapeDtypeStruct((t, B_KEY_WIDTH), F32),
        ],
        compiler_params=pltpu.CompilerParams(
            dimension_semantics=("parallel",), vmem_limit_bytes=VMEM_LIMIT),
        name="inproj",
    )(res, g, wa, wb, wlr, wg, bg)


def _attn_kernel(q_ref, k_ref, v_ref, bias_ref, o_ref, vt_scr, s_scr, m_scr, e_scr):
    seq = q_ref.shape[0]
    n_blk = seq // Q_BLOCK
    n_prev = PREV // Q_BLOCK
    lo = lax.broadcasted_iota(jnp.int32, (1, LANES), 1) < A_HEAD_DIM

    for r in range(n_blk):
        vt_scr[r, :LANES, :] = v_ref[r * LANES:(r + 1) * LANES, :].astype(F32).T.astype(BF16)
        vt_scr[r, LANES:, :] = jnp.ones((VT_ROWS - LANES, LANES), BF16)

    def window(t):
        return max(t - n_prev, 0), min(t + 1, N_WIN_TILES)

    def scores(t):
        kt, n_tiles = window(t)
        rows = n_tiles * LANES
        q2 = q_ref[t * Q_BLOCK:(t + 1) * Q_BLOCK, :].astype(F32) * (A_HEAD_DIM ** -0.5 * LOG2E)
        q4 = jnp.concatenate(
            [jnp.where(lo, q2, 0.0), jnp.where(lo, 0.0, q2)], axis=0).astype(BF16)
        kw = k_ref[kt * LANES:kt * LANES + rows, :]
        s = _dot_nt(kw, q4)
        m = None
        for i in range(n_tiles):
            tile = s[i * LANES:(i + 1) * LANES, :]
            w = N_WIN_TILES - n_tiles + i
            if w not in ZERO_BIAS_TILES:
                tile = tile + bias_ref[w * LANES:(w + 1) * LANES, :]
            s_scr[t % N_SLOTS, i * LANES:(i + 1) * LANES, :] = tile
            tile_max = jnp.max(tile.reshape(LANES // 8, 8, 2 * Q_BLOCK), axis=0)
            m = tile_max if m is None else jnp.maximum(m, tile_max)
        m_scr[t % N_SLOTS] = jnp.max(m, axis=0, keepdims=True)

    def exps(t):
        rows = window(t)[1] * LANES
        e = jnp.exp2(s_scr[t % N_SLOTS, :rows, :] - m_scr[t % N_SLOTS])
        e_scr[t % N_SLOTS, :rows, :] = e.astype(BF16)

    def weighted_sum(t):
        kt, n_tiles = window(t)
        vt = jnp.concatenate([vt_scr[kt + i] for i in range(n_tiles)], axis=1)
        ot = _dot(vt, e_scr[t % N_SLOTS, :n_tiles * LANES, :])
        inv = 1.0 / ot[LANES:LANES + 1, :]
        o_pair = jnp.concatenate(
            [ot[:A_HEAD_DIM, :Q_BLOCK] * inv[:, :Q_BLOCK],
             ot[A_HEAD_DIM:LANES, Q_BLOCK:] * inv[:, Q_BLOCK:]], axis=0)
        o_ref[t * Q_BLOCK:(t + 1) * Q_BLOCK, :] = o_pair.T.astype(BF16)

    for step in range(n_blk + 2):
        if step < n_blk:
            scores(step)
        if 0 <= step - 1 < n_blk:
            exps(step - 1)
        if 0 <= step - 2 < n_blk:
            weighted_sum(step - 2)


def _attention(qkva, bias, layer):
    bsz, seq, _ = qkva.shape
    n_pair = A_HEADS // 2
    blk = (None, seq, LANES)
    return pl.pallas_call(
        _attn_kernel,
        grid=(bsz, n_pair),
        in_specs=[
            pl.BlockSpec(blk, lambda b, p: (b, 0, p)),
            pl.BlockSpec(blk, lambda b, p: (b, 0, n_pair + p)),
            pl.BlockSpec(blk, lambda b, p: (b, 0, 2 * n_pair + p)),
            pl.BlockSpec((None, K_WINDOW, 2 * Q_BLOCK), lambda b, p: (layer * n_pair + p, 0, 0)),
        ],
        out_specs=pl.BlockSpec(blk, lambda b, p: (b, 0, p)),
        out_shape=jax.ShapeDtypeStruct((bsz, seq, A_WIDTH), BF16),
        scratch_shapes=[pltpu.VMEM((seq // LANES, VT_ROWS, LANES), BF16),
                        pltpu.VMEM((N_SLOTS, K_WINDOW, 2 * Q_BLOCK), F32),
                        pltpu.VMEM((N_SLOTS, 1, 2 * Q_BLOCK), F32),
                        pltpu.VMEM((N_SLOTS, K_WINDOW, 2 * Q_BLOCK), BF16)],
        compiler_params=pltpu.CompilerParams(
            dimension_semantics=("parallel", "parallel"), vmem_limit_bytes=VMEM_LIMIT),
        name="band_attention",
    )(qkva, qkva, qkva, bias)


def _chunk_cumsum(x):
    row = lax.broadcasted_iota(jnp.int32, x.shape, 0) % CHUNK
    d = 1
    while d < CHUNK:
        x = x + jnp.where(row >= d, pltpu.roll(x, d, axis=0), 0.0)
        d *= 2
    return x


def _gla_kernel(q_ref, k_ref, v_ref, r_ref, la_ref, g_ref, o_ref):
    seq = q_ref.shape[0]
    lo = lax.broadcasted_iota(jnp.int32, (1, LANES), 1) < B_KEY_DIM
    ri = lax.broadcasted_iota(jnp.int32, (2 * CHUNK, CHUNK), 0)
    ci = lax.broadcasted_iota(jnp.int32, (2 * CHUNK, CHUNK), 1)
    causal = (ri % CHUNK) >= ci
    g = g_ref[...]

    def head_out(o, gh, gate):
        o = o * lax.rsqrt(jnp.mean(o * o, axis=-1, keepdims=True) + EPS) * gh
        return (o * gate).astype(BF16)

    group = GLA_GROUP * CHUNK

    def body(i, st):
        rows = pl.ds(pl.multiple_of(i * group, group), group)
        b = _chunk_cumsum(la_ref[rows, :])
        qt = q_ref[rows, :] * (B_KEY_DIM ** -0.5) * jnp.exp(b)
        k = k_ref[rows, :]
        kt = (k * jnp.exp(-b)).astype(BF16)
        q_even = jnp.where(lo, qt, 0.0).astype(BF16)
        q_odd = jnp.where(lo, 0.0, qt).astype(BF16)
        v = v_ref[rows, :].astype(BF16)
        r = r_ref[rows, :]
        gate = r * _sigmoid(r)
        outs = []
        for c in range(GLA_GROUP):
            sl = slice(c * CHUNK, (c + 1) * CHUNK)
            b_last = b[(c + 1) * CHUNK - 1:(c + 1) * CHUNK, :]
            kend = (k[sl] * jnp.exp(b_last - b[sl])).astype(BF16)
            q4 = jnp.concatenate([q_even[sl], q_odd[sl]], axis=0)
            v_even, v_odd = v[sl, :B_VAL_DIM], v[sl, B_VAL_DIM:]
            att = jnp.where(causal, _dot_nt(q4, kt[sl]), 0.0).astype(BF16)
            inter = _dot_nt(q4, st.astype(BF16))
            outs.append(jnp.concatenate(
                [_dot(att[:CHUNK], v_even) + inter[:CHUNK],
                 _dot(att[CHUNK:], v_odd) + inter[CHUNK:]], axis=1))
            d_st = jnp.where(lo, _dot_tn(v_even, kend), _dot_tn(v_odd, kend))
            st = st * jnp.exp(b_last) + d_st
        o = jnp.concatenate(outs, axis=0)
        o_ref[rows, :B_VAL_DIM] = head_out(
            o[:, :B_VAL_DIM], g[:, :B_VAL_DIM], gate[:, :B_VAL_DIM])
        o_ref[rows, B_VAL_DIM:] = head_out(
            o[:, B_VAL_DIM:], g[:, B_VAL_DIM:], gate[:, B_VAL_DIM:])
        return st

    lax.fori_loop(0, seq // group, body, jnp.zeros((B_VAL_DIM, LANES), F32))


def _gla(bq, loga, g):
    bsz, seq, _ = bq.shape
    n_pair = B_HEADS // 2
    kblk = (None, seq, LANES)
    vblk = (None, seq, 2 * B_VAL_DIM)
    k_off = B_KEY_WIDTH // LANES
    v_off = 2 * B_KEY_WIDTH // (2 * B_VAL_DIM)
    r_off = (2 * B_KEY_WIDTH + B_WIDTH) // (2 * B_VAL_DIM)
    return pl.pallas_call(
        _gla_kernel,
        grid=(bsz, n_pair),
        in_specs=[
            pl.BlockSpec(kblk, lambda b, p: (b, 0, p)),
            pl.BlockSpec(kblk, lambda b, p: (b, 0, k_off + p)),
            pl.BlockSpec(vblk, lambda b, p: (b, 0, v_off + p)),
            pl.BlockSpec(vblk, lambda b, p: (b, 0, r_off + p)),
            pl.BlockSpec(kblk, lambda b, p: (b, 0, p)),
            pl.BlockSpec((1, 2 * B_VAL_DIM), lambda b, p: (0, p)),
        ],
        out_specs=pl.BlockSpec(vblk, lambda b, p: (b, 0, p)),
        out_shape=jax.ShapeDtypeStruct((bsz, seq, B_WIDTH), BF16),
        compiler_params=pltpu.CompilerParams(
            dimension_semantics=("parallel", "parallel"), vmem_limit_bytes=VMEM_LIMIT),
        name="gla",
    )(bq, bq, bq, bq, loga, g)


def _merge_kernel(res_ref, g_ref, ya_ref, yb_ref, wgate_ref, wbr_ref, wout_ref, o_ref):
    res = res_ref[...]
    h = _rmsnorm_bf16(res, g_ref[...])
    gates = _dot(h, wgate_ref[...])
    u_a = _dot(ya_ref[...], wbr_ref[0])
    u_b = _dot(yb_ref[...], wbr_ref[1])
    merged = _sigmoid(gates[:, :D_MODEL]) * u_a + _sigmoid(gates[:, D_MODEL:]) * u_b
    o_ref[...] = res + _dot(merged.astype(BF16), wout_ref[...])


def _merge(res, g, ya, yb, wgate, wbr, wout, tm, layer):
    t = res.shape[0]
    return pl.pallas_call(
        _merge_kernel,
        grid=(t // tm,),
        in_specs=[
            pl.BlockSpec((tm, D_MODEL), lambda i: (i, 0)),
            _const_spec((1, D_MODEL)),
            pl.BlockSpec((tm, A_WIDTH), lambda i: (i, 0)),
            pl.BlockSpec((tm, B_WIDTH), lambda i: (i, 0)),
            _layer_spec(wgate, layer),
            _layer_spec(wbr, layer),
            _layer_spec(wout, layer),
        ],
        out_specs=pl.BlockSpec((tm, D_MODEL), lambda i: (i, 0)),
        out_shape=jax.ShapeDtypeStruct((t, D_MODEL), F32),
        compiler_params=pltpu.CompilerParams(
            dimension_semantics=("parallel",), vmem_limit_bytes=VMEM_LIMIT),
        name="merge",
    )(res, g, ya, yb, wgate, wbr, wout)


def _mlp_kernel(res_ref, g_ref, wup_ref, wdn_ref, gf_ref, o_ref, *, final_norm):
    res = res_ref[...]
    h = _rmsnorm_bf16(res, g_ref[...])
    acc = res
    for c in range(D_FF // D_MODEL):
        cols = slice(c * D_MODEL, (c + 1) * D_MODEL)
        up = jnp.maximum(_dot(h, wup_ref[:, cols]), 0.0)
        acc = acc + _dot((up * up).astype(BF16), wdn_ref[cols, :])
    if final_norm:
        ms = jnp.mean(acc * acc, axis=-1, keepdims=True)
        acc = (acc * lax.rsqrt(ms + EPS)) * gf_ref[...]
    o_ref[...] = acc


def _mlp(res, g, wup, wdn, gf, tm, layer, final_norm):
    t = res.shape[0]
    return pl.pallas_call(
        functools.partial(_mlp_kernel, final_norm=final_norm),
        grid=(t // tm,),
        in_specs=[
            pl.BlockSpec((tm, D_MODEL), lambda i: (i, 0)),
            _const_spec((1, D_MODEL)),
            _layer_spec(wup, layer),
            _layer_spec(wdn, layer),
            _const_spec((1, D_MODEL)),
        ],
        out_specs=pl.BlockSpec((tm, D_MODEL), lambda i: (i, 0)),
        out_shape=jax.ShapeDtypeStruct((t, D_MODEL), F32),
        compiler_params=pltpu.CompilerParams(
            dimension_semantics=("parallel",), vmem_limit_bytes=VMEM_LIMIT),
        name="mlp",
    )(res, g, wup, wdn, gf)


def kernel(x, mix_norm_g, w_in, rel_bias, w_gate_lr, b_gate, gla_norm_g, w_branch, w_out,
           mlp_norm_g, w_up, w_down, final_norm_g):
    bsz, seq, d = x.shape
    t = bsz * seq
    tm = 512
    a_end = 3 * A_WIDTH
    b_end = a_end + 2 * B_KEY_WIDTH + 2 * B_WIDTH
    lr_end = b_end + GATE_RANK
    w_a = w_in[:, :, :a_end].astype(BF16)
    w_b = w_in[:, :, a_end:b_end].astype(BF16)
    w_lr = w_in[:, :, b_end:lr_end].astype(BF16)
    w_gates = w_in[:, :, lr_end:].astype(BF16)
    w_glr = w_gate_lr.astype(BF16)
    w_br = w_branch.astype(BF16)
    w_o = w_out.astype(BF16)
    w_u = w_up.astype(BF16)
    w_d = w_down.astype(BF16)
    bias = _bias_tables(rel_bias)

    res = x.reshape(t, d)
    for l in range(DEPTH):
        qkva, bq, loga = _inproj(res, mix_norm_g[l][None], w_a, w_b, w_lr, w_glr,
                                 b_gate[l][None], tm, l)
        ya = _attention(qkva.reshape(bsz, seq, -1), bias, l)
        yb = _gla(bq.reshape(bsz, seq, -1), loga.reshape(bsz, seq, -1), gla_norm_g[l][None])
        res = _merge(res, mix_norm_g[l][None], ya.reshape(t, -1), yb.reshape(t, -1),
                     w_gates, w_br, w_o, tm, l)
        res = _mlp(res, mlp_norm_g[l][None], w_u, w_d, final_norm_g[None], tm, l,
                   final_norm=(l == DEPTH - 1))
    return res.reshape(bsz, seq, d)
```

```python
import functools

import jax
import jax.numpy as jnp
from jax import lax
from jax.experimental import pallas as pl
from jax.experimental.pallas import tpu as pltpu

D_MODEL = 1024
DEPTH = 4
CHUNK = 64
EPS = 1e-6

A_HEADS = 8
A_HEAD_DIM = 64
A_WIDTH = A_HEADS * A_HEAD_DIM
N_PREV_CHUNKS = 8
PREV = N_PREV_CHUNKS * CHUNK
REL_CLIP = 128
N_REL = 2 * REL_CLIP + 1

B_HEADS = 4
B_KEY_DIM = 64
B_VAL_DIM = 128
B_KEY_WIDTH = B_HEADS * B_KEY_DIM
B_WIDTH = B_HEADS * B_VAL_DIM
GATE_RANK = 16
GATE_TAU = 16.0
D_FF = 4 * D_MODEL

LANES = 128
Q_BLOCK = 2 * CHUNK
K_WINDOW = PREV + Q_BLOCK
N_WIN_TILES = K_WINDOW // 128
VT_ROWS = 128 + 16
N_SLOTS = 2
ZERO_BIAS_TILES = tuple(range(1, (PREV - REL_CLIP) // 128))
LOG2E = 1.4426950408889634
GLA_GROUP = 8
TOEPLITZ_W = 768
VMEM_LIMIT = 56 * 1024 * 1024

F32 = jnp.float32
BF16 = jnp.bfloat16
NEG = -1e30


def _rmsnorm_bf16(x, g):
    ms = jnp.mean(x * x, axis=-1, keepdims=True)
    return ((x * lax.rsqrt(ms + EPS)) * g).astype(BF16)


def _sigmoid(x):
    return 1.0 / (1.0 + jnp.exp(-x))


def _dot(a, b):
    return jnp.dot(a, b, preferred_element_type=F32)


def _dot_nt(a, b):
    return lax.dot_general(a, b, (((1,), (1,)), ((), ())), preferred_element_type=F32)


def _dot_tn(a, b):
    return lax.dot_general(a, b, (((0,), (0,)), ((), ())), preferred_element_type=F32)


def _const_spec(shape):
    nd = len(shape)
    return pl.BlockSpec(shape, lambda *_: (0,) * nd)


def _layer_spec(stacked, layer):
    rest = stacked.shape[1:]
    return pl.BlockSpec((None,) + rest, lambda *_: (layer,) + (0,) * len(rest))


def _bias_kernel(t_ref, o_ref):
    t = t_ref[...]
    t = (t - t[:, 0:1]) * LOG2E
    x = jnp.broadcast_to(t, (Q_BLOCK, TOEPLITZ_W))
    row = lax.broadcasted_iota(jnp.int32, (Q_BLOCK, TOEPLITZ_W), 0)
    shift = 1
    while shift < Q_BLOCK:
        x = jnp.where((row & shift) != 0, pltpu.roll(x, shift, axis=1), x)
        shift *= 2
    x = x[:, :K_WINDOW]
    r = lax.broadcasted_iota(jnp.int32, (Q_BLOCK, K_WINDOW), 0)
    c = lax.broadcasted_iota(jnp.int32, (Q_BLOCK, K_WINDOW), 1)
    band_lo = (r // CHUNK) * CHUNK
    in_band = (c >= band_lo) & (c < band_lo + PREV + CHUNK)
    x = jnp.where(in_band, x, NEG)
    for j in range(K_WINDOW // LANES):
        o_ref[j * LANES:(j + 1) * LANES, :] = x[:, j * LANES:(j + 1) * LANES].T


def _bias_tables(rel_bias):
    far = rel_bias[..., N_REL - 1:]
    n_far_lo = PREV - REL_CLIP + 1
    t0 = jnp.concatenate([
        jnp.broadcast_to(far, rel_bias.shape[:-1] + (n_far_lo,)),
        rel_bias[..., N_REL - 2:0:-1],
        jnp.broadcast_to(far, rel_bias.shape[:-1] + (TOEPLITZ_W - n_far_lo - (N_REL - 2),)),
    ], axis=-1)
    n = DEPTH * A_HEADS
    t0 = t0.reshape(n, 1, TOEPLITZ_W)
    return pl.pallas_call(
        _bias_kernel,
        grid=(n,),
        in_specs=[pl.BlockSpec((None, 1, TOEPLITZ_W), lambda i: (i, 0, 0))],
        out_specs=pl.BlockSpec((None, K_WINDOW, Q_BLOCK), lambda i: (i // 2, 0, i % 2)),
        out_shape=jax.ShapeDtypeStruct((n // 2, K_WINDOW, 2 * Q_BLOCK), F32),
        name="bias_table",
    )(t0)


def _inproj_kernel(res_ref, g_ref, wa_ref, wb_ref, wlr_ref, wg_ref, bg_ref,
                   qkva_ref, gla_ref, gate_ref, decay_ref, vt_ref):
    tm = res_ref.shape[0]
    n_chunk = tm // CHUNK
    kw = B_KEY_WIDTH
    h = _rmsnorm_bf16(res_ref[...], g_ref[...])
    lr = _dot(h, wlr_ref[...])
    z = _dot(lr.astype(BF16), wg_ref[...]) + bg_ref[...]
    log_sig = jnp.minimum(z, 0.0) - jnp.log(1.0 + jnp.exp(-jnp.abs(z)))
    b = _chunk_cumsum(log_sig / GATE_TAU)
    b3 = b.reshape(n_chunk, CHUNK, kw)
    b_last = b3[:, CHUNK - 1:CHUNK, :]
    decay_ref[...] = jnp.exp(b_last).reshape(n_chunk, kw)
    to_end = jnp.exp(b_last - b3).reshape(tm, kw)
    pb = _dot(h, wb_ref[...])
    qt = pb[:, :kw] * (B_KEY_DIM ** -0.5) * jnp.exp(b)
    k = pb[:, kw:2 * kw]
    even = (lax.broadcasted_iota(jnp.int32, (1, kw), 1) // B_KEY_DIM) % 2 == 0
    gla_ref[:, :kw] = jnp.where(even, qt, 0.0).astype(BF16)
    gla_ref[:, kw:2 * kw] = jnp.where(even, 0.0, qt).astype(BF16)
    gla_ref[:, 2 * kw:3 * kw] = (k * jnp.exp(-b)).astype(BF16)
    gla_ref[:, 3 * kw:4 * kw] = (k * to_end).astype(BF16)
    v = pb[:, 2 * kw:2 * kw + B_WIDTH]
    gla_ref[:, 4 * kw:] = v.astype(BF16)
    vt_ref[...] = v.T.astype(BF16)
    r = pb[:, 2 * kw + B_WIDTH:]
    gate_ref[...] = r * _sigmoid(r)
    qkva_ref[...] = _dot(h, wa_ref[...]).astype(BF16)


def _inproj(res, g, wa, wb, wlr, wg, bg, tm, layer):
    t = res.shape[0]
    gla_cols = 4 * B_KEY_WIDTH + B_WIDTH
    return pl.pallas_call(
        _inproj_kernel,
        grid=(t // tm,),
        in_specs=[
            pl.BlockSpec((tm, D_MODEL), lambda i: (i, 0)),
            _const_spec((1, D_MODEL)),
            _layer_spec(wa, layer),
            _layer_spec(wb, layer),
            _layer_spec(wlr, layer),
            _layer_spec(wg, layer),
            _const_spec((1, B_KEY_WIDTH)),
        ],
        out_specs=[
            pl.BlockSpec((tm, 3 * A_WIDTH), lambda i: (i, 0)),
            pl.BlockSpec((tm, gla_cols), lambda i: (i, 0)),
            pl.BlockSpec((tm, B_WIDTH), lambda i: (i, 0)),
            pl.BlockSpec((tm // CHUNK, B_KEY_WIDTH), lambda i: (i, 0)),
            pl.BlockSpec((None, B_WIDTH, tm), lambda i: (i, 0, 0)),
        ],
        out_shape=[
            jax.ShapeDtypeStruct((t, 3 * A_WIDTH), BF16),
            jax.ShapeDtypeStruct((t, gla_cols), BF16),
            jax.ShapeDtypeStruct((t, B_WIDTH), F32),
            jax.ShapeDtypeStruct((t // CHUNK, B_KEY_WIDTH), F32),
            jax.ShapeDtypeStruct((t // tm, B_WIDTH, tm), BF16),
        ],
        compiler_params=pltpu.CompilerParams(
            dimension_semantics=("parallel",), vmem_limit_bytes=VMEM_LIMIT),
        name="inproj",
    )(res, g, wa, wb, wlr, wg, bg)


def _attn_kernel(q_ref, k_ref, v_ref, bias_ref, o_ref, vt_scr, s_scr, m_scr, e_scr):
    seq = q_ref.shape[0]
    n_blk = seq // Q_BLOCK
    n_prev = PREV // Q_BLOCK
    lo = lax.broadcasted_iota(jnp.int32, (1, LANES), 1) < A_HEAD_DIM

    for r in range(n_blk):
        vt_scr[r, :LANES, :] = v_ref[r * LANES:(r + 1) * LANES, :].astype(F32).T.astype(BF16)
        vt_scr[r, LANES:, :] = jnp.ones((VT_ROWS - LANES, LANES), BF16)

    def window(t):
        return max(t - n_prev, 0), min(t + 1, N_WIN_TILES)

    def scores(t):
        kt, n_tiles = window(t)
        rows = n_tiles * LANES
        q2 = q_ref[t * Q_BLOCK:(t + 1) * Q_BLOCK, :].astype(F32) * (A_HEAD_DIM ** -0.5 * LOG2E)
        q4 = jnp.concatenate(
            [jnp.where(lo, q2, 0.0), jnp.where(lo, 0.0, q2)], axis=0).astype(BF16)
        kw = k_ref[kt * LANES:kt * LANES + rows, :]
        s = _dot_nt(kw, q4)
        m = None
        for i in range(n_tiles):
            tile = s[i * LANES:(i + 1) * LANES, :]
            w = N_WIN_TILES - n_tiles + i
            if w not in ZERO_BIAS_TILES:
                tile = tile + bias_ref[w * LANES:(w + 1) * LANES, :]
            s_scr[t % N_SLOTS, i * LANES:(i + 1) * LANES, :] = tile
            tile_max = jnp.max(tile.reshape(LANES // 8, 8, 2 * Q_BLOCK), axis=0)
            m = tile_max if m is None else jnp.maximum(m, tile_max)
        m_scr[t % N_SLOTS] = jnp.max(m, axis=0, keepdims=True)

    def exps(t):
        rows = window(t)[1] * LANES
        e = jnp.exp2(s_scr[t % N_SLOTS, :rows, :] - m_scr[t % N_SLOTS])
        e_scr[t % N_SLOTS, :rows, :] = e.astype(BF16)

    def weighted_sum(t):
        kt, n_tiles = window(t)
        vt = jnp.concatenate([vt_scr[kt + i] for i in range(n_tiles)], axis=1)
        ot = _dot(vt, e_scr[t % N_SLOTS, :n_tiles * LANES, :])
        inv = 1.0 / ot[LANES:LANES + 1, :]
        o_pair = jnp.concatenate(
            [ot[:A_HEAD_DIM, :Q_BLOCK] * inv[:, :Q_BLOCK],
             ot[A_HEAD_DIM:LANES, Q_BLOCK:] * inv[:, Q_BLOCK:]], axis=0)
        o_ref[t * Q_BLOCK:(t + 1) * Q_BLOCK, :] = o_pair.T.astype(BF16)

    for step in range(n_blk + 2):
        if step < n_blk:
            scores(step)
        if 0 <= step - 1 < n_blk:
            exps(step - 1)
        if 0 <= step - 2 < n_blk:
            weighted_sum(step - 2)


def _attention(qkva, bias, layer):
    bsz, seq, _ = qkva.shape
    n_pair = A_HEADS // 2
    blk = (None, seq, LANES)
    return pl.pallas_call(
        _attn_kernel,
        grid=(bsz, n_pair),
        in_specs=[
            pl.BlockSpec(blk, lambda b, p: (b, 0, p)),
            pl.BlockSpec(blk, lambda b, p: (b, 0, n_pair + p)),
            pl.BlockSpec(blk, lambda b, p: (b, 0, 2 * n_pair + p)),
            pl.BlockSpec((None, K_WINDOW, 2 * Q_BLOCK), lambda b, p: (layer * n_pair + p, 0, 0)),
        ],
        out_specs=pl.BlockSpec(blk, lambda b, p: (b, 0, p)),
        out_shape=jax.ShapeDtypeStruct((bsz, seq, A_WIDTH), BF16),
        scratch_shapes=[pltpu.VMEM((seq // LANES, VT_ROWS, LANES), BF16),
                        pltpu.VMEM((N_SLOTS, K_WINDOW, 2 * Q_BLOCK), F32),
                        pltpu.VMEM((N_SLOTS, 1, 2 * Q_BLOCK), F32),
                        pltpu.VMEM((N_SLOTS, K_WINDOW, 2 * Q_BLOCK), BF16)],
        compiler_params=pltpu.CompilerParams(
            dimension_semantics=("parallel", "parallel"), vmem_limit_bytes=VMEM_LIMIT),
        name="band_attention",
    )(qkva, qkva, qkva, bias)


def _chunk_cumsum(x):
    row = lax.broadcasted_iota(jnp.int32, x.shape, 0) % CHUNK
    d = 1
    while d < CHUNK:
        x = x + jnp.where(row >= d, pltpu.roll(x, d, axis=0), 0.0)
        d *= 2
    return x


def _gla_kernel(qe_ref, qo_ref, kt_ref, kend_ref, v_ref, vt_ref, gate_ref, decay_ref, g_ref,
                o_ref):
    seq = qe_ref.shape[0]
    lo = lax.broadcasted_iota(jnp.int32, (1, LANES), 1) < B_KEY_DIM
    ri = lax.broadcasted_iota(jnp.int32, (2 * CHUNK, CHUNK), 0)
    ci = lax.broadcasted_iota(jnp.int32, (2 * CHUNK, CHUNK), 1)
    causal = (ri % CHUNK) >= ci
    g = g_ref[...]

    def head_out(o, gh, gate):
        o = o * lax.rsqrt(jnp.mean(o * o, axis=-1, keepdims=True) + EPS) * gh
        return (o * gate).astype(BF16)

    group = GLA_GROUP * CHUNK

    def body(i, st):
        rows = pl.ds(pl.multiple_of(i * group, group), group)
        q_even = qe_ref[rows, :]
        q_odd = qo_ref[rows, :]
        kt = kt_ref[rows, :]
        kend = kend_ref[rows, :]
        v = v_ref[rows, :]
        decay = decay_ref[pl.ds(pl.multiple_of(i * GLA_GROUP, GLA_GROUP), GLA_GROUP), :]
        vt = vt_ref[i]
        pad = jnp.zeros((CHUNK, LANES), BF16)
        outs = []
        for c in range(GLA_GROUP):
            sl = slice(c * CHUNK, (c + 1) * CHUNK)
            q4 = jnp.concatenate([q_even[sl], q_odd[sl]], axis=0)
            v_even, v_odd = v[sl, :B_VAL_DIM], v[sl, B_VAL_DIM:]
            att = jnp.where(causal, _dot_nt(q4, kt[sl]), 0.0).astype(BF16)
            inter = _dot_nt(q4, st.astype(BF16))
            outs.append(jnp.concatenate(
                [_dot(att[:CHUNK], v_even) + inter[:CHUNK],
                 _dot(att[CHUNK:], v_odd) + inter[CHUNK:]], axis=1))
            pair_cols = slice((c // 2) * LANES, (c // 2 + 1) * LANES)
            kend_c = [kend[sl], pad] if c % 2 == 0 else [pad, kend[sl]]
            d_full = _dot(vt[:, pair_cols], jnp.concatenate(kend_c, axis=0))
            d_st = jnp.where(lo, d_full[:B_VAL_DIM], d_full[B_VAL_DIM:])
            st = st * decay[c:c + 1, :] + d_st
        o = jnp.concatenate(outs, axis=0)
        gate = gate_ref[rows, :]
        o_ref[rows, :B_VAL_DIM] = head_out(
            o[:, :B_VAL_DIM], g[:, :B_VAL_DIM], gate[:, :B_VAL_DIM])
        o_ref[rows, B_VAL_DIM:] = head_out(
            o[:, B_VAL_DIM:], g[:, B_VAL_DIM:], gate[:, B_VAL_DIM:])
        return st

    lax.fori_loop(0, seq // group, body, jnp.zeros((B_VAL_DIM, LANES), F32))


def _gla(gq, vt, gate, decay, g):
    bsz, seq, _ = gq.shape
    n_pair = B_HEADS // 2
    kblk = (None, seq, LANES)
    vblk = (None, seq, 2 * B_VAL_DIM)
    v_off = 4 * B_KEY_WIDTH // (2 * B_VAL_DIM)
    return pl.pallas_call(
        _gla_kernel,
        grid=(bsz, n_pair),
        in_specs=[
            pl.BlockSpec(kblk, lambda b, p: (b, 0, p)),
            pl.BlockSpec(kblk, lambda b, p: (b, 0, n_pair + p)),
            pl.BlockSpec(kblk, lambda b, p: (b, 0, 2 * n_pair + p)),
            pl.BlockSpec(kblk, lambda b, p: (b, 0, 3 * n_pair + p)),
            pl.BlockSpec(vblk, lambda b, p: (b, 0, v_off + p)),
            pl.BlockSpec((None,) + vt.shape[1:2] + (2 * B_VAL_DIM, vt.shape[3]),
                         lambda b, p: (b, 0, p, 0)),
            pl.BlockSpec(vblk, lambda b, p: (b, 0, p)),
            pl.BlockSpec((None, seq // CHUNK, LANES), lambda b, p: (b, 0, p)),
            pl.BlockSpec((1, 2 * B_VAL_DIM), lambda b, p: (0, p)),
        ],
        out_specs=pl.BlockSpec(vblk, lambda b, p: (b, 0, p)),
        out_shape=jax.ShapeDtypeStruct((bsz, seq, B_WIDTH), BF16),
        compiler_params=pltpu.CompilerParams(
            dimension_semantics=("parallel", "parallel"), vmem_limit_bytes=VMEM_LIMIT),
        name="gla",
    )(gq, gq, gq, gq, gq, vt, gate, decay, g)


def _merge_kernel(res_ref, g_ref, ya_ref, yb_ref, wgate_ref, wbr_ref, wout_ref, o_ref):
    res = res_ref[...]
    h = _rmsnorm_bf16(res, g_ref[...])
    gates = _dot(h, wgate_ref[...])
    u_a = _dot(ya_ref[...], wbr_ref[0])
    u_b = _dot(yb_ref[...], wbr_ref[1])
    merged = _sigmoid(gates[:, :D_MODEL]) * u_a + _sigmoid(gates[:, D_MODEL:]) * u_b
    o_ref[...] = res + _dot(merged.astype(BF16), wout_ref[...])


def _merge(res, g, ya, yb, wgate, wbr, wout, tm, layer):
    t = res.shape[0]
    return pl.pallas_call(
        _merge_kernel,
        grid=(t // tm,),
        in_specs=[
            pl.BlockSpec((tm, D_MODEL), lambda i: (i, 0)),
            _const_spec((1, D_MODEL)),
            pl.BlockSpec((tm, A_WIDTH), lambda i: (i, 0)),
            pl.BlockSpec((tm, B_WIDTH), lambda i: (i, 0)),
            _layer_spec(wgate, layer),
            _layer_spec(wbr, layer),
            _layer_spec(wout, layer),
        ],
        out_specs=pl.BlockSpec((tm, D_MODEL), lambda i: (i, 0)),
        out_shape=jax.ShapeDtypeStruct((t, D_MODEL), F32),
        compiler_params=pltpu.CompilerParams(
            dimension_semantics=("parallel",), vmem_limit_bytes=VMEM_LIMIT),
        name="merge",
    )(res, g, ya, yb, wgate, wbr, wout)


def _mlp_kernel(res_ref, g_ref, wup_ref, wdn_ref, gf_ref, o_ref, *, final_norm):
    res = res_ref[...]
    h = _rmsnorm_bf16(res, g_ref[...])
    acc = res
    for c in range(D_FF // D_MODEL):
        cols = slice(c * D_MODEL, (c + 1) * D_MODEL)
        up = jnp.maximum(_dot(h, wup_ref[:, cols]), 0.0)
        acc = acc + _dot((up * up).astype(BF16), wdn_ref[cols, :])
    if final_norm:
        ms = jnp.mean(acc * acc, axis=-1, keepdims=True)
        acc = (acc * lax.rsqrt(ms + EPS)) * gf_ref[...]
    o_ref[...] = acc


def _mlp(res, g, wup, wdn, gf, tm, layer, final_norm):
    t = res.shape[0]
    return pl.pallas_call(
        functools.partial(_mlp_kernel, final_norm=final_norm),
        grid=(t // tm,),
        in_specs=[
            pl.BlockSpec((tm, D_MODEL), lambda i: (i, 0)),
            _const_spec((1, D_MODEL)),
            _layer_spec(wup, layer),
            _layer_spec(wdn, layer),
            _const_spec((1, D_MODEL)),
        ],
        out_specs=pl.BlockSpec((tm, D_MODEL), lambda i: (i, 0)),
        out_shape=jax.ShapeDtypeStruct((t, D_MODEL), F32),
        compiler_params=pltpu.CompilerParams(
            dimension_semantics=("parallel",), vmem_limit_bytes=VMEM_LIMIT),
        name="mlp",
    )(res, g, wup, wdn, gf)


def kernel(x, mix_norm_g, w_in, rel_bias, w_gate_lr, b_gate, gla_norm_g, w_branch, w_out,
           mlp_norm_g, w_up, w_down, final_norm_g):
    bsz, seq, d = x.shape
    t = bsz * seq
    tm = GLA_GROUP * CHUNK
    a_end = 3 * A_WIDTH
    b_end = a_end + 2 * B_KEY_WIDTH + 2 * B_WIDTH
    lr_end = b_end + GATE_RANK
    w_a = w_in[:, :, :a_end].astype(BF16)
    w_b = w_in[:, :, a_end:b_end].astype(BF16)
    w_lr = w_in[:, :, b_end:lr_end].astype(BF16)
    w_gates = w_in[:, :, lr_end:].astype(BF16)
    w_glr = w_gate_lr.astype(BF16)
    w_br = w_branch.astype(BF16)
    w_o = w_out.astype(BF16)
    w_u = w_up.astype(BF16)
    w_d = w_down.astype(BF16)
    bias = _bias_tables(rel_bias)

    res = x.reshape(t, d)
    for l in range(DEPTH):
        qkva, gq, gate, decay, vt = _inproj(res, mix_norm_g[l][None], w_a, w_b, w_lr, w_glr,
                                            b_gate[l][None], tm, l)
        ya = _attention(qkva.reshape(bsz, seq, -1), bias, l)
        yb = _gla(gq.reshape(bsz, seq, -1), vt.reshape(bsz, seq // tm, B_WIDTH, tm),
                  gate.reshape(bsz, seq, -1), decay.reshape(bsz, seq // CHUNK, -1),
                  gla_norm_g[l][None])
        res = _merge(res, mix_norm_g[l][None], ya.reshape(t, -1), yb.reshape(t, -1),
                     w_gates, w_br, w_o, tm, l)
        res = _mlp(res, mlp_norm_g[l][None], w_u, w_d, final_norm_g[None], tm, l,
                   final_norm=(l == DEPTH - 1))
    return res.reshape(bsz, seq, d)
```

```python
import functools

import jax
import jax.numpy as jnp
from jax import lax
from jax.experimental import pallas as pl
from jax.experimental.pallas import tpu as pltpu

D_MODEL = 1024
DEPTH = 4
CHUNK = 64
EPS = 1e-6

A_HEADS = 8
A_HEAD_DIM = 64
A_WIDTH = A_HEADS * A_HEAD_DIM
N_PREV_CHUNKS = 8
PREV = N_PREV_CHUNKS * CHUNK
REL_CLIP = 128
N_REL = 2 * REL_CLIP + 1

B_HEADS = 4
B_KEY_DIM = 64
B_VAL_DIM = 128
B_KEY_WIDTH = B_HEADS * B_KEY_DIM
B_WIDTH = B_HEADS * B_VAL_DIM
GATE_RANK = 16
GATE_TAU = 16.0
D_FF = 4 * D_MODEL

LANES = 128
Q_BLOCK = 2 * CHUNK
K_WINDOW = PREV + Q_BLOCK
N_WIN_TILES = K_WINDOW // 128
VT_ROWS = 128 + 16
N_SLOTS = 2
ZERO_BIAS_TILES = tuple(range(1, (PREV - REL_CLIP) // 128))
LOG2E = 1.4426950408889634
ROW_SLABS = 2
GLA_GROUP = 8
TOEPLITZ_W = 768
VMEM_LIMIT = 56 * 1024 * 1024

F32 = jnp.float32
BF16 = jnp.bfloat16
NEG = -1e30


def _rmsnorm_bf16(x, g):
    ms = jnp.mean(x * x, axis=-1, keepdims=True)
    return ((x * lax.rsqrt(ms + EPS)) * g).astype(BF16)


def _sigmoid(x):
    return 1.0 / (1.0 + jnp.exp(-x))


def _dot(a, b):
    return jnp.dot(a, b, preferred_element_type=F32)


def _dot_nt(a, b):
    return lax.dot_general(a, b, (((1,), (1,)), ((), ())), preferred_element_type=F32)


def _dot_tn(a, b):
    return lax.dot_general(a, b, (((0,), (0,)), ((), ())), preferred_element_type=F32)


def _const_spec(shape):
    nd = len(shape)
    return pl.BlockSpec(shape, lambda *_: (0,) * nd)


def _layer_spec(stacked, layer):
    rest = stacked.shape[1:]
    return pl.BlockSpec((None,) + rest, lambda *_: (layer,) + (0,) * len(rest))


def _bias_kernel(t_ref, o_ref):
    t = t_ref[...]
    t = (t - t[:, 0:1]) * LOG2E
    x = jnp.broadcast_to(t, (Q_BLOCK, TOEPLITZ_W))
    row = lax.broadcasted_iota(jnp.int32, (Q_BLOCK, TOEPLITZ_W), 0)
    shift = 1
    while shift < Q_BLOCK:
        x = jnp.where((row & shift) != 0, pltpu.roll(x, shift, axis=1), x)
        shift *= 2
    x = x[:, :K_WINDOW]
    r = lax.broadcasted_iota(jnp.int32, (Q_BLOCK, K_WINDOW), 0)
    c = lax.broadcasted_iota(jnp.int32, (Q_BLOCK, K_WINDOW), 1)
    band_lo = (r // CHUNK) * CHUNK
    in_band = (c >= band_lo) & (c < band_lo + PREV + CHUNK)
    x = jnp.where(in_band, x, NEG)
    for j in range(K_WINDOW // LANES):
        o_ref[j * LANES:(j + 1) * LANES, :] = x[:, j * LANES:(j + 1) * LANES].T


def _bias_tables(rel_bias):
    far = rel_bias[..., N_REL - 1:]
    n_far_lo = PREV - REL_CLIP + 1
    t0 = jnp.concatenate([
        jnp.broadcast_to(far, rel_bias.shape[:-1] + (n_far_lo,)),
        rel_bias[..., N_REL - 2:0:-1],
        jnp.broadcast_to(far, rel_bias.shape[:-1] + (TOEPLITZ_W - n_far_lo - (N_REL - 2),)),
    ], axis=-1)
    n = DEPTH * A_HEADS
    t0 = t0.reshape(n, 1, TOEPLITZ_W)
    return pl.pallas_call(
        _bias_kernel,
        grid=(n,),
        in_specs=[pl.BlockSpec((None, 1, TOEPLITZ_W), lambda i: (i, 0, 0))],
        out_specs=pl.BlockSpec((None, K_WINDOW, Q_BLOCK), lambda i: (i // 2, 0, i % 2)),
        out_shape=jax.ShapeDtypeStruct((n // 2, K_WINDOW, 2 * Q_BLOCK), F32),
        name="bias_table",
    )(t0)


AB_COLS = 3 * A_WIDTH + 2 * B_KEY_WIDTH + 2 * B_WIDTH


def _split_w_in_kernel(w_ref, ab_ref, lr_ref, gates_ref):
    ab_ref[...] = w_ref[:, :AB_COLS].astype(BF16)
    lr_ref[...] = w_ref[:, AB_COLS:AB_COLS + GATE_RANK].astype(BF16)
    gates_ref[...] = w_ref[:, AB_COLS + GATE_RANK:].astype(BF16)


def _split_w_in(w_in, rows=256):
    depth, d, cols = w_in.shape
    n_gate = cols - AB_COLS - GATE_RANK
    return pl.pallas_call(
        _split_w_in_kernel,
        grid=(depth, d // rows),
        in_specs=[pl.BlockSpec((None, rows, cols), lambda l, i: (l, i, 0))],
        out_specs=[
            pl.BlockSpec((None, rows, AB_COLS), lambda l, i: (l, i, 0)),
            pl.BlockSpec((None, rows, GATE_RANK), lambda l, i: (l, i, 0)),
            pl.BlockSpec((None, rows, n_gate), lambda l, i: (l, i, 0)),
        ],
        out_shape=[
            jax.ShapeDtypeStruct((depth, d, AB_COLS), BF16),
            jax.ShapeDtypeStruct((depth, d, GATE_RANK), BF16),
            jax.ShapeDtypeStruct((depth, d, n_gate), BF16),
        ],
        compiler_params=pltpu.CompilerParams(
            dimension_semantics=("parallel", "parallel"), vmem_limit_bytes=VMEM_LIMIT),
        name="split_w_in",
    )(w_in)


def _inproj_kernel(res_ref, g_ref, wa_ref, wb_ref, wlr_ref, wg_ref, bg_ref,
                   qkva_ref, gla_ref, gate_ref, decay_ref, vt_ref):
    slab = res_ref.shape[0] // ROW_SLABS
    n_chunk = slab // CHUNK
    kw = B_KEY_WIDTH
    even = (lax.broadcasted_iota(jnp.int32, (1, kw), 1) // B_KEY_DIM) % 2 == 0
    decays = []
    for s in range(ROW_SLABS):
        rows = slice(s * slab, (s + 1) * slab)
        h = _rmsnorm_bf16(res_ref[rows, :], g_ref[...])
        lr = _dot(h, wlr_ref[...])
        z = _dot(lr.astype(BF16), wg_ref[...]) + bg_ref[...]
        log_sig = jnp.minimum(z, 0.0) - jnp.log(1.0 + jnp.exp(-jnp.abs(z)))
        b = _chunk_cumsum(log_sig / GATE_TAU)
        b3 = b.reshape(n_chunk, CHUNK, kw)
        b_last = b3[:, CHUNK - 1:CHUNK, :]
        decays.append(jnp.exp(b_last).reshape(n_chunk, kw))
        to_end = jnp.exp(b_last - b3).reshape(slab, kw)
        pb = _dot(h, wb_ref[...])
        qt = pb[:, :kw] * (B_KEY_DIM ** -0.5) * jnp.exp(b)
        k = pb[:, kw:2 * kw]
        gla_ref[rows, :kw] = jnp.where(even, qt, 0.0).astype(BF16)
        gla_ref[rows, kw:2 * kw] = jnp.where(even, 0.0, qt).astype(BF16)
        gla_ref[rows, 2 * kw:3 * kw] = (k * jnp.exp(-b)).astype(BF16)
        gla_ref[rows, 3 * kw:4 * kw] = (k * to_end).astype(BF16)
        v = pb[:, 2 * kw:2 * kw + B_WIDTH]
        gla_ref[rows, 4 * kw:] = v.astype(BF16)
        vt_ref[:, rows] = v.T.astype(BF16)
        r = pb[:, 2 * kw + B_WIDTH:]
        gate_ref[rows, :] = r * _sigmoid(r)
        qkva_ref[rows, :] = _dot(h, wa_ref[...]).astype(BF16)
    decay_ref[...] = jnp.concatenate(decays, axis=0)


def _inproj(res, g, wab, wlr, wg, bg, tm, layer):
    t = res.shape[0]
    gla_cols = 4 * B_KEY_WIDTH + B_WIDTH
    half = (None, D_MODEL, wab.shape[-1] // 2)
    return pl.pallas_call(
        _inproj_kernel,
        grid=(t // tm,),
        in_specs=[
            pl.BlockSpec((tm, D_MODEL), lambda i: (i, 0)),
            _const_spec((1, D_MODEL)),
            pl.BlockSpec(half, lambda i: (layer, 0, 0)),
            pl.BlockSpec(half, lambda i: (layer, 0, 1)),
            _layer_spec(wlr, layer),
            _layer_spec(wg, layer),
            _const_spec((1, B_KEY_WIDTH)),
        ],
        out_specs=[
            pl.BlockSpec((tm, 3 * A_WIDTH), lambda i: (i, 0)),
            pl.BlockSpec((tm, gla_cols), lambda i: (i, 0)),
            pl.BlockSpec((tm, B_WIDTH), lambda i: (i, 0)),
            pl.BlockSpec((tm // CHUNK, B_KEY_WIDTH), lambda i: (i, 0)),
            pl.BlockSpec((None, B_WIDTH, tm), lambda i: (i, 0, 0)),
        ],
        out_shape=[
            jax.ShapeDtypeStruct((t, 3 * A_WIDTH), BF16),
            jax.ShapeDtypeStruct((t, gla_cols), BF16),
            jax.ShapeDtypeStruct((t, B_WIDTH), F32),
            jax.ShapeDtypeStruct((t // CHUNK, B_KEY_WIDTH), F32),
            jax.ShapeDtypeStruct((t // tm, B_WIDTH, tm), BF16),
        ],
        compiler_params=pltpu.CompilerParams(
            dimension_semantics=("parallel",), vmem_limit_bytes=VMEM_LIMIT),
        name="inproj",
    )(res, g, wab, wab, wlr, wg, bg)


def _attn_kernel(q_ref, k_ref, v_ref, bias_ref, o_ref, vt_scr, s_scr, m_scr, e_scr):
    seq = q_ref.shape[0]
    n_blk = seq // Q_BLOCK
    n_prev = PREV // Q_BLOCK
    lo = lax.broadcasted_iota(jnp.int32, (1, LANES), 1) < A_HEAD_DIM

    for r in range(n_blk):
        vt_scr[r, :LANES, :] = v_ref[r * LANES:(r + 1) * LANES, :].astype(F32).T.astype(BF16)
        vt_scr[r, LANES:, :] = jnp.ones((VT_ROWS - LANES, LANES), BF16)

    def window(t):
        return max(t - n_prev, 0), min(t + 1, N_WIN_TILES)

    def scores(t):
        kt, n_tiles = window(t)
        rows = n_tiles * LANES
        q2 = q_ref[t * Q_BLOCK:(t + 1) * Q_BLOCK, :].astype(F32) * (A_HEAD_DIM ** -0.5 * LOG2E)
        q4 = jnp.concatenate(
            [jnp.where(lo, q2, 0.0), jnp.where(lo, 0.0, q2)], axis=0).astype(BF16)
        kw = k_ref[kt * LANES:kt * LANES + rows, :]
        s = _dot_nt(kw, q4)
        m = None
        for i in range(n_tiles):
            tile = s[i * LANES:(i + 1) * LANES, :]
            w = N_WIN_TILES - n_tiles + i
            if w not in ZERO_BIAS_TILES:
                tile = tile + bias_ref[w * LANES:(w + 1) * LANES, :]
            s_scr[t % N_SLOTS, i * LANES:(i + 1) * LANES, :] = tile
            tile_max = jnp.max(tile.reshape(LANES // 8, 8, 2 * Q_BLOCK), axis=0)
            m = tile_max if m is None else jnp.maximum(m, tile_max)
        m_scr[t % N_SLOTS] = jnp.max(m, axis=0, keepdims=True)

    def exps(t):
        rows = window(t)[1] * LANES
        d = (s_scr[t % N_SLOTS, :rows, :] - m_scr[t % N_SLOTS]).astype(BF16)
        e_scr[t % N_SLOTS, :rows, :] = jnp.exp2(d)

    def weighted_sum(t):
        kt, n_tiles = window(t)
        vt = jnp.concatenate([vt_scr[kt + i] for i in range(n_tiles)], axis=1)
        ot = _dot(vt, e_scr[t % N_SLOTS, :n_tiles * LANES, :])
        inv = 1.0 / ot[LANES:LANES + 1, :]
        o_pair = jnp.concatenate(
            [ot[:A_HEAD_DIM, :Q_BLOCK] * inv[:, :Q_BLOCK],
             ot[A_HEAD_DIM:LANES, Q_BLOCK:] * inv[:, Q_BLOCK:]], axis=0)
        o_ref[t * Q_BLOCK:(t + 1) * Q_BLOCK, :] = o_pair.T.astype(BF16)

    for step in range(n_blk + 2):
        if step < n_blk:
            scores(step)
        if 0 <= step - 1 < n_blk:
            exps(step - 1)
        if 0 <= step - 2 < n_blk:
            weighted_sum(step - 2)


def _attention(qkva, bias, layer):
    bsz, seq, _ = qkva.shape
    n_pair = A_HEADS // 2
    blk = (None, seq, LANES)
    return pl.pallas_call(
        _attn_kernel,
        grid=(bsz, n_pair),
        in_specs=[
            pl.BlockSpec(blk, lambda b, p: (b, 0, p)),
            pl.BlockSpec(blk, lambda b, p: (b, 0, n_pair + p)),
            pl.BlockSpec(blk, lambda b, p: (b, 0, 2 * n_pair + p)),
            pl.BlockSpec((None, K_WINDOW, 2 * Q_BLOCK), lambda b, p: (layer * n_pair + p, 0, 0)),
        ],
        out_specs=pl.BlockSpec(blk, lambda b, p: (b, 0, p)),
        out_shape=jax.ShapeDtypeStruct((bsz, seq, A_WIDTH), BF16),
        scratch_shapes=[pltpu.VMEM((seq // LANES, VT_ROWS, LANES), BF16),
                        pltpu.VMEM((N_SLOTS, K_WINDOW, 2 * Q_BLOCK), F32),
                        pltpu.VMEM((N_SLOTS, 1, 2 * Q_BLOCK), F32),
                        pltpu.VMEM((N_SLOTS, K_WINDOW, 2 * Q_BLOCK), BF16)],
        compiler_params=pltpu.CompilerParams(
            dimension_semantics=("parallel", "parallel"), vmem_limit_bytes=VMEM_LIMIT),
        name="band_attention",
    )(qkva, qkva, qkva, bias)


def _chunk_cumsum(x):
    row = lax.broadcasted_iota(jnp.int32, x.shape, 0) % CHUNK
    d = 1
    while d < CHUNK:
        x = x + jnp.where(row >= d, pltpu.roll(x, d, axis=0), 0.0)
        d *= 2
    return x


def _gla_kernel(qe_ref, qo_ref, kt_ref, kend_ref, v_ref, vt_ref, gate_ref, decay_ref, g_ref,
                o_ref):
    seq = qe_ref.shape[0]
    lo = lax.broadcasted_iota(jnp.int32, (1, LANES), 1) < B_KEY_DIM
    ri = lax.broadcasted_iota(jnp.int32, (2 * CHUNK, CHUNK), 0)
    ci = lax.broadcasted_iota(jnp.int32, (2 * CHUNK, CHUNK), 1)
    causal = (ri % CHUNK) >= ci
    g = g_ref[...]

    def head_out(o, gh, gate):
        o = o * lax.rsqrt(jnp.mean(o * o, axis=-1, keepdims=True) + EPS) * gh
        return (o * gate).astype(BF16)

    group = GLA_GROUP * CHUNK

    def body(i, st):
        rows = pl.ds(pl.multiple_of(i * group, group), group)
        q_even = qe_ref[rows, :]
        q_odd = qo_ref[rows, :]
        kt = kt_ref[rows, :]
        kend = kend_ref[rows, :]
        v = v_ref[rows, :]
        decay = decay_ref[pl.ds(pl.multiple_of(i * GLA_GROUP, GLA_GROUP), GLA_GROUP), :]
        vt = vt_ref[i]
        pad = jnp.zeros((CHUNK, LANES), BF16)
        outs = []
        for c in range(GLA_GROUP):
            sl = slice(c * CHUNK, (c + 1) * CHUNK)
            q4 = jnp.concatenate([q_even[sl], q_odd[sl]], axis=0)
            v_even, v_odd = v[sl, :B_VAL_DIM], v[sl, B_VAL_DIM:]
            att = jnp.where(causal, _dot_nt(q4, kt[sl]), 0.0).astype(BF16)
            inter = _dot_nt(q4, st.astype(BF16))
            outs.append(jnp.concatenate(
                [_dot(att[:CHUNK], v_even) + inter[:CHUNK],
                 _dot(att[CHUNK:], v_odd) + inter[CHUNK:]], axis=1))
            pair_cols = slice((c // 2) * LANES, (c // 2 + 1) * LANES)
            kend_c = [kend[sl], pad] if c % 2 == 0 else [pad, kend[sl]]
            d_full = _dot(vt[:, pair_cols], jnp.concatenate(kend_c, axis=0))
            d_st = jnp.where(lo, d_full[:B_VAL_DIM], d_full[B_VAL_DIM:])
            st = st * decay[c:c + 1, :] + d_st
        o = jnp.concatenate(outs, axis=0)
        gate = gate_ref[rows, :]
        o_ref[rows, :B_VAL_DIM] = head_out(
            o[:, :B_VAL_DIM], g[:, :B_VAL_DIM], gate[:, :B_VAL_DIM])
        o_ref[rows, B_VAL_DIM:] = head_out(
            o[:, B_VAL_DIM:], g[:, B_VAL_DIM:], gate[:, B_VAL_DIM:])
        return st

    lax.fori_loop(0, seq // group, body, jnp.zeros((B_VAL_DIM, LANES), F32))


def _gla(gq, vt, gate, decay, g):
    bsz, seq, _ = gq.shape
    n_pair = B_HEADS // 2
    kblk = (None, seq, LANES)
    vblk = (None, seq, 2 * B_VAL_DIM)
    v_off = 4 * B_KEY_WIDTH // (2 * B_VAL_DIM)
    return pl.pallas_call(
        _gla_kernel,
        grid=(bsz, n_pair),
        in_specs=[
            pl.BlockSpec(kblk, lambda b, p: (b, 0, p)),
            pl.BlockSpec(kblk, lambda b, p: (b, 0, n_pair + p)),
            pl.BlockSpec(kblk, lambda b, p: (b, 0, 2 * n_pair + p)),
            pl.BlockSpec(kblk, lambda b, p: (b, 0, 3 * n_pair + p)),
            pl.BlockSpec(vblk, lambda b, p: (b, 0, v_off + p)),
            pl.BlockSpec((None,) + vt.shape[1:2] + (2 * B_VAL_DIM, vt.shape[3]),
                         lambda b, p: (b, 0, p, 0)),
            pl.BlockSpec(vblk, lambda b, p: (b, 0, p)),
            pl.BlockSpec((None, seq // CHUNK, LANES), lambda b, p: (b, 0, p)),
            pl.BlockSpec((1, 2 * B_VAL_DIM), lambda b, p: (0, p)),
        ],
        out_specs=pl.BlockSpec(vblk, lambda b, p: (b, 0, p)),
        out_shape=jax.ShapeDtypeStruct((bsz, seq, B_WIDTH), BF16),
        compiler_params=pltpu.CompilerParams(
            dimension_semantics=("parallel", "parallel"), vmem_limit_bytes=VMEM_LIMIT),
        name="gla",
    )(gq, gq, gq, gq, gq, vt, gate, decay, g)


def _merge_kernel(res_ref, g_ref, ya_ref, yb_ref, wgate_ref, wbr_ref, wout_ref, o_ref):
    slab = res_ref.shape[0] // ROW_SLABS
    for r in range(ROW_SLABS):
        rows = slice(r * slab, (r + 1) * slab)
        res = res_ref[rows, :]
        h = _rmsnorm_bf16(res, g_ref[...])
        gates = _dot(h, wgate_ref[...])
        u_a = _dot(ya_ref[rows, :], wbr_ref[0])
        u_b = _dot(yb_ref[rows, :], wbr_ref[1])
        merged = _sigmoid(gates[:, :D_MODEL]) * u_a + _sigmoid(gates[:, D_MODEL:]) * u_b
        o_ref[rows, :] = res + _dot(merged.astype(BF16), wout_ref[...])


def _merge(res, g, ya, yb, wgate, wbr, wout, tm, layer):
    t = res.shape[0]
    return pl.pallas_call(
        _merge_kernel,
        grid=(t // tm,),
        in_specs=[
            pl.BlockSpec((tm, D_MODEL), lambda i: (i, 0)),
            _const_spec((1, D_MODEL)),
            pl.BlockSpec((tm, A_WIDTH), lambda i: (i, 0)),
            pl.BlockSpec((tm, B_WIDTH), lambda i: (i, 0)),
            _layer_spec(wgate, layer),
            _layer_spec(wbr, layer),
            _layer_spec(wout, layer),
        ],
        out_specs=pl.BlockSpec((tm, D_MODEL), lambda i: (i, 0)),
        out_shape=jax.ShapeDtypeStruct((t, D_MODEL), F32),
        compiler_params=pltpu.CompilerParams(
            dimension_semantics=("parallel",), vmem_limit_bytes=VMEM_LIMIT),
        name="merge",
    )(res, g, ya, yb, wgate, wbr, wout)


def _mlp_kernel(res_ref, g_ref, wup_ref, wdn_ref, gf_ref, o_ref, *, final_norm):
    slab = res_ref.shape[0] // ROW_SLABS
    for r in range(ROW_SLABS):
        rows = slice(r * slab, (r + 1) * slab)
        res = res_ref[rows, :]
        h = _rmsnorm_bf16(res, g_ref[...])
        acc = res
        for c in range(D_FF // D_MODEL):
            cols = slice(c * D_MODEL, (c + 1) * D_MODEL)
            up = jnp.maximum(_dot(h, wup_ref[:, cols]), 0.0)
            acc = acc + _dot((up * up).astype(BF16), wdn_ref[cols, :])
        if final_norm:
            ms = jnp.mean(acc * acc, axis=-1, keepdims=True)
            acc = (acc * lax.rsqrt(ms + EPS)) * gf_ref[...]
        o_ref[rows, :] = acc


def _mlp(res, g, wup, wdn, gf, tm, layer, final_norm):
    t = res.shape[0]
    return pl.pallas_call(
        functools.partial(_mlp_kernel, final_norm=final_norm),
        grid=(t // tm,),
        in_specs=[
            pl.BlockSpec((tm, D_MODEL), lambda i: (i, 0)),
            _const_spec((1, D_MODEL)),
            _layer_spec(wup, layer),
            _layer_spec(wdn, layer),
            _const_spec((1, D_MODEL)),
        ],
        out_specs=pl.BlockSpec((tm, D_MODEL), lambda i: (i, 0)),
        out_shape=jax.ShapeDtypeStruct((t, D_MODEL), F32),
        compiler_params=pltpu.CompilerParams(
            dimension_semantics=("parallel",), vmem_limit_bytes=VMEM_LIMIT),
        name="mlp",
    )(res, g, wup, wdn, gf)


def kernel(x, mix_norm_g, w_in, rel_bias, w_gate_lr, b_gate, gla_norm_g, w_branch, w_out,
           mlp_norm_g, w_up, w_down, final_norm_g):
    bsz, seq, d = x.shape
    t = bsz * seq
    tm = GLA_GROUP * CHUNK
    w_ab, w_lr, w_gates = _split_w_in(w_in)
    w_glr = w_gate_lr.astype(BF16)
    w_br = w_branch.astype(BF16)
    w_o = w_out.astype(BF16)
    w_u = w_up.astype(BF16)
    w_d = w_down.astype(BF16)
    bias = _bias_tables(rel_bias)

    res = x.reshape(t, d)
    for l in range(DEPTH):
        qkva, gq, gate, decay, vt = _inproj(res, mix_norm_g[l][None], w_ab, w_lr, w_glr,
                                            b_gate[l][None], tm, l)
        ya = _attention(qkva.reshape(bsz, seq, -1), bias, l)
        yb = _gla(gq.reshape(bsz, seq, -1), vt.reshape(bsz, seq // tm, B_WIDTH, tm),
                  gate.reshape(bsz, seq, -1), decay.reshape(bsz, seq // CHUNK, -1),
                  gla_norm_g[l][None])
        res = _merge(res, mix_norm_g[l][None], ya.reshape(t, -1), yb.reshape(t, -1),
                     w_gates, w_br, w_o, tm, l)
        res = _mlp(res, mlp_norm_g[l][None], w_u, w_d, final_norm_g[None], tm, l,
                   final_norm=(l == DEPTH - 1))
    return res.reshape(bsz, seq, d)
```

```python
import functools

import jax
import jax.numpy as jnp
from jax import lax
from jax.experimental import pallas as pl
from jax.experimental.pallas import tpu as pltpu

D_MODEL = 1024
DEPTH = 4
CHUNK = 64
EPS = 1e-6

A_HEADS = 8
A_HEAD_DIM = 64
A_WIDTH = A_HEADS * A_HEAD_DIM
N_PREV_CHUNKS = 8
PREV = N_PREV_CHUNKS * CHUNK
REL_CLIP = 128
N_REL = 2 * REL_CLIP + 1

B_HEADS = 4
B_KEY_DIM = 64
B_VAL_DIM = 128
B_KEY_WIDTH = B_HEADS * B_KEY_DIM
B_WIDTH = B_HEADS * B_VAL_DIM
GATE_RANK = 16
GATE_TAU = 16.0
D_FF = 4 * D_MODEL

LANES = 128
Q_BLOCK = 2 * CHUNK
K_WINDOW = PREV + Q_BLOCK
N_WIN_TILES = K_WINDOW // 128
VT_ROWS = 128 + 16
N_SLOTS = 2
ZERO_BIAS_TILES = tuple(range(1, (PREV - REL_CLIP) // 128))
LOG2E = 1.4426950408889634
ROW_SLABS = 2
GLA_GROUP = 8
TOEPLITZ_W = 768
VMEM_LIMIT = 56 * 1024 * 1024

F32 = jnp.float32
BF16 = jnp.bfloat16
NEG = -1e30


def _rmsnorm_bf16(x, g):
    ms = jnp.mean(x * x, axis=-1, keepdims=True)
    return ((x * lax.rsqrt(ms + EPS)) * g).astype(BF16)


def _sigmoid(x):
    return 1.0 / (1.0 + jnp.exp(-x))


def _dot(a, b):
    return jnp.dot(a, b, preferred_element_type=F32)


def _dot_nt(a, b):
    return lax.dot_general(a, b, (((1,), (1,)), ((), ())), preferred_element_type=F32)


def _dot_tn(a, b):
    return lax.dot_general(a, b, (((0,), (0,)), ((), ())), preferred_element_type=F32)


def _const_spec(shape):
    nd = len(shape)
    return pl.BlockSpec(shape, lambda *_: (0,) * nd)


def _layer_spec(stacked, layer):
    rest = stacked.shape[1:]
    return pl.BlockSpec((None,) + rest, lambda *_: (layer,) + (0,) * len(rest))


def _bias_kernel(t_ref, o_ref):
    t = t_ref[...]
    t = (t - t[:, 0:1]) * LOG2E
    x = jnp.broadcast_to(t, (Q_BLOCK, TOEPLITZ_W))
    row = lax.broadcasted_iota(jnp.int32, (Q_BLOCK, TOEPLITZ_W), 0)
    shift = 1
    while shift < Q_BLOCK:
        x = jnp.where((row & shift) != 0, pltpu.roll(x, shift, axis=1), x)
        shift *= 2
    x = x[:, :K_WINDOW]
    r = lax.broadcasted_iota(jnp.int32, (Q_BLOCK, K_WINDOW), 0)
    c = lax.broadcasted_iota(jnp.int32, (Q_BLOCK, K_WINDOW), 1)
    band_lo = (r // CHUNK) * CHUNK
    in_band = (c >= band_lo) & (c < band_lo + PREV + CHUNK)
    x = jnp.where(in_band, x, NEG)
    for j in range(K_WINDOW // LANES):
        o_ref[j * LANES:(j + 1) * LANES, :] = x[:, j * LANES:(j + 1) * LANES].T


def _bias_tables(rel_bias):
    far = rel_bias[..., N_REL - 1:]
    n_far_lo = PREV - REL_CLIP + 1
    t0 = jnp.concatenate([
        jnp.broadcast_to(far, rel_bias.shape[:-1] + (n_far_lo,)),
        rel_bias[..., N_REL - 2:0:-1],
        jnp.broadcast_to(far, rel_bias.shape[:-1] + (TOEPLITZ_W - n_far_lo - (N_REL - 2),)),
    ], axis=-1)
    n = DEPTH * A_HEADS
    t0 = t0.reshape(n, 1, TOEPLITZ_W)
    return pl.pallas_call(
        _bias_kernel,
        grid=(n,),
        in_specs=[pl.BlockSpec((None, 1, TOEPLITZ_W), lambda i: (i, 0, 0))],
        out_specs=pl.BlockSpec((None, K_WINDOW, Q_BLOCK), lambda i: (i // 2, 0, i % 2)),
        out_shape=jax.ShapeDtypeStruct((n // 2, K_WINDOW, 2 * Q_BLOCK), F32),
        name="bias_table",
    )(t0)


def _inproj_kernel(res_ref, g_ref, wa_ref, wb_ref, wlr_ref, wg_ref, bg_ref,
                   qkva_ref, gla_ref, gate_ref, decay_ref, vt_ref):
    tm = res_ref.shape[0]
    n_chunk = tm // CHUNK
    kw = B_KEY_WIDTH
    even = (lax.broadcasted_iota(jnp.int32, (1, kw), 1) // B_KEY_DIM) % 2 == 0
    h = _rmsnorm_bf16(res_ref[...], g_ref[...])
    lr = _dot_nt(h, wlr_ref[...])
    z = _dot(lr.astype(BF16), wg_ref[...]) + bg_ref[...]
    log_sig = jnp.minimum(z, 0.0) - jnp.log(1.0 + jnp.exp(-jnp.abs(z)))
    b = _chunk_cumsum(log_sig / GATE_TAU)
    b3 = b.reshape(n_chunk, CHUNK, kw)
    b_last = b3[:, CHUNK - 1:CHUNK, :]
    decay_ref[...] = jnp.exp(b_last).reshape(n_chunk, kw)
    to_end = jnp.exp(b_last - b3).reshape(tm, kw)
    pb = _dot_nt(h, wb_ref[...])
    qt = pb[:, :kw] * (B_KEY_DIM ** -0.5) * jnp.exp(b)
    k = pb[:, kw:2 * kw]
    gla_ref[:, :kw] = jnp.where(even, qt, 0.0).astype(BF16)
    gla_ref[:, kw:2 * kw] = jnp.where(even, 0.0, qt).astype(BF16)
    gla_ref[:, 2 * kw:3 * kw] = (k * jnp.exp(-b)).astype(BF16)
    gla_ref[:, 3 * kw:4 * kw] = (k * to_end).astype(BF16)
    v = pb[:, 2 * kw:2 * kw + B_WIDTH]
    gla_ref[:, 4 * kw:] = v.astype(BF16)
    vt_ref[...] = v.T.astype(BF16)
    r = pb[:, 2 * kw + B_WIDTH:]
    gate_ref[...] = r * _sigmoid(r)
    qkva_ref[...] = _dot_nt(h, wa_ref[...]).astype(BF16)


def _inproj(res, g, w_t, wg, bg, tm, layer):
    t = res.shape[0]
    gla_cols = 4 * B_KEY_WIDTH + B_WIDTH
    half_rows = 3 * A_WIDTH
    assert half_rows == 2 * B_KEY_WIDTH + 2 * B_WIDTH
    half = (None, half_rows, D_MODEL)
    lr_block = 2 * half_rows // GATE_RANK
    return pl.pallas_call(
        _inproj_kernel,
        grid=(t // tm,),
        in_specs=[
            pl.BlockSpec((tm, D_MODEL), lambda i: (i, 0)),
            _const_spec((1, D_MODEL)),
            pl.BlockSpec(half, lambda i: (layer, 0, 0)),
            pl.BlockSpec(half, lambda i: (layer, 1, 0)),
            pl.BlockSpec((None, GATE_RANK, D_MODEL), lambda i: (layer, lr_block, 0)),
            _layer_spec(wg, layer),
            _const_spec((1, B_KEY_WIDTH)),
        ],
        out_specs=[
            pl.BlockSpec((tm, 3 * A_WIDTH), lambda i: (i, 0)),
            pl.BlockSpec((tm, gla_cols), lambda i: (i, 0)),
            pl.BlockSpec((tm, B_WIDTH), lambda i: (i, 0)),
            pl.BlockSpec((tm // CHUNK, B_KEY_WIDTH), lambda i: (i, 0)),
            pl.BlockSpec((None, B_WIDTH, tm), lambda i: (i, 0, 0)),
        ],
        out_shape=[
            jax.ShapeDtypeStruct((t, 3 * A_WIDTH), BF16),
            jax.ShapeDtypeStruct((t, gla_cols), BF16),
            jax.ShapeDtypeStruct((t, B_WIDTH), F32),
            jax.ShapeDtypeStruct((t // CHUNK, B_KEY_WIDTH), F32),
            jax.ShapeDtypeStruct((t // tm, B_WIDTH, tm), BF16),
        ],
        compiler_params=pltpu.CompilerParams(
            dimension_semantics=("parallel",), vmem_limit_bytes=VMEM_LIMIT),
        name="inproj",
    )(res, g, w_t, w_t, w_t, wg, bg)


def _attn_kernel(q_ref, k_ref, v_ref, bias_ref, o_ref, vt_scr, s_scr, m_scr, e_scr):
    seq = q_ref.shape[0]
    n_blk = seq // Q_BLOCK
    n_prev = PREV // Q_BLOCK
    lo = lax.broadcasted_iota(jnp.int32, (1, LANES), 1) < A_HEAD_DIM

    for r in range(n_blk):
        vt_scr[r, :LANES, :] = v_ref[r * LANES:(r + 1) * LANES, :].astype(F32).T.astype(BF16)
        vt_scr[r, LANES:, :] = jnp.ones((VT_ROWS - LANES, LANES), BF16)

    def window(t):
        return max(t - n_prev, 0), min(t + 1, N_WIN_TILES)

    def scores(t):
        kt, n_tiles = window(t)
        rows = n_tiles * LANES
        q2 = q_ref[t * Q_BLOCK:(t + 1) * Q_BLOCK, :].astype(F32) * (A_HEAD_DIM ** -0.5 * LOG2E)
        q4 = jnp.concatenate(
            [jnp.where(lo, q2, 0.0), jnp.where(lo, 0.0, q2)], axis=0).astype(BF16)
        kw = k_ref[kt * LANES:kt * LANES + rows, :]
        s = _dot_nt(kw, q4)
        m = None
        for i in range(n_tiles):
            tile = s[i * LANES:(i + 1) * LANES, :]
            w = N_WIN_TILES - n_tiles + i
            if w not in ZERO_BIAS_TILES:
                tile = tile + bias_ref[w * LANES:(w + 1) * LANES, :]
            s_scr[t % N_SLOTS, i * LANES:(i + 1) * LANES, :] = tile
            tile_max = jnp.max(tile.reshape(LANES // 8, 8, 2 * Q_BLOCK), axis=0)
            m = tile_max if m is None else jnp.maximum(m, tile_max)
        m_scr[t % N_SLOTS] = jnp.max(m, axis=0, keepdims=True)

    def exps(t):
        rows = window(t)[1] * LANES
        d = (s_scr[t % N_SLOTS, :rows, :] - m_scr[t % N_SLOTS]).astype(BF16)
        e_scr[t % N_SLOTS, :rows, :] = jnp.exp2(d)

    def weighted_sum(t):
        kt, n_tiles = window(t)
        vt = jnp.concatenate([vt_scr[kt + i] for i in range(n_tiles)], axis=1)
        ot = _dot(vt, e_scr[t % N_SLOTS, :n_tiles * LANES, :])
        inv = 1.0 / ot[LANES:LANES + 1, :]
        o_pair = jnp.concatenate(
            [ot[:A_HEAD_DIM, :Q_BLOCK] * inv[:, :Q_BLOCK],
             ot[A_HEAD_DIM:LANES, Q_BLOCK:] * inv[:, Q_BLOCK:]], axis=0)
        o_ref[t * Q_BLOCK:(t + 1) * Q_BLOCK, :] = o_pair.T.astype(BF16)

    for step in range(n_blk + 2):
        if step < n_blk:
            scores(step)
        if 0 <= step - 1 < n_blk:
            exps(step - 1)
        if 0 <= step - 2 < n_blk:
            weighted_sum(step - 2)


def _attention(qkva, bias, layer):
    bsz, seq, _ = qkva.shape
    n_pair = A_HEADS // 2
    blk = (None, seq, LANES)
    return pl.pallas_call(
        _attn_kernel,
        grid=(bsz, n_pair),
        in_specs=[
            pl.BlockSpec(blk, lambda b, p: (b, 0, p)),
            pl.BlockSpec(blk, lambda b, p: (b, 0, n_pair + p)),
            pl.BlockSpec(blk, lambda b, p: (b, 0, 2 * n_pair + p)),
            pl.BlockSpec((None, K_WINDOW, 2 * Q_BLOCK), lambda b, p: (layer * n_pair + p, 0, 0)),
        ],
        out_specs=pl.BlockSpec(blk, lambda b, p: (b, 0, p)),
        out_shape=jax.ShapeDtypeStruct((bsz, seq, A_WIDTH), BF16),
        scratch_shapes=[pltpu.VMEM((seq // LANES, VT_ROWS, LANES), BF16),
                        pltpu.VMEM((N_SLOTS, K_WINDOW, 2 * Q_BLOCK), F32),
                        pltpu.VMEM((N_SLOTS, 1, 2 * Q_BLOCK), F32),
                        pltpu.VMEM((N_SLOTS, K_WINDOW, 2 * Q_BLOCK), BF16)],
        compiler_params=pltpu.CompilerParams(
            dimension_semantics=("parallel", "parallel"), vmem_limit_bytes=VMEM_LIMIT),
        name="band_attention",
    )(qkva, qkva, qkva, bias)


def _chunk_cumsum(x):
    row = lax.broadcasted_iota(jnp.int32, x.shape, 0) % CHUNK
    d = 1
    while d < CHUNK:
        x = x + jnp.where(row >= d, pltpu.roll(x, d, axis=0), 0.0)
        d *= 2
    return x


def _gla_kernel(qe_ref, qo_ref, kt_ref, kend_ref, v_ref, vt_ref, gate_ref, decay_ref, g_ref,
                o_ref):
    seq = qe_ref.shape[0]
    lo = lax.broadcasted_iota(jnp.int32, (1, LANES), 1) < B_KEY_DIM
    ri = lax.broadcasted_iota(jnp.int32, (2 * CHUNK, CHUNK), 0)
    ci = lax.broadcasted_iota(jnp.int32, (2 * CHUNK, CHUNK), 1)
    causal = (ri % CHUNK) >= ci
    g = g_ref[...]

    def head_out(o, gh, gate):
        o = o * lax.rsqrt(jnp.mean(o * o, axis=-1, keepdims=True) + EPS) * gh
        return (o * gate).astype(BF16)

    group = GLA_GROUP * CHUNK

    def body(i, st):
        rows = pl.ds(pl.multiple_of(i * group, group), group)
        q_even = qe_ref[rows, :]
        q_odd = qo_ref[rows, :]
        kt = kt_ref[rows, :]
        kend = kend_ref[rows, :]
        v = v_ref[rows, :]
        decay = decay_ref[pl.ds(pl.multiple_of(i * GLA_GROUP, GLA_GROUP), GLA_GROUP), :]
        vt = vt_ref[i]
        pad = jnp.zeros((CHUNK, LANES), BF16)
        outs = []
        for c in range(GLA_GROUP):
            sl = slice(c * CHUNK, (c + 1) * CHUNK)
            q4 = jnp.concatenate([q_even[sl], q_odd[sl]], axis=0)
            v_even, v_odd = v[sl, :B_VAL_DIM], v[sl, B_VAL_DIM:]
            att = jnp.where(causal, _dot_nt(q4, kt[sl]), 0.0).astype(BF16)
            inter = _dot_nt(q4, st.astype(BF16))
            outs.append(jnp.concatenate(
                [_dot(att[:CHUNK], v_even) + inter[:CHUNK],
                 _dot(att[CHUNK:], v_odd) + inter[CHUNK:]], axis=1))
            pair_cols = slice((c // 2) * LANES, (c // 2 + 1) * LANES)
            kend_c = [kend[sl], pad] if c % 2 == 0 else [pad, kend[sl]]
            d_full = _dot(vt[:, pair_cols], jnp.concatenate(kend_c, axis=0))
            d_st = jnp.where(lo, d_full[:B_VAL_DIM], d_full[B_VAL_DIM:])
            st = st * decay[c:c + 1, :] + d_st
        o = jnp.concatenate(outs, axis=0)
        gate = gate_ref[rows, :]
        o_ref[rows, :B_VAL_DIM] = head_out(
            o[:, :B_VAL_DIM], g[:, :B_VAL_DIM], gate[:, :B_VAL_DIM])
        o_ref[rows, B_VAL_DIM:] = head_out(
            o[:, B_VAL_DIM:], g[:, B_VAL_DIM:], gate[:, B_VAL_DIM:])
        return st

    lax.fori_loop(0, seq // group, body, jnp.zeros((B_VAL_DIM, LANES), F32))


def _gla(gq, vt, gate, decay, g):
    bsz, seq, _ = gq.shape
    n_pair = B_HEADS // 2
    kblk = (None, seq, LANES)
    vblk = (None, seq, 2 * B_VAL_DIM)
    v_off = 4 * B_KEY_WIDTH // (2 * B_VAL_DIM)
    return pl.pallas_call(
        _gla_kernel,
        grid=(bsz, n_pair),
        in_specs=[
            pl.BlockSpec(kblk, lambda b, p: (b, 0, p)),
            pl.BlockSpec(kblk, lambda b, p: (b, 0, n_pair + p)),
            pl.BlockSpec(kblk, lambda b, p: (b, 0, 2 * n_pair + p)),
            pl.BlockSpec(kblk, lambda b, p: (b, 0, 3 * n_pair + p)),
            pl.BlockSpec(vblk, lambda b, p: (b, 0, v_off + p)),
            pl.BlockSpec((None,) + vt.shape[1:2] + (2 * B_VAL_DIM, vt.shape[3]),
                         lambda b, p: (b, 0, p, 0)),
            pl.BlockSpec(vblk, lambda b, p: (b, 0, p)),
            pl.BlockSpec((None, seq // CHUNK, LANES), lambda b, p: (b, 0, p)),
            pl.BlockSpec((1, 2 * B_VAL_DIM), lambda b, p: (0, p)),
        ],
        out_specs=pl.BlockSpec(vblk, lambda b, p: (b, 0, p)),
        out_shape=jax.ShapeDtypeStruct((bsz, seq, B_WIDTH), BF16),
        compiler_params=pltpu.CompilerParams(
            dimension_semantics=("parallel", "parallel"), vmem_limit_bytes=VMEM_LIMIT),
        name="gla",
    )(gq, gq, gq, gq, gq, vt, gate, decay, g)


def _merge_kernel(res_ref, g_ref, ya_ref, yb_ref, wgate_ref, wbr_ref, wout_ref, o_ref):
    slab = res_ref.shape[0] // ROW_SLABS
    for r in range(ROW_SLABS):
        rows = slice(r * slab, (r + 1) * slab)
        res = res_ref[rows, :]
        h = _rmsnorm_bf16(res, g_ref[...])
        gates = _dot_nt(h, wgate_ref[...])
        u_a = _dot(ya_ref[rows, :], wbr_ref[0])
        u_b = _dot(yb_ref[rows, :], wbr_ref[1])
        merged = _sigmoid(gates[:, :D_MODEL]) * u_a + _sigmoid(gates[:, D_MODEL:]) * u_b
        o_ref[rows, :] = res + _dot(merged.astype(BF16), wout_ref[...])


def _merge(res, g, ya, yb, wgate, wbr, wout, tm, layer):
    t = res.shape[0]
    return pl.pallas_call(
        _merge_kernel,
        grid=(t // tm,),
        in_specs=[
            pl.BlockSpec((tm, D_MODEL), lambda i: (i, 0)),
            _const_spec((1, D_MODEL)),
            pl.BlockSpec((tm, A_WIDTH), lambda i: (i, 0)),
            pl.BlockSpec((tm, B_WIDTH), lambda i: (i, 0)),
            _layer_spec(wgate, layer),
            _layer_spec(wbr, layer),
            _layer_spec(wout, layer),
        ],
        out_specs=pl.BlockSpec((tm, D_MODEL), lambda i: (i, 0)),
        out_shape=jax.ShapeDtypeStruct((t, D_MODEL), F32),
        compiler_params=pltpu.CompilerParams(
            dimension_semantics=("parallel",), vmem_limit_bytes=VMEM_LIMIT),
        name="merge",
    )(res, g, ya, yb, wgate, wbr, wout)


def _mlp_kernel(res_ref, g_ref, wup_ref, wdn_ref, gf_ref, o_ref, *, final_norm):
    slab = res_ref.shape[0] // ROW_SLABS
    for r in range(ROW_SLABS):
        rows = slice(r * slab, (r + 1) * slab)
        res = res_ref[rows, :]
        h = _rmsnorm_bf16(res, g_ref[...])
        acc = res
        for c in range(D_FF // D_MODEL):
            cols = slice(c * D_MODEL, (c + 1) * D_MODEL)
            up = jnp.maximum(_dot(h, wup_ref[:, cols]), 0.0)
            acc = acc + _dot((up * up).astype(BF16), wdn_ref[cols, :])
        if final_norm:
            ms = jnp.mean(acc * acc, axis=-1, keepdims=True)
            acc = (acc * lax.rsqrt(ms + EPS)) * gf_ref[...]
        o_ref[rows, :] = acc


def _mlp(res, g, wup, wdn, gf, tm, layer, final_norm):
    t = res.shape[0]
    return pl.pallas_call(
        functools.partial(_mlp_kernel, final_norm=final_norm),
        grid=(t // tm,),
        in_specs=[
            pl.BlockSpec((tm, D_MODEL), lambda i: (i, 0)),
            _const_spec((1, D_MODEL)),
            _layer_spec(wup, layer),
            _layer_spec(wdn, layer),
            _const_spec((1, D_MODEL)),
        ],
        out_specs=pl.BlockSpec((tm, D_MODEL), lambda i: (i, 0)),
        out_shape=jax.ShapeDtypeStruct((t, D_MODEL), F32),
        compiler_params=pltpu.CompilerParams(
            dimension_semantics=("parallel",), vmem_limit_bytes=VMEM_LIMIT),
        name="mlp",
    )(res, g, wup, wdn, gf)


def kernel(x, mix_norm_g, w_in, rel_bias, w_gate_lr, b_gate, gla_norm_g, w_branch, w_out,
           mlp_norm_g, w_up, w_down, final_norm_g):
    bsz, seq, d = x.shape
    t = bsz * seq
    tm = GLA_GROUP * CHUNK
    ab_cols = 3 * A_WIDTH + 2 * B_KEY_WIDTH + 2 * B_WIDTH
    w_in_t = jnp.swapaxes(w_in, 1, 2).astype(BF16)
    w_gates = w_in_t[:, ab_cols + GATE_RANK:]
    w_glr = w_gate_lr.astype(BF16)
    w_br = w_branch.astype(BF16)
    w_o = w_out.astype(BF16)
    w_u = w_up.astype(BF16)
    w_d = w_down.astype(BF16)
    bias = _bias_tables(rel_bias)

    res = x.reshape(t, d)
    for l in range(DEPTH):
        qkva, gq, gate, decay, vt = _inproj(res, mix_norm_g[l][None], w_in_t, w_glr,
                                            b_gate[l][None], tm, l)
        ya = _attention(qkva.reshape(bsz, seq, -1), bias, l)
        yb = _gla(gq.reshape(bsz, seq, -1), vt.reshape(bsz, seq // tm, B_WIDTH, tm),
                  gate.reshape(bsz, seq, -1), decay.reshape(bsz, seq // CHUNK, -1),
                  gla_norm_g[l][None])
        res = _merge(res, mix_norm_g[l][None], ya.reshape(t, -1), yb.reshape(t, -1),
                     w_gates, w_br, w_o, tm, l)
        res = _mlp(res, mlp_norm_g[l][None], w_u, w_d, final_norm_g[None], tm, l,
                   final_norm=(l == DEPTH - 1))
    return res.reshape(bsz, seq, d)
```

```python
import functools

import jax
import jax.numpy as jnp
from jax import lax
from jax.experimental import pallas as pl
from jax.experimental.pallas import tpu as pltpu

D_MODEL = 1024
DEPTH = 4
CHUNK = 64
EPS = 1e-6

A_HEADS = 8
A_HEAD_DIM = 64
A_WIDTH = A_HEADS * A_HEAD_DIM
N_PREV_CHUNKS = 8
PREV = N_PREV_CHUNKS * CHUNK
REL_CLIP = 128
N_REL = 2 * REL_CLIP + 1

B_HEADS = 4
B_KEY_DIM = 64
B_VAL_DIM = 128
B_KEY_WIDTH = B_HEADS * B_KEY_DIM
B_WIDTH = B_HEADS * B_VAL_DIM
GATE_RANK = 16
GATE_TAU = 16.0
D_FF = 4 * D_MODEL

LANES = 128
Q_BLOCK = 2 * CHUNK
K_WINDOW = PREV + Q_BLOCK
N_WIN_TILES = K_WINDOW // 128
VT_ROWS = 128 + 16
STAGE_LAG = 2
N_SLOTS = STAGE_LAG + 1
ZERO_BIAS_TILES = tuple(range(1, (PREV - REL_CLIP) // 128))
LOG2E = 1.4426950408889634
ROW_SLABS = 2
GLA_GROUP = 8
NEAR_START = PREV - REL_CLIP
NEAR_SEG = K_WINDOW - NEAR_START + Q_BLOCK
VMEM_LIMIT = 56 * 1024 * 1024

F32 = jnp.float32
BF16 = jnp.bfloat16
NEG = -1e30


def _rmsnorm_bf16(x, g):
    ms = jnp.mean(x * x, axis=-1, keepdims=True)
    return ((x * lax.rsqrt(ms + EPS)) * g).astype(BF16)


def _sigmoid(x):
    return 1.0 / (1.0 + jnp.exp(-x))


def _dot(a, b):
    return jnp.dot(a, b, preferred_element_type=F32)


def _dot_nt(a, b):
    return lax.dot_general(a, b, (((1,), (1,)), ((), ())), preferred_element_type=F32)


def _const_spec(shape):
    nd = len(shape)
    return pl.BlockSpec(shape, lambda *_: (0,) * nd)


def _layer_spec(stacked, layer):
    rest = stacked.shape[1:]
    return pl.BlockSpec((None,) + rest, lambda *_: (layer,) + (0,) * len(rest),
                        pipeline_mode=pl.Buffered(1))


def _bias_kernel(t_ref, o_ref):
    t = t_ref[...]
    t = (t - t[:, 0:1]) * LOG2E
    x = jnp.broadcast_to(t, (Q_BLOCK, NEAR_SEG))
    row = lax.broadcasted_iota(jnp.int32, (Q_BLOCK, NEAR_SEG), 0)
    shift = 1
    while shift < Q_BLOCK:
        x = jnp.where((row & shift) != 0, pltpu.roll(x, shift, axis=1), x)
        shift *= 2
    near = x[:, Q_BLOCK:]
    r = lax.broadcasted_iota(jnp.int32, near.shape, 0)
    c = lax.broadcasted_iota(jnp.int32, near.shape, 1)
    near = jnp.where((r < CHUNK) & (c >= near.shape[1] - CHUNK), NEG, near)
    for j in range(near.shape[1] // LANES):
        o_ref[NEAR_START + j * LANES:NEAR_START + (j + 1) * LANES, :] = (
            near[:, j * LANES:(j + 1) * LANES].T)
    kr = lax.broadcasted_iota(jnp.int32, (NEAR_START, Q_BLOCK), 0)
    qc = lax.broadcasted_iota(jnp.int32, (NEAR_START, Q_BLOCK), 1)
    o_ref[:NEAR_START, :] = jnp.where((kr < CHUNK) & (qc >= CHUNK), NEG, 0.0)


def _bias_tables(rel_bias):
    far = rel_bias[..., N_REL - 1:]
    n_far = NEAR_SEG - (N_REL - 2)
    seg = jnp.concatenate([
        jnp.broadcast_to(far, rel_bias.shape[:-1] + (n_far,)),
        rel_bias[..., N_REL - 2:0:-1],
    ], axis=-1)
    n = DEPTH * A_HEADS
    seg = seg.reshape(n, 1, NEAR_SEG)
    return pl.pallas_call(
        _bias_kernel,
        grid=(n,),
        in_specs=[pl.BlockSpec((None, 1, NEAR_SEG), lambda i: (i, 0, 0))],
        out_specs=pl.BlockSpec((None, K_WINDOW, Q_BLOCK), lambda i: (i // 2, 0, i % 2)),
        out_shape=jax.ShapeDtypeStruct((n // 2, K_WINDOW, 2 * Q_BLOCK), F32),
        name="bias_table",
    )(seg)


def _inproj_kernel(res_ref, g_ref, wa_ref, wb_ref, wlr_ref, wg_ref, bg_ref,
                   qkva_ref, gla_ref, gate_ref, decay_ref, vt_ref):
    tm = res_ref.shape[0]
    n_chunk = tm // CHUNK
    kw = B_KEY_WIDTH
    even = (lax.broadcasted_iota(jnp.int32, (1, kw), 1) // B_KEY_DIM) % 2 == 0
    h = _rmsnorm_bf16(res_ref[...], g_ref[...])
    lr = _dot_nt(h, wlr_ref[...])
    z = _dot(lr.astype(BF16), wg_ref[...]) + bg_ref[...]
    log_sig = jnp.minimum(z, 0.0) - jnp.log(1.0 + jnp.exp(-jnp.abs(z)))
    b = _chunk_cumsum(log_sig / GATE_TAU)
    b3 = b.reshape(n_chunk, CHUNK, kw)
    b_last = b3[:, CHUNK - 1:CHUNK, :]
    decay_ref[...] = jnp.exp(b_last).reshape(n_chunk, kw)
    to_end = jnp.exp(b_last - b3).reshape(tm, kw)
    pb = _dot_nt(h, wb_ref[...])
    qt = pb[:, :kw] * (B_KEY_DIM ** -0.5) * jnp.exp(b)
    k = pb[:, kw:2 * kw]
    gla_ref[:, :kw] = jnp.where(even, qt, 0.0).astype(BF16)
    gla_ref[:, kw:2 * kw] = jnp.where(even, 0.0, qt).astype(BF16)
    gla_ref[:, 2 * kw:3 * kw] = (k * jnp.exp(-b)).astype(BF16)
    gla_ref[:, 3 * kw:4 * kw] = (k * to_end).astype(BF16)
    v = pb[:, 2 * kw:2 * kw + B_WIDTH]
    gla_ref[:, 4 * kw:] = v.astype(BF16)
    vt_ref[...] = v.T.astype(BF16)
    r = pb[:, 2 * kw + B_WIDTH:]
    gate_ref[...] = r * _sigmoid(r)
    qkva_ref[...] = _dot_nt(h, wa_ref[...]).astype(BF16)


def _inproj(res, g, w_t, wg, bg, tm, layer):
    t = res.shape[0]
    gla_cols = 4 * B_KEY_WIDTH + B_WIDTH
    half_rows = 3 * A_WIDTH
    assert half_rows == 2 * B_KEY_WIDTH + 2 * B_WIDTH
    half = (None, half_rows, D_MODEL)
    lr_block = 2 * half_rows // GATE_RANK
    return pl.pallas_call(
        _inproj_kernel,
        grid=(t // tm,),
        in_specs=[
            pl.BlockSpec((tm, D_MODEL), lambda i: (i, 0)),
            _const_spec((1, D_MODEL)),
            pl.BlockSpec(half, lambda i: (layer, 0, 0)),
            pl.BlockSpec(half, lambda i: (layer, 1, 0)),
            pl.BlockSpec((None, GATE_RANK, D_MODEL), lambda i: (layer, lr_block, 0)),
            _layer_spec(wg, layer),
            _const_spec((1, B_KEY_WIDTH)),
        ],
        out_specs=[
            pl.BlockSpec((tm, 3 * A_WIDTH), lambda i: (i, 0)),
            pl.BlockSpec((tm, gla_cols), lambda i: (i, 0)),
            pl.BlockSpec((tm, B_WIDTH), lambda i: (i, 0)),
            pl.BlockSpec((tm // CHUNK, B_KEY_WIDTH), lambda i: (i, 0)),
            pl.BlockSpec((None, B_WIDTH, tm), lambda i: (i, 0, 0)),
        ],
        out_shape=[
            jax.ShapeDtypeStruct((t, 3 * A_WIDTH), BF16),
            jax.ShapeDtypeStruct((t, gla_cols), BF16),
            jax.ShapeDtypeStruct((t, B_WIDTH), F32),
            jax.ShapeDtypeStruct((t // CHUNK, B_KEY_WIDTH), F32),
            jax.ShapeDtypeStruct((t // tm, B_WIDTH, tm), BF16),
        ],
        compiler_params=pltpu.CompilerParams(
            dimension_semantics=("parallel",), vmem_limit_bytes=VMEM_LIMIT),
        name="inproj",
    )(res, g, w_t, w_t, w_t, wg, bg)


def _attn_kernel(q_ref, k_ref, v_ref, bias_ref, o_ref, vt_scr, s_scr, m_scr, e_scr):
    seq = q_ref.shape[0]
    n_blk = seq // Q_BLOCK
    n_prev = PREV // Q_BLOCK
    lo = lax.broadcasted_iota(jnp.int32, (1, LANES), 1) < A_HEAD_DIM

    for r in range(n_blk):
        vt_scr[r, :LANES, :] = v_ref[r * LANES:(r + 1) * LANES, :].astype(F32).T.astype(BF16)
        vt_scr[r, LANES:, :] = jnp.ones((VT_ROWS - LANES, LANES), BF16)

    def window(t):
        return max(t - n_prev, 0), min(t + 1, N_WIN_TILES)

    def scores(t):
        kt, n_tiles = window(t)
        rows = n_tiles * LANES
        q2 = q_ref[t * Q_BLOCK:(t + 1) * Q_BLOCK, :].astype(F32) * (A_HEAD_DIM ** -0.5 * LOG2E)
        q4 = jnp.concatenate(
            [jnp.where(lo, q2, 0.0), jnp.where(lo, 0.0, q2)], axis=0).astype(BF16)
        kw = k_ref[kt * LANES:kt * LANES + rows, :]
        s = _dot_nt(kw, q4)
        m = None
        for i in range(n_tiles):
            tile = s[i * LANES:(i + 1) * LANES, :]
            w = N_WIN_TILES - n_tiles + i
            if w not in ZERO_BIAS_TILES:
                tile = tile + bias_ref[w * LANES:(w + 1) * LANES, :]
            s_scr[t % N_SLOTS, i * LANES:(i + 1) * LANES, :] = tile
            tile_max = jnp.max(tile.reshape(LANES // 8, 8, 2 * Q_BLOCK), axis=0)
            m = tile_max if m is None else jnp.maximum(m, tile_max)
        m_scr[t % N_SLOTS] = jnp.max(m, axis=0, keepdims=True)

    def exps(t):
        rows = window(t)[1] * LANES
        d = (s_scr[t % N_SLOTS, :rows, :] - m_scr[t % N_SLOTS]).astype(BF16)
        e_scr[t % N_SLOTS, :rows, :] = jnp.exp2(d)

    def weighted_sum(t):
        kt, n_tiles = window(t)
        vt = jnp.concatenate([vt_scr[kt + i] for i in range(n_tiles)], axis=1)
        ot = _dot(vt, e_scr[t % N_SLOTS, :n_tiles * LANES, :])
        inv = 1.0 / ot[LANES:LANES + 1, :]
        o_pair = jnp.concatenate(
            [ot[:A_HEAD_DIM, :Q_BLOCK] * inv[:, :Q_BLOCK],
             ot[A_HEAD_DIM:LANES, Q_BLOCK:] * inv[:, Q_BLOCK:]], axis=0)
        o_ref[t * Q_BLOCK:(t + 1) * Q_BLOCK, :] = o_pair.T.astype(BF16)

    for step in range(n_blk + 2 * STAGE_LAG):
        if step < n_blk:
            scores(step)
        if 0 <= step - STAGE_LAG < n_blk:
            exps(step - STAGE_LAG)
        if 0 <= step - 2 * STAGE_LAG < n_blk:
            weighted_sum(step - 2 * STAGE_LAG)


def _attention(qkva, bias, layer):
    bsz, seq, _ = qkva.shape
    n_pair = A_HEADS // 2
    blk = (None, seq, LANES)
    return pl.pallas_call(
        _attn_kernel,
        grid=(bsz, n_pair),
        in_specs=[
            pl.BlockSpec(blk, lambda b, p: (b, 0, p)),
            pl.BlockSpec(blk, lambda b, p: (b, 0, n_pair + p)),
            pl.BlockSpec(blk, lambda b, p: (b, 0, 2 * n_pair + p)),
            pl.BlockSpec((None, K_WINDOW, 2 * Q_BLOCK), lambda b, p: (layer * n_pair + p, 0, 0)),
        ],
        out_specs=pl.BlockSpec(blk, lambda b, p: (b, 0, p)),
        out_shape=jax.ShapeDtypeStruct((bsz, seq, A_WIDTH), BF16),
        scratch_shapes=[pltpu.VMEM((seq // LANES, VT_ROWS, LANES), BF16),
                        pltpu.VMEM((N_SLOTS, K_WINDOW, 2 * Q_BLOCK), F32),
                        pltpu.VMEM((N_SLOTS, 1, 2 * Q_BLOCK), F32),
                        pltpu.VMEM((N_SLOTS, K_WINDOW, 2 * Q_BLOCK), BF16)],
        compiler_params=pltpu.CompilerParams(
            dimension_semantics=("parallel", "parallel"), vmem_limit_bytes=VMEM_LIMIT),
        name="band_attention",
    )(qkva, qkva, qkva, bias)


def _chunk_cumsum(x):
    row = lax.broadcasted_iota(jnp.int32, x.shape, 0) % CHUNK
    d = 1
    while d < CHUNK:
        x = x + jnp.where(row >= d, pltpu.roll(x, d, axis=0), 0.0)
        d *= 2
    return x


def _gla_kernel(qe_ref, qo_ref, kt_ref, kend_ref, v_ref, vt_ref, gate_ref, decay_ref, g_ref,
                o_ref):
    seq = qe_ref.shape[0]
    lo = lax.broadcasted_iota(jnp.int32, (1, LANES), 1) < B_KEY_DIM
    ri = lax.broadcasted_iota(jnp.int32, (2 * CHUNK, CHUNK), 0)
    ci = lax.broadcasted_iota(jnp.int32, (2 * CHUNK, CHUNK), 1)
    causal = (ri % CHUNK) >= ci
    g = g_ref[...]

    def head_out(o, gh, gate):
        o = o * lax.rsqrt(jnp.mean(o * o, axis=-1, keepdims=True) + EPS) * gh
        return (o * gate).astype(BF16)

    group = GLA_GROUP * CHUNK

    def body(i, st):
        rows = pl.ds(pl.multiple_of(i * group, group), group)
        q_even = qe_ref[rows, :]
        q_odd = qo_ref[rows, :]
        kt = kt_ref[rows, :]
        kend = kend_ref[rows, :]
        v = v_ref[rows, :]
        decay = decay_ref[pl.ds(pl.multiple_of(i * GLA_GROUP, GLA_GROUP), GLA_GROUP), :]
        vt = vt_ref[i]
        pad = jnp.zeros((CHUNK, LANES), BF16)
        outs = []
        for c in range(GLA_GROUP):
            sl = slice(c * CHUNK, (c + 1) * CHUNK)
            q4 = jnp.concatenate([q_even[sl], q_odd[sl]], axis=0)
            v_even, v_odd = v[sl, :B_VAL_DIM], v[sl, B_VAL_DIM:]
            att = jnp.where(causal, _dot_nt(q4, kt[sl]), 0.0).astype(BF16)
            inter = _dot_nt(q4, st.astype(BF16))
            outs.append(jnp.concatenate(
                [_dot(att[:CHUNK], v_even) + inter[:CHUNK],
                 _dot(att[CHUNK:], v_odd) + inter[CHUNK:]], axis=1))
            pair_cols = slice((c // 2) * LANES, (c // 2 + 1) * LANES)
            kend_c = [kend[sl], pad] if c % 2 == 0 else [pad, kend[sl]]
            d_full = _dot(vt[:, pair_cols], jnp.concatenate(kend_c, axis=0))
            d_st = jnp.where(lo, d_full[:B_VAL_DIM], d_full[B_VAL_DIM:])
            st = st * decay[c:c + 1, :] + d_st
        o = jnp.concatenate(outs, axis=0)
        gate = gate_ref[rows, :]
        o_ref[rows, :B_VAL_DIM] = head_out(
            o[:, :B_VAL_DIM], g[:, :B_VAL_DIM], gate[:, :B_VAL_DIM])
        o_ref[rows, B_VAL_DIM:] = head_out(
            o[:, B_VAL_DIM:], g[:, B_VAL_DIM:], gate[:, B_VAL_DIM:])
        return st

    lax.fori_loop(0, seq // group, body, jnp.zeros((B_VAL_DIM, LANES), F32))


def _gla(gq, vt, gate, decay, g):
    bsz, seq, _ = gq.shape
    n_pair = B_HEADS // 2
    kblk = (None, seq, LANES)
    vblk = (None, seq, 2 * B_VAL_DIM)
    v_off = 4 * B_KEY_WIDTH // (2 * B_VAL_DIM)
    return pl.pallas_call(
        _gla_kernel,
        grid=(bsz, n_pair),
        in_specs=[
            pl.BlockSpec(kblk, lambda b, p: (b, 0, p)),
            pl.BlockSpec(kblk, lambda b, p: (b, 0, n_pair + p)),
            pl.BlockSpec(kblk, lambda b, p: (b, 0, 2 * n_pair + p)),
            pl.BlockSpec(kblk, lambda b, p: (b, 0, 3 * n_pair + p)),
            pl.BlockSpec(vblk, lambda b, p: (b, 0, v_off + p)),
            pl.BlockSpec((None,) + vt.shape[1:2] + (2 * B_VAL_DIM, vt.shape[3]),
                         lambda b, p: (b, 0, p, 0)),
            pl.BlockSpec(vblk, lambda b, p: (b, 0, p)),
            pl.BlockSpec((None, seq // CHUNK, LANES), lambda b, p: (b, 0, p)),
            pl.BlockSpec((1, 2 * B_VAL_DIM), lambda b, p: (0, p)),
        ],
        out_specs=pl.BlockSpec(vblk, lambda b, p: (b, 0, p)),
        out_shape=jax.ShapeDtypeStruct((bsz, seq, B_WIDTH), BF16),
        compiler_params=pltpu.CompilerParams(
            dimension_semantics=("parallel", "parallel"), vmem_limit_bytes=VMEM_LIMIT),
        name="gla",
    )(gq, gq, gq, gq, gq, vt, gate, decay, g)


def _merge_kernel(res_ref, g_ref, ya_ref, yb_ref, wgate_ref, wbr_ref, wout_ref, o_ref):
    slab = res_ref.shape[0] // ROW_SLABS
    for r in range(ROW_SLABS):
        rows = slice(r * slab, (r + 1) * slab)
        res = res_ref[rows, :]
        h = _rmsnorm_bf16(res, g_ref[...])
        gates = _dot_nt(h, wgate_ref[...])
        u_a = _dot(ya_ref[rows, :], wbr_ref[0])
        u_b = _dot(yb_ref[rows, :], wbr_ref[1])
        merged = _sigmoid(gates[:, :D_MODEL]) * u_a + _sigmoid(gates[:, D_MODEL:]) * u_b
        o_ref[rows, :] = res + _dot(merged.astype(BF16), wout_ref[...])


def _merge(res, g, ya, yb, wgate, wbr, wout, tm, layer):
    t = res.shape[0]
    return pl.pallas_call(
        _merge_kernel,
        grid=(t // tm,),
        in_specs=[
            pl.BlockSpec((tm, D_MODEL), lambda i: (i, 0)),
            _const_spec((1, D_MODEL)),
            pl.BlockSpec((tm, A_WIDTH), lambda i: (i, 0)),
            pl.BlockSpec((tm, B_WIDTH), lambda i: (i, 0)),
            _layer_spec(wgate, layer),
            _layer_spec(wbr, layer),
            _layer_spec(wout, layer),
        ],
        out_specs=pl.BlockSpec((tm, D_MODEL), lambda i: (i, 0)),
        out_shape=jax.ShapeDtypeStruct((t, D_MODEL), F32),
        compiler_params=pltpu.CompilerParams(
            dimension_semantics=("parallel",), vmem_limit_bytes=VMEM_LIMIT),
        name="merge",
    )(res, g, ya, yb, wgate, wbr, wout)


def _mlp_kernel(res_ref, g_ref, wup_ref, wdn_ref, gf_ref, o_ref, *, final_norm):
    slab = res_ref.shape[0] // ROW_SLABS
    for r in range(ROW_SLABS):
        rows = slice(r * slab, (r + 1) * slab)
        res = res_ref[rows, :]
        h = _rmsnorm_bf16(res, g_ref[...])
        acc = res
        for c in range(D_FF // D_MODEL):
            cols = slice(c * D_MODEL, (c + 1) * D_MODEL)
            up = jnp.maximum(_dot(h, wup_ref[:, cols]), 0.0)
            acc = acc + _dot((up * up).astype(BF16), wdn_ref[cols, :])
        if final_norm:
            ms = jnp.mean(acc * acc, axis=-1, keepdims=True)
            acc = (acc * lax.rsqrt(ms + EPS)) * gf_ref[...]
        o_ref[rows, :] = acc


def _mlp(res, g, wup, wdn, gf, tm, layer, final_norm):
    t = res.shape[0]
    return pl.pallas_call(
        functools.partial(_mlp_kernel, final_norm=final_norm),
        grid=(t // tm,),
        in_specs=[
            pl.BlockSpec((tm, D_MODEL), lambda i: (i, 0)),
            _const_spec((1, D_MODEL)),
            _layer_spec(wup, layer),
            _layer_spec(wdn, layer),
            _const_spec((1, D_MODEL)),
        ],
        out_specs=pl.BlockSpec((tm, D_MODEL), lambda i: (i, 0)),
        out_shape=jax.ShapeDtypeStruct((t, D_MODEL), F32),
        compiler_params=pltpu.CompilerParams(
            dimension_semantics=("parallel",), vmem_limit_bytes=VMEM_LIMIT),
        name="mlp",
    )(res, g, wup, wdn, gf)


def kernel(x, mix_norm_g, w_in, rel_bias, w_gate_lr, b_gate, gla_norm_g, w_branch, w_out,
           mlp_norm_g, w_up, w_down, final_norm_g):
    bsz, seq, d = x.shape
    t = bsz * seq
    tm = GLA_GROUP * CHUNK
    ab_cols = 3 * A_WIDTH + 2 * B_KEY_WIDTH + 2 * B_WIDTH
    w_in_t = jnp.swapaxes(w_in, 1, 2).astype(BF16)
    w_gates = w_in_t[:, ab_cols + GATE_RANK:]
    w_glr = w_gate_lr.astype(BF16)
    w_br = w_branch.astype(BF16)
    w_o = w_out.astype(BF16)
    w_u = w_up.astype(BF16)
    w_d = w_down.astype(BF16)
    bias = _bias_tables(rel_bias)

    res = x.reshape(t, d)
    for l in range(DEPTH):
        qkva, gq, gate, decay, vt = _inproj(res, mix_norm_g[l][None], w_in_t, w_glr,
                                            b_gate[l][None], tm, l)
        ya = _attention(qkva.reshape(bsz, seq, -1), bias, l)
        yb = _gla(gq.reshape(bsz, seq, -1), vt.reshape(bsz, seq // tm, B_WIDTH, tm),
                  gate.reshape(bsz, seq, -1), decay.reshape(bsz, seq // CHUNK, -1),
                  gla_norm_g[l][None])
        res = _merge(res, mix_norm_g[l][None], ya.reshape(t, -1), yb.reshape(t, -1),
                     w_gates, w_br, w_o, 2 * tm, l)
        res = _mlp(res, mlp_norm_g[l][None], w_u, w_d, final_norm_g[None], 2 * tm, l,
                   final_norm=(l == DEPTH - 1))
    return res.reshape(bsz, seq, d)
```

```python
import functools

import jax
import jax.numpy as jnp
from jax import lax
from jax.experimental import pallas as pl
from jax.experimental.pallas import tpu as pltpu

D_MODEL = 1024
DEPTH = 4
CHUNK = 64
EPS = 1e-6

A_HEADS = 8
A_HEAD_DIM = 64
A_WIDTH = A_HEADS * A_HEAD_DIM
N_PREV_CHUNKS = 8
PREV = N_PREV_CHUNKS * CHUNK
REL_CLIP = 128
N_REL = 2 * REL_CLIP + 1

B_HEADS = 4
B_KEY_DIM = 64
B_VAL_DIM = 128
B_KEY_WIDTH = B_HEADS * B_KEY_DIM
B_WIDTH = B_HEADS * B_VAL_DIM
GATE_RANK = 16
GATE_TAU = 16.0
D_FF = 4 * D_MODEL

LANES = 128
Q_BLOCK = 2 * CHUNK
K_WINDOW = PREV + Q_BLOCK
N_WIN_TILES = K_WINDOW // 128
VT_ROWS = 128 + 16
STAGE_LAG = 2
N_SLOTS = STAGE_LAG + 1
ZERO_BIAS_TILES = tuple(range(1, (PREV - REL_CLIP) // 128))
LOG2E = 1.4426950408889634
ROW_SLABS = 2
GLA_GROUP = 16
DENSE_TM = 1024
NEAR_START = PREV - REL_CLIP
NEAR_SEG = K_WINDOW - NEAR_START + Q_BLOCK
VMEM_LIMIT = 56 * 1024 * 1024

F32 = jnp.float32
BF16 = jnp.bfloat16
NEG = -1e30


def _rmsnorm_bf16(x, g):
    ms = jnp.mean(x * x, axis=-1, keepdims=True)
    return ((x * lax.rsqrt(ms + EPS)) * g).astype(BF16)


def _sigmoid(x):
    return 1.0 / (1.0 + jnp.exp(-x))


def _dot(a, b):
    return jnp.dot(a, b, preferred_element_type=F32)


def _dot_nt(a, b):
    return lax.dot_general(a, b, (((1,), (1,)), ((), ())), preferred_element_type=F32)


def _const_spec(shape):
    nd = len(shape)
    return pl.BlockSpec(shape, lambda *_: (0,) * nd)


def _layer_spec(stacked, layer):
    rest = stacked.shape[1:]
    return pl.BlockSpec((None,) + rest, lambda *_: (layer,) + (0,) * len(rest),
                        pipeline_mode=pl.Buffered(1))


def _bias_kernel(t_ref, o_ref):
    t = t_ref[...]
    t = (t - t[:, 0:1]) * LOG2E
    x = jnp.broadcast_to(t, (Q_BLOCK, NEAR_SEG))
    row = lax.broadcasted_iota(jnp.int32, (Q_BLOCK, NEAR_SEG), 0)
    shift = 1
    while shift < Q_BLOCK:
        x = jnp.where((row & shift) != 0, pltpu.roll(x, shift, axis=1), x)
        shift *= 2
    near = x[:, Q_BLOCK:]
    r = lax.broadcasted_iota(jnp.int32, near.shape, 0)
    c = lax.broadcasted_iota(jnp.int32, near.shape, 1)
    near = jnp.where((r < CHUNK) & (c >= near.shape[1] - CHUNK), NEG, near)
    for j in range(near.shape[1] // LANES):
        o_ref[NEAR_START + j * LANES:NEAR_START + (j + 1) * LANES, :] = (
            near[:, j * LANES:(j + 1) * LANES].T)
    kr = lax.broadcasted_iota(jnp.int32, (NEAR_START, Q_BLOCK), 0)
    qc = lax.broadcasted_iota(jnp.int32, (NEAR_START, Q_BLOCK), 1)
    o_ref[:NEAR_START, :] = jnp.where((kr < CHUNK) & (qc >= CHUNK), NEG, 0.0)


def _bias_tables(rel_bias):
    far = rel_bias[..., N_REL - 1:]
    n_far = NEAR_SEG - (N_REL - 2)
    seg = jnp.concatenate([
        jnp.broadcast_to(far, rel_bias.shape[:-1] + (n_far,)),
        rel_bias[..., N_REL - 2:0:-1],
    ], axis=-1)
    n = DEPTH * A_HEADS
    seg = seg.reshape(n, 1, NEAR_SEG)
    return pl.pallas_call(
        _bias_kernel,
        grid=(n,),
        in_specs=[pl.BlockSpec((None, 1, NEAR_SEG), lambda i: (i, 0, 0))],
        out_specs=pl.BlockSpec((None, K_WINDOW, Q_BLOCK), lambda i: (i // 2, 0, i % 2)),
        out_shape=jax.ShapeDtypeStruct((n // 2, K_WINDOW, 2 * Q_BLOCK), F32),
        name="bias_table",
    )(seg)


def _inproj_kernel(res_ref, g_ref, wa_ref, wb_ref, wlr_ref, wg_ref, bg_ref,
                   qkva_ref, gla_ref, gate_ref, decay_ref, vt_ref):
    tm = res_ref.shape[0]
    n_chunk = tm // CHUNK
    kw = B_KEY_WIDTH
    even = (lax.broadcasted_iota(jnp.int32, (1, kw), 1) // B_KEY_DIM) % 2 == 0
    h = _rmsnorm_bf16(res_ref[...], g_ref[...])
    lr = _dot_nt(h, wlr_ref[...])
    z = _dot(lr.astype(BF16), wg_ref[...]) + bg_ref[...]
    log_sig = jnp.minimum(z, 0.0) - jnp.log(1.0 + jnp.exp(-jnp.abs(z)))
    b = _chunk_cumsum(log_sig / GATE_TAU)
    b3 = b.reshape(n_chunk, CHUNK, kw)
    b_last = b3[:, CHUNK - 1:CHUNK, :]
    decay_ref[...] = jnp.exp(b_last).reshape(n_chunk, kw)
    to_end = jnp.exp(b_last - b3).reshape(tm, kw)
    pb = _dot_nt(h, wb_ref[...])
    qt = pb[:, :kw] * (B_KEY_DIM ** -0.5) * jnp.exp(b)
    k = pb[:, kw:2 * kw]
    gla_ref[:, :kw] = jnp.where(even, qt, 0.0).astype(BF16)
    gla_ref[:, kw:2 * kw] = jnp.where(even, 0.0, qt).astype(BF16)
    gla_ref[:, 2 * kw:3 * kw] = (k * jnp.exp(-b)).astype(BF16)
    gla_ref[:, 3 * kw:4 * kw] = (k * to_end).astype(BF16)
    v = pb[:, 2 * kw:2 * kw + B_WIDTH]
    gla_ref[:, 4 * kw:] = v.astype(BF16)
    vt_ref[...] = v.T.astype(BF16)
    r = pb[:, 2 * kw + B_WIDTH:]
    gate_ref[...] = r * _sigmoid(r)
    qkva_ref[...] = _dot_nt(h, wa_ref[...]).astype(BF16)


def _inproj(res, g, w_t, wg, bg, tm, layer):
    t = res.shape[0]
    gla_cols = 4 * B_KEY_WIDTH + B_WIDTH
    half_rows = 3 * A_WIDTH
    assert half_rows == 2 * B_KEY_WIDTH + 2 * B_WIDTH
    half = (None, half_rows, D_MODEL)
    lr_block = 2 * half_rows // GATE_RANK
    return pl.pallas_call(
        _inproj_kernel,
        grid=(t // tm,),
        in_specs=[
            pl.BlockSpec((tm, D_MODEL), lambda i: (i, 0)),
            _const_spec((1, D_MODEL)),
            pl.BlockSpec(half, lambda i: (layer, 0, 0)),
            pl.BlockSpec(half, lambda i: (layer, 1, 0)),
            pl.BlockSpec((None, GATE_RANK, D_MODEL), lambda i: (layer, lr_block, 0)),
            _layer_spec(wg, layer),
            _const_spec((1, B_KEY_WIDTH)),
        ],
        out_specs=[
            pl.BlockSpec((tm, 3 * A_WIDTH), lambda i: (i, 0)),
            pl.BlockSpec((tm, gla_cols), lambda i: (i, 0)),
            pl.BlockSpec((tm, B_WIDTH), lambda i: (i, 0)),
            pl.BlockSpec((tm // CHUNK, B_KEY_WIDTH), lambda i: (i, 0)),
            pl.BlockSpec((None, B_WIDTH, tm), lambda i: (i, 0, 0)),
        ],
        out_shape=[
            jax.ShapeDtypeStruct((t, 3 * A_WIDTH), BF16),
            jax.ShapeDtypeStruct((t, gla_cols), BF16),
            jax.ShapeDtypeStruct((t, B_WIDTH), F32),
            jax.ShapeDtypeStruct((t // CHUNK, B_KEY_WIDTH), F32),
            jax.ShapeDtypeStruct((t // tm, B_WIDTH, tm), BF16),
        ],
        compiler_params=pltpu.CompilerParams(
            dimension_semantics=("parallel",), vmem_limit_bytes=VMEM_LIMIT),
        name="inproj",
    )(res, g, w_t, w_t, w_t, wg, bg)


def _attn_kernel(q_ref, k_ref, v_ref, bias_ref, o_ref, vt_scr, s_scr, m_scr, e_scr):
    seq = q_ref.shape[0]
    n_blk = seq // Q_BLOCK
    n_prev = PREV // Q_BLOCK
    lo = lax.broadcasted_iota(jnp.int32, (1, LANES), 1) < A_HEAD_DIM

    for r in range(n_blk):
        vt_scr[r, :LANES, :] = v_ref[r * LANES:(r + 1) * LANES, :].astype(F32).T.astype(BF16)
        vt_scr[r, LANES:, :] = jnp.ones((VT_ROWS - LANES, LANES), BF16)

    def window(t):
        return max(t - n_prev, 0), min(t + 1, N_WIN_TILES)

    def scores(t):
        kt, n_tiles = window(t)
        rows = n_tiles * LANES
        q2 = q_ref[t * Q_BLOCK:(t + 1) * Q_BLOCK, :].astype(F32) * (A_HEAD_DIM ** -0.5 * LOG2E)
        q4 = jnp.concatenate(
            [jnp.where(lo, q2, 0.0), jnp.where(lo, 0.0, q2)], axis=0).astype(BF16)
        kw = k_ref[kt * LANES:kt * LANES + rows, :]
        s = _dot_nt(kw, q4)
        m = None
        for i in range(n_tiles):
            tile = s[i * LANES:(i + 1) * LANES, :]
            w = N_WIN_TILES - n_tiles + i
            if w not in ZERO_BIAS_TILES:
                tile = tile + bias_ref[w * LANES:(w + 1) * LANES, :]
            s_scr[t % N_SLOTS, i * LANES:(i + 1) * LANES, :] = tile
            tile_max = jnp.max(tile.reshape(LANES // 8, 8, 2 * Q_BLOCK), axis=0)
            m = tile_max if m is None else jnp.maximum(m, tile_max)
        m_scr[t % N_SLOTS] = jnp.max(m, axis=0, keepdims=True)

    def exps(t):
        rows = window(t)[1] * LANES
        d = (s_scr[t % N_SLOTS, :rows, :] - m_scr[t % N_SLOTS]).astype(BF16)
        e_scr[t % N_SLOTS, :rows, :] = jnp.exp2(d)

    def weighted_sum(t):
        kt, n_tiles = window(t)
        vt = jnp.concatenate([vt_scr[kt + i] for i in range(n_tiles)], axis=1)
        ot = _dot(vt, e_scr[t % N_SLOTS, :n_tiles * LANES, :])
        inv = 1.0 / ot[LANES:LANES + 1, :]
        o_pair = jnp.concatenate(
            [ot[:A_HEAD_DIM, :Q_BLOCK] * inv[:, :Q_BLOCK],
             ot[A_HEAD_DIM:LANES, Q_BLOCK:] * inv[:, Q_BLOCK:]], axis=0)
        o_ref[t * Q_BLOCK:(t + 1) * Q_BLOCK, :] = o_pair.T.astype(BF16)

    for step in range(n_blk + 2 * STAGE_LAG):
        if step < n_blk:
            scores(step)
        if 0 <= step - STAGE_LAG < n_blk:
            exps(step - STAGE_LAG)
        if 0 <= step - 2 * STAGE_LAG < n_blk:
            weighted_sum(step - 2 * STAGE_LAG)


def _attention(qkva, bias, layer):
    bsz, seq, _ = qkva.shape
    n_pair = A_HEADS // 2
    blk = (None, seq, LANES)
    return pl.pallas_call(
        _attn_kernel,
        grid=(bsz, n_pair),
        in_specs=[
            pl.BlockSpec(blk, lambda b, p: (b, 0, p)),
            pl.BlockSpec(blk, lambda b, p: (b, 0, n_pair + p)),
            pl.BlockSpec(blk, lambda b, p: (b, 0, 2 * n_pair + p)),
            pl.BlockSpec((None, K_WINDOW, 2 * Q_BLOCK), lambda b, p: (layer * n_pair + p, 0, 0)),
        ],
        out_specs=pl.BlockSpec(blk, lambda b, p: (b, 0, p)),
        out_shape=jax.ShapeDtypeStruct((bsz, seq, A_WIDTH), BF16),
        scratch_shapes=[pltpu.VMEM((seq // LANES, VT_ROWS, LANES), BF16),
                        pltpu.VMEM((N_SLOTS, K_WINDOW, 2 * Q_BLOCK), F32),
                        pltpu.VMEM((N_SLOTS, 1, 2 * Q_BLOCK), F32),
                        pltpu.VMEM((N_SLOTS, K_WINDOW, 2 * Q_BLOCK), BF16)],
        compiler_params=pltpu.CompilerParams(
            dimension_semantics=("parallel", "parallel"), vmem_limit_bytes=VMEM_LIMIT),
        name="band_attention",
    )(qkva, qkva, qkva, bias)


def _chunk_cumsum(x):
    row = lax.broadcasted_iota(jnp.int32, x.shape, 0) % CHUNK
    d = 1
    while d < CHUNK:
        x = x + jnp.where(row >= d, pltpu.roll(x, d, axis=0), 0.0)
        d *= 2
    return x


def _gla_kernel(qe_ref, qo_ref, kt_ref, kend_ref, v_ref, vt_ref, gate_ref, decay_ref, g_ref,
                o_ref):
    seq = qe_ref.shape[0]
    n_pair = B_HEADS // 2
    lo = lax.broadcasted_iota(jnp.int32, (1, LANES), 1) < B_KEY_DIM
    ri = lax.broadcasted_iota(jnp.int32, (2 * CHUNK, CHUNK), 0)
    ci = lax.broadcasted_iota(jnp.int32, (2 * CHUNK, CHUNK), 1)
    causal = (ri % CHUNK) >= ci
    group = GLA_GROUP * CHUNK

    def body(i, states):
        rows = pl.ds(pl.multiple_of(i * group, group), group)
        q_even = qe_ref[rows, :]
        q_odd = qo_ref[rows, :]
        kt = kt_ref[rows, :]
        kend = kend_ref[rows, :]
        v = v_ref[rows, :]
        decay = decay_ref[pl.ds(pl.multiple_of(i * GLA_GROUP, GLA_GROUP), GLA_GROUP), :]
        vt = vt_ref[i]
        pad = jnp.zeros((CHUNK, LANES), BF16)
        states = list(states)
        outs = []
        for c in range(GLA_GROUP):
            sl = slice(c * CHUNK, (c + 1) * CHUNK)
            pair_cols = slice((c // 2) * LANES, (c // 2 + 1) * LANES)
            o_c = []
            for p in range(n_pair):
                kl = slice(p * LANES, (p + 1) * LANES)
                ve = slice(2 * p * B_VAL_DIM, (2 * p + 1) * B_VAL_DIM)
                vo = slice((2 * p + 1) * B_VAL_DIM, (2 * p + 2) * B_VAL_DIM)
                st = states[p]
                q4 = jnp.concatenate([q_even[sl, kl], q_odd[sl, kl]], axis=0)
                att = jnp.where(causal, _dot_nt(q4, kt[sl, kl]), 0.0).astype(BF16)
                inter = _dot_nt(q4, st.astype(BF16))
                o_c += [_dot(att[:CHUNK], v[sl, ve]) + inter[:CHUNK],
                        _dot(att[CHUNK:], v[sl, vo]) + inter[CHUNK:]]
                kend_c = [kend[sl, kl], pad] if c % 2 == 0 else [pad, kend[sl, kl]]
                d_full = _dot(vt[2 * p * B_VAL_DIM:(2 * p + 2) * B_VAL_DIM, pair_cols],
                              jnp.concatenate(kend_c, axis=0))
                d_st = jnp.where(lo, d_full[:B_VAL_DIM], d_full[B_VAL_DIM:])
                states[p] = st * decay[c:c + 1, kl] + d_st
            outs.append(jnp.concatenate(o_c, axis=1))
        o = jnp.concatenate(outs, axis=0)
        gate = gate_ref[rows, :]
        for h in range(B_HEADS):
            hl = slice(h * B_VAL_DIM, (h + 1) * B_VAL_DIM)
            oh = o[:, hl]
            oh = oh * lax.rsqrt(jnp.mean(oh * oh, axis=-1, keepdims=True) + EPS) * g_ref[:, hl]
            o_ref[rows, hl] = (oh * gate[:, hl]).astype(BF16)
        return tuple(states)

    zero = jnp.zeros((B_VAL_DIM, LANES), F32)
    lax.fori_loop(0, seq // group, body, (zero,) * n_pair)


def _gla(gq, vt, gate, decay, g):
    bsz, seq, _ = gq.shape
    kblk = (None, seq, B_KEY_WIDTH)
    vblk = (None, seq, B_WIDTH)
    return pl.pallas_call(
        _gla_kernel,
        grid=(bsz,),
        in_specs=[pl.BlockSpec(kblk, lambda b, j=j: (b, 0, j)) for j in range(4)] + [
            pl.BlockSpec(vblk, lambda b: (b, 0, 4 * B_KEY_WIDTH // B_WIDTH)),
            pl.BlockSpec((None,) + vt.shape[1:], lambda b: (b, 0, 0, 0)),
            pl.BlockSpec(vblk, lambda b: (b, 0, 0)),
            pl.BlockSpec((None, seq // CHUNK, B_KEY_WIDTH), lambda b: (b, 0, 0)),
            _const_spec((1, B_WIDTH)),
        ],
        out_specs=pl.BlockSpec(vblk, lambda b: (b, 0, 0)),
        out_shape=jax.ShapeDtypeStruct((bsz, seq, B_WIDTH), BF16),
        compiler_params=pltpu.CompilerParams(
            dimension_semantics=("parallel",), vmem_limit_bytes=VMEM_LIMIT),
        name="gla",
    )(gq, gq, gq, gq, gq, vt, gate, decay, g)


def _merge_kernel(res_ref, g_ref, ya_ref, yb_ref, wgate_ref, wbr_ref, wout_ref, o_ref):
    slab = res_ref.shape[0] // ROW_SLABS
    for r in range(ROW_SLABS):
        rows = slice(r * slab, (r + 1) * slab)
        res = res_ref[rows, :]
        h = _rmsnorm_bf16(res, g_ref[...])
        gates = _dot_nt(h, wgate_ref[...])
        u_a = _dot(ya_ref[rows, :], wbr_ref[0])
        u_b = _dot(yb_ref[rows, :], wbr_ref[1])
        merged = _sigmoid(gates[:, :D_MODEL]) * u_a + _sigmoid(gates[:, D_MODEL:]) * u_b
        o_ref[rows, :] = res + _dot(merged.astype(BF16), wout_ref[...])


def _merge(res, g, ya, yb, wgate, wbr, wout, tm, layer):
    t = res.shape[0]
    return pl.pallas_call(
        _merge_kernel,
        grid=(t // tm,),
        in_specs=[
            pl.BlockSpec((tm, D_MODEL), lambda i: (i, 0)),
            _const_spec((1, D_MODEL)),
            pl.BlockSpec((tm, A_WIDTH), lambda i: (i, 0)),
            pl.BlockSpec((tm, B_WIDTH), lambda i: (i, 0)),
            _layer_spec(wgate, layer),
            _layer_spec(wbr, layer),
            _layer_spec(wout, layer),
        ],
        out_specs=pl.BlockSpec((tm, D_MODEL), lambda i: (i, 0)),
        out_shape=jax.ShapeDtypeStruct((t, D_MODEL), F32),
        compiler_params=pltpu.CompilerParams(
            dimension_semantics=("parallel",), vmem_limit_bytes=VMEM_LIMIT),
        name="merge",
    )(res, g, ya, yb, wgate, wbr, wout)


def _mlp_kernel(res_ref, g_ref, wup_ref, wdn_ref, gf_ref, o_ref, *, final_norm):
    slab = res_ref.shape[0] // ROW_SLABS
    for r in range(ROW_SLABS):
        rows = slice(r * slab, (r + 1) * slab)
        res = res_ref[rows, :]
        h = _rmsnorm_bf16(res, g_ref[...])
        acc = res
        for c in range(D_FF // D_MODEL):
            cols = slice(c * D_MODEL, (c + 1) * D_MODEL)
            up = jnp.maximum(_dot(h, wup_ref[:, cols]), 0.0)
            acc = acc + _dot((up * up).astype(BF16), wdn_ref[cols, :])
        if final_norm:
            ms = jnp.mean(acc * acc, axis=-1, keepdims=True)
            acc = (acc * lax.rsqrt(ms + EPS)) * gf_ref[...]
        o_ref[rows, :] = acc


def _mlp(res, g, wup, wdn, gf, tm, layer, final_norm):
    t = res.shape[0]
    return pl.pallas_call(
        functools.partial(_mlp_kernel, final_norm=final_norm),
        grid=(t // tm,),
        in_specs=[
            pl.BlockSpec((tm, D_MODEL), lambda i: (i, 0)),
            _const_spec((1, D_MODEL)),
            _layer_spec(wup, layer),
            _layer_spec(wdn, layer),
            _const_spec((1, D_MODEL)),
        ],
        out_specs=pl.BlockSpec((tm, D_MODEL), lambda i: (i, 0)),
        out_shape=jax.ShapeDtypeStruct((t, D_MODEL), F32),
        compiler_params=pltpu.CompilerParams(
            dimension_semantics=("parallel",), vmem_limit_bytes=VMEM_LIMIT),
        name="mlp",
    )(res, g, wup, wdn, gf)


def kernel(x, mix_norm_g, w_in, rel_bias, w_gate_lr, b_gate, gla_norm_g, w_branch, w_out,
           mlp_norm_g, w_up, w_down, final_norm_g):
    bsz, seq, d = x.shape
    t = bsz * seq
    tm = GLA_GROUP * CHUNK
    ab_cols = 3 * A_WIDTH + 2 * B_KEY_WIDTH + 2 * B_WIDTH
    w_in_t = jnp.swapaxes(w_in, 1, 2).astype(BF16)
    w_gates = w_in_t[:, ab_cols + GATE_RANK:]
    w_glr = w_gate_lr.astype(BF16)
    w_br = w_branch.astype(BF16)
    w_o = w_out.astype(BF16)
    w_u = w_up.astype(BF16)
    w_d = w_down.astype(BF16)
    bias = _bias_tables(rel_bias)

    res = x.reshape(t, d)
    for l in range(DEPTH):
        qkva, gq, gate, decay, vt = _inproj(res, mix_norm_g[l][None], w_in_t, w_glr,
                                            b_gate[l][None], tm, l)
        ya = _attention(qkva.reshape(bsz, seq, -1), bias, l)
        yb = _gla(gq.reshape(bsz, seq, -1), vt.reshape(bsz, seq // tm, B_WIDTH, tm),
                  gate.reshape(bsz, seq, -1), decay.reshape(bsz, seq // CHUNK, -1),
                  gla_norm_g[l][None])
        res = _merge(res, mix_norm_g[l][None], ya.reshape(t, -1), yb.reshape(t, -1),
                     w_gates, w_br, w_o, DENSE_TM, l)
        res = _mlp(res, mlp_norm_g[l][None], w_u, w_d, final_norm_g[None], DENSE_TM, l,
                   final_norm=(l == DEPTH - 1))
    return res.reshape(bsz, seq, d)
```

```python
import functools

import jax
import jax.numpy as jnp
from jax import lax
from jax.experimental import pallas as pl
from jax.experimental.pallas import tpu as pltpu

D_MODEL = 1024
DEPTH = 4
CHUNK = 64
EPS = 1e-6

A_HEADS = 8
A_HEAD_DIM = 64
A_WIDTH = A_HEADS * A_HEAD_DIM
N_PREV_CHUNKS = 8
PREV = N_PREV_CHUNKS * CHUNK
REL_CLIP = 128
N_REL = 2 * REL_CLIP + 1

B_HEADS = 4
B_KEY_DIM = 64
B_VAL_DIM = 128
B_KEY_WIDTH = B_HEADS * B_KEY_DIM
B_WIDTH = B_HEADS * B_VAL_DIM
GATE_RANK = 16
GATE_TAU = 16.0
D_FF = 4 * D_MODEL

LANES = 128
Q_BLOCK = 2 * CHUNK
K_WINDOW = PREV + Q_BLOCK
N_WIN_TILES = K_WINDOW // 128
VT_ROWS = 128 + 16
STAGE_LAG = 2
N_SLOTS = STAGE_LAG + 1
ZERO_BIAS_TILES = tuple(range(1, (PREV - REL_CLIP) // 128))
LOG2E = 1.4426950408889634
ROW_SLABS = 2
GLA_GROUP = 16
DENSE_TM = 1024
NEAR_START = PREV - REL_CLIP
NEAR_SEG = K_WINDOW - NEAR_START + Q_BLOCK
VMEM_LIMIT = 56 * 1024 * 1024

F32 = jnp.float32
BF16 = jnp.bfloat16
NEG = -1e30


def _rmsnorm_bf16(x, g):
    ms = jnp.mean(x * x, axis=-1, keepdims=True)
    return ((x * lax.rsqrt(ms + EPS)) * g).astype(BF16)


def _sigmoid(x):
    return 1.0 / (1.0 + jnp.exp(-x))


def _dot(a, b):
    return jnp.dot(a, b, preferred_element_type=F32)


def _dot_nt(a, b):
    return lax.dot_general(a, b, (((1,), (1,)), ((), ())), preferred_element_type=F32)


def _const_spec(shape):
    nd = len(shape)
    return pl.BlockSpec(shape, lambda *_: (0,) * nd)


def _layer_spec(stacked, layer):
    rest = stacked.shape[1:]
    return pl.BlockSpec((None,) + rest, lambda *_: (layer,) + (0,) * len(rest),
                        pipeline_mode=pl.Buffered(1))


def _bias_kernel(t_ref, o_ref):
    t = t_ref[...]
    t = (t - t[:, 0:1]) * LOG2E
    x = jnp.broadcast_to(t, (Q_BLOCK, NEAR_SEG))
    row = lax.broadcasted_iota(jnp.int32, (Q_BLOCK, NEAR_SEG), 0)
    shift = 1
    while shift < Q_BLOCK:
        x = jnp.where((row & shift) != 0, pltpu.roll(x, shift, axis=1), x)
        shift *= 2
    near = x[:, Q_BLOCK:]
    r = lax.broadcasted_iota(jnp.int32, near.shape, 0)
    c = lax.broadcasted_iota(jnp.int32, near.shape, 1)
    near = jnp.where((r < CHUNK) & (c >= near.shape[1] - CHUNK), NEG, near)
    for j in range(near.shape[1] // LANES):
        o_ref[NEAR_START + j * LANES:NEAR_START + (j + 1) * LANES, :] = (
            near[:, j * LANES:(j + 1) * LANES].T)
    kr = lax.broadcasted_iota(jnp.int32, (NEAR_START, Q_BLOCK), 0)
    qc = lax.broadcasted_iota(jnp.int32, (NEAR_START, Q_BLOCK), 1)
    o_ref[:NEAR_START, :] = jnp.where((kr < CHUNK) & (qc >= CHUNK), NEG, 0.0)


def _bias_tables(rel_bias):
    far = rel_bias[..., N_REL - 1:]
    n_far = NEAR_SEG - (N_REL - 2)
    seg = jnp.concatenate([
        jnp.broadcast_to(far, rel_bias.shape[:-1] + (n_far,)),
        rel_bias[..., N_REL - 2:0:-1],
    ], axis=-1)
    n = DEPTH * A_HEADS
    seg = seg.reshape(n, 1, NEAR_SEG)
    return pl.pallas_call(
        _bias_kernel,
        grid=(n,),
        in_specs=[pl.BlockSpec((None, 1, NEAR_SEG), lambda i: (i, 0, 0))],
        out_specs=pl.BlockSpec((None, K_WINDOW, Q_BLOCK), lambda i: (i // 2, 0, i % 2)),
        out_shape=jax.ShapeDtypeStruct((n // 2, K_WINDOW, 2 * Q_BLOCK), F32),
        name="bias_table",
    )(seg)


def _inproj_kernel(res_ref, g_ref, wa_ref, wb_ref, wlr_ref, wg_ref, bg_ref,
                   qkva_ref, gla_ref, gate_ref, decay_ref, vt_ref):
    tm = res_ref.shape[0]
    n_chunk = tm // CHUNK
    kw = B_KEY_WIDTH
    even = (lax.broadcasted_iota(jnp.int32, (1, kw), 1) // B_KEY_DIM) % 2 == 0
    h = _rmsnorm_bf16(res_ref[...], g_ref[...])
    lr = _dot_nt(h, wlr_ref[...])
    z = _dot(lr.astype(BF16), wg_ref[...]) + bg_ref[...]
    log_sig = jnp.minimum(z, 0.0) - jnp.log(1.0 + jnp.exp(-jnp.abs(z)))
    b = _chunk_cumsum(log_sig / GATE_TAU)
    b3 = b.reshape(n_chunk, CHUNK, kw)
    b_last = b3[:, CHUNK - 1:CHUNK, :]
    decay_ref[...] = jnp.exp(b_last).reshape(n_chunk, kw)
    to_end = jnp.exp(b_last - b3).reshape(tm, kw)
    pb = _dot_nt(h, wb_ref[...])
    qt = pb[:, :kw] * (B_KEY_DIM ** -0.5) * jnp.exp(b)
    k = pb[:, kw:2 * kw]
    gla_ref[:, :kw] = jnp.where(even, qt, 0.0).astype(BF16)
    gla_ref[:, kw:2 * kw] = jnp.where(even, 0.0, qt).astype(BF16)
    gla_ref[:, 2 * kw:3 * kw] = (k * jnp.exp(-b)).astype(BF16)
    gla_ref[:, 3 * kw:4 * kw] = (k * to_end).astype(BF16)
    v = pb[:, 2 * kw:2 * kw + B_WIDTH]
    gla_ref[:, 4 * kw:] = v.astype(BF16)
    vt_ref[...] = v.T.astype(BF16)
    r = pb[:, 2 * kw + B_WIDTH:]
    gate_ref[...] = r * _sigmoid(r)
    qkva_ref[...] = _dot_nt(h, wa_ref[...]).astype(BF16)


def _inproj(res, g, w_t, wg, bg, tm, layer):
    t = res.shape[0]
    gla_cols = 4 * B_KEY_WIDTH + B_WIDTH
    half_rows = 3 * A_WIDTH
    assert half_rows == 2 * B_KEY_WIDTH + 2 * B_WIDTH
    half = (None, half_rows, D_MODEL)
    lr_block = 2 * half_rows // GATE_RANK
    return pl.pallas_call(
        _inproj_kernel,
        grid=(t // tm,),
        in_specs=[
            pl.BlockSpec((tm, D_MODEL), lambda i: (i, 0)),
            _const_spec((1, D_MODEL)),
            pl.BlockSpec(half, lambda i: (layer, 0, 0)),
            pl.BlockSpec(half, lambda i: (layer, 1, 0)),
            pl.BlockSpec((None, GATE_RANK, D_MODEL), lambda i: (layer, lr_block, 0)),
            _layer_spec(wg, layer),
            _const_spec((1, B_KEY_WIDTH)),
        ],
        out_specs=[
            pl.BlockSpec((tm, 3 * A_WIDTH), lambda i: (i, 0)),
            pl.BlockSpec((tm, gla_cols), lambda i: (i, 0)),
            pl.BlockSpec((tm, B_WIDTH), lambda i: (i, 0)),
            pl.BlockSpec((tm // CHUNK, B_KEY_WIDTH), lambda i: (i, 0)),
            pl.BlockSpec((None, B_WIDTH, tm), lambda i: (i, 0, 0)),
        ],
        out_shape=[
            jax.ShapeDtypeStruct((t, 3 * A_WIDTH), BF16),
            jax.ShapeDtypeStruct((t, gla_cols), BF16),
            jax.ShapeDtypeStruct((t, B_WIDTH), F32),
            jax.ShapeDtypeStruct((t // CHUNK, B_KEY_WIDTH), F32),
            jax.ShapeDtypeStruct((t // tm, B_WIDTH, tm), BF16),
        ],
        compiler_params=pltpu.CompilerParams(
            dimension_semantics=("parallel",), vmem_limit_bytes=VMEM_LIMIT),
        name="inproj",
    )(res, g, w_t, w_t, w_t, wg, bg)


def _attn_kernel(q_ref, k_ref, v_ref, bias_ref, o_ref, vt_scr, s_scr, m_scr, e_scr):
    seq = q_ref.shape[0]
    n_blk = seq // Q_BLOCK
    n_prev = PREV // Q_BLOCK
    lo = lax.broadcasted_iota(jnp.int32, (1, LANES), 1) < A_HEAD_DIM

    for r in range(n_blk):
        vt_scr[r, :LANES, :] = v_ref[r * LANES:(r + 1) * LANES, :].astype(F32).T.astype(BF16)
        vt_scr[r, LANES:, :] = jnp.ones((VT_ROWS - LANES, LANES), BF16)

    def window(t):
        return max(t - n_prev, 0), min(t + 1, N_WIN_TILES)

    def scores(t):
        kt, n_tiles = window(t)
        rows = n_tiles * LANES
        q2 = q_ref[t * Q_BLOCK:(t + 1) * Q_BLOCK, :].astype(F32) * (A_HEAD_DIM ** -0.5 * LOG2E)
        q4 = jnp.concatenate(
            [jnp.where(lo, q2, 0.0), jnp.where(lo, 0.0, q2)], axis=0).astype(BF16)
        kw = k_ref[kt * LANES:kt * LANES + rows, :]
        s = _dot_nt(kw, q4)
        m = None
        for i in range(n_tiles):
            tile = s[i * LANES:(i + 1) * LANES, :]
            w = N_WIN_TILES - n_tiles + i
            if w not in ZERO_BIAS_TILES:
                tile = tile + bias_ref[w * LANES:(w + 1) * LANES, :]
            s_scr[t % N_SLOTS, i * LANES:(i + 1) * LANES, :] = tile
            tile_max = jnp.max(tile.reshape(LANES // 8, 8, 2 * Q_BLOCK), axis=0)
            m = tile_max if m is None else jnp.maximum(m, tile_max)
        m_scr[t % N_SLOTS] = jnp.max(m, axis=0, keepdims=True)

    def exps(t):
        rows = window(t)[1] * LANES
        d = (s_scr[t % N_SLOTS, :rows, :] - m_scr[t % N_SLOTS]).astype(BF16)
        e_scr[t % N_SLOTS, :rows, :] = jnp.exp2(d)

    def weighted_sum(t):
        kt, n_tiles = window(t)
        vt = jnp.concatenate([vt_scr[kt + i] for i in range(n_tiles)], axis=1)
        ot = _dot(vt, e_scr[t % N_SLOTS, :n_tiles * LANES, :])
        inv = 1.0 / ot[LANES:LANES + 1, :]
        o_pair = jnp.concatenate(
            [ot[:A_HEAD_DIM, :Q_BLOCK] * inv[:, :Q_BLOCK],
             ot[A_HEAD_DIM:LANES, Q_BLOCK:] * inv[:, Q_BLOCK:]], axis=0)
        o_ref[t * Q_BLOCK:(t + 1) * Q_BLOCK, :] = o_pair.T.astype(BF16)

    for step in range(n_blk + 2 * STAGE_LAG):
        if step < n_blk:
            scores(step)
        if 0 <= step - STAGE_LAG < n_blk:
            exps(step - STAGE_LAG)
        if 0 <= step - 2 * STAGE_LAG < n_blk:
            weighted_sum(step - 2 * STAGE_LAG)


def _attention(qkva, bias, layer):
    bsz, seq, _ = qkva.shape
    n_pair = A_HEADS // 2
    blk = (None, seq, LANES)
    return pl.pallas_call(
        _attn_kernel,
        grid=(bsz, n_pair),
        in_specs=[
            pl.BlockSpec(blk, lambda b, p: (b, 0, p)),
            pl.BlockSpec(blk, lambda b, p: (b, 0, n_pair + p)),
            pl.BlockSpec(blk, lambda b, p: (b, 0, 2 * n_pair + p)),
            pl.BlockSpec((None, K_WINDOW, 2 * Q_BLOCK), lambda b, p: (layer * n_pair + p, 0, 0)),
        ],
        out_specs=pl.BlockSpec(blk, lambda b, p: (b, 0, p)),
        out_shape=jax.ShapeDtypeStruct((bsz, seq, A_WIDTH), BF16),
        scratch_shapes=[pltpu.VMEM((seq // LANES, VT_ROWS, LANES), BF16),
                        pltpu.VMEM((N_SLOTS, K_WINDOW, 2 * Q_BLOCK), F32),
                        pltpu.VMEM((N_SLOTS, 1, 2 * Q_BLOCK), F32),
                        pltpu.VMEM((N_SLOTS, K_WINDOW, 2 * Q_BLOCK), BF16)],
        compiler_params=pltpu.CompilerParams(
            dimension_semantics=("parallel", "parallel"), vmem_limit_bytes=VMEM_LIMIT),
        name="band_attention",
    )(qkva, qkva, qkva, bias)


def _chunk_cumsum(x):
    row = lax.broadcasted_iota(jnp.int32, x.shape, 0) % CHUNK
    d = 1
    while d < CHUNK:
        x = x + jnp.where(row >= d, pltpu.roll(x, d, axis=0), 0.0)
        d *= 2
    return x


def _gla_kernel(qe_ref, qo_ref, kt_ref, kend_ref, v_ref, vt_ref, gate_ref, decay_ref, g_ref,
                o_ref):
    seq = qe_ref.shape[0]
    n_pair = B_HEADS // 2
    lo = lax.broadcasted_iota(jnp.int32, (1, LANES), 1) < B_KEY_DIM
    ri = lax.broadcasted_iota(jnp.int32, (2 * CHUNK, CHUNK), 0)
    ci = lax.broadcasted_iota(jnp.int32, (2 * CHUNK, CHUNK), 1)
    causal = (ri % CHUNK) >= ci
    group = GLA_GROUP * CHUNK

    def body(i, states):
        rows = pl.ds(pl.multiple_of(i * group, group), group)
        q_even = qe_ref[rows, :]
        q_odd = qo_ref[rows, :]
        kt = kt_ref[rows, :]
        kend = kend_ref[rows, :]
        v = v_ref[rows, :]
        decay = decay_ref[pl.ds(pl.multiple_of(i * GLA_GROUP, GLA_GROUP), GLA_GROUP), :]
        vt = vt_ref[i]
        pad = jnp.zeros((CHUNK, LANES), BF16)
        states = list(states)
        outs = []
        for c in range(GLA_GROUP):
            sl = slice(c * CHUNK, (c + 1) * CHUNK)
            pair_cols = slice((c // 2) * LANES, (c // 2 + 1) * LANES)
            o_c = []
            for p in range(n_pair):
                kl = slice(p * LANES, (p + 1) * LANES)
                ve = slice(2 * p * B_VAL_DIM, (2 * p + 1) * B_VAL_DIM)
                vo = slice((2 * p + 1) * B_VAL_DIM, (2 * p + 2) * B_VAL_DIM)
                st = states[p]
                q4 = jnp.concatenate([q_even[sl, kl], q_odd[sl, kl]], axis=0)
                att = jnp.where(causal, _dot_nt(q4, kt[sl, kl]), 0.0).astype(BF16)
                inter = _dot_nt(q4, st.astype(BF16))
                o_c += [_dot(att[:CHUNK], v[sl, ve]) + inter[:CHUNK],
                        _dot(att[CHUNK:], v[sl, vo]) + inter[CHUNK:]]
                kend_c = [kend[sl, kl], pad] if c % 2 == 0 else [pad, kend[sl, kl]]
                d_full = _dot(vt[2 * p * B_VAL_DIM:(2 * p + 2) * B_VAL_DIM, pair_cols],
                              jnp.concatenate(kend_c, axis=0))
                d_st = jnp.where(lo, d_full[:B_VAL_DIM], d_full[B_VAL_DIM:])
                states[p] = st * decay[c:c + 1, kl] + d_st
            outs.append(jnp.concatenate(o_c, axis=1))
        o = jnp.concatenate(outs, axis=0)
        gate = gate_ref[rows, :]
        for h in range(B_HEADS):
            hl = slice(h * B_VAL_DIM, (h + 1) * B_VAL_DIM)
            oh = o[:, hl]
            oh = oh * lax.rsqrt(jnp.mean(oh * oh, axis=-1, keepdims=True) + EPS) * g_ref[:, hl]
            o_ref[rows, hl] = (oh * gate[:, hl]).astype(BF16)
        return tuple(states)

    zero = jnp.zeros((B_VAL_DIM, LANES), F32)
    lax.fori_loop(0, seq // group, body, (zero,) * n_pair)


def _gla(gq, vt, gate, decay, g):
    bsz, seq, _ = gq.shape
    kblk = (None, seq, B_KEY_WIDTH)
    vblk = (None, seq, B_WIDTH)
    return pl.pallas_call(
        _gla_kernel,
        grid=(bsz,),
        in_specs=[pl.BlockSpec(kblk, lambda b, j=j: (b, 0, j)) for j in range(4)] + [
            pl.BlockSpec(vblk, lambda b: (b, 0, 4 * B_KEY_WIDTH // B_WIDTH)),
            pl.BlockSpec((None,) + vt.shape[1:], lambda b: (b, 0, 0, 0)),
            pl.BlockSpec(vblk, lambda b: (b, 0, 0)),
            pl.BlockSpec((None, seq // CHUNK, B_KEY_WIDTH), lambda b: (b, 0, 0)),
            _const_spec((1, B_WIDTH)),
        ],
        out_specs=pl.BlockSpec(vblk, lambda b: (b, 0, 0)),
        out_shape=jax.ShapeDtypeStruct((bsz, seq, B_WIDTH), BF16),
        compiler_params=pltpu.CompilerParams(
            dimension_semantics=("parallel",), vmem_limit_bytes=VMEM_LIMIT),
        name="gla",
    )(gq, gq, gq, gq, gq, vt, gate, decay, g)


def _merge_kernel(res_ref, g_ref, ya_ref, yb_ref, wgate_ref, wbr_ref, wout_ref, o_ref):
    slab = res_ref.shape[0] // ROW_SLABS
    for r in range(ROW_SLABS):
        rows = slice(r * slab, (r + 1) * slab)
        res = res_ref[rows, :]
        h = _rmsnorm_bf16(res, g_ref[...])
        gates = _dot_nt(h, wgate_ref[...])
        u_a = _dot(ya_ref[rows, :], wbr_ref[0])
        u_b = _dot(yb_ref[rows, :], wbr_ref[1])
        merged = _sigmoid(gates[:, :D_MODEL]) * u_a + _sigmoid(gates[:, D_MODEL:]) * u_b
        o_ref[rows, :] = res + _dot(merged.astype(BF16), wout_ref[...])


def _merge(res, g, ya, yb, wgate, wbr, wout, tm, layer):
    t = res.shape[0]
    return pl.pallas_call(
        _merge_kernel,
        grid=(t // tm,),
        in_specs=[
            pl.BlockSpec((tm, D_MODEL), lambda i: (i, 0)),
            _const_spec((1, D_MODEL)),
            pl.BlockSpec((tm, A_WIDTH), lambda i: (i, 0)),
            pl.BlockSpec((tm, B_WIDTH), lambda i: (i, 0)),
            _layer_spec(wgate, layer),
            _layer_spec(wbr, layer),
            _layer_spec(wout, layer),
        ],
        out_specs=pl.BlockSpec((tm, D_MODEL), lambda i: (i, 0)),
        out_shape=jax.ShapeDtypeStruct((t, D_MODEL), F32),
        compiler_params=pltpu.CompilerParams(
            dimension_semantics=("parallel",), vmem_limit_bytes=VMEM_LIMIT),
        name="merge",
    )(res, g, ya, yb, wgate, wbr, wout)


def _mlp_kernel(res_ref, g_ref, wup_hbm, wdn_hbm, gf_ref, o_ref,
                wup_bf, wdn_bf, stage, sem, *, layer, final_norm):
    first_step = pl.program_id(0) == 0
    n_ff = D_FF // D_MODEL

    def chunk_copy(k):
        c = k // 2
        cols = pl.ds(c * D_MODEL, D_MODEL)
        src = wup_hbm.at[layer, :, cols] if k % 2 == 0 else wdn_hbm.at[layer, cols, :]
        return pltpu.make_async_copy(src, stage.at[k % 2], sem.at[k % 2])

    def land_chunk(k):
        c = k // 2
        cols = slice(c * D_MODEL, (c + 1) * D_MODEL)
        chunk_copy(k).wait()
        if k % 2 == 0:
            wup_bf[:, cols] = stage[k % 2].astype(BF16)
        else:
            wdn_bf[cols, :] = stage[k % 2].astype(BF16)
        if k + 2 < 2 * n_ff:
            chunk_copy(k + 2).start()

    @pl.when(first_step)
    def _():
        chunk_copy(0).start()
        chunk_copy(1).start()

    slab = res_ref.shape[0] // ROW_SLABS
    for r in range(ROW_SLABS):
        rows = slice(r * slab, (r + 1) * slab)
        res = res_ref[rows, :]
        h = _rmsnorm_bf16(res, g_ref[...])
        acc = res
        for c in range(n_ff):
            cols = slice(c * D_MODEL, (c + 1) * D_MODEL)
            if r == 0:
                pl.when(first_step)(functools.partial(land_chunk, 2 * c))
            up = jnp.maximum(_dot(h, wup_bf[:, cols]), 0.0)
            if r == 0:
                pl.when(first_step)(functools.partial(land_chunk, 2 * c + 1))
            acc = acc + _dot((up * up).astype(BF16), wdn_bf[cols, :])
        if final_norm:
            ms = jnp.mean(acc * acc, axis=-1, keepdims=True)
            acc = (acc * lax.rsqrt(ms + EPS)) * gf_ref[...]
        o_ref[rows, :] = acc


def _mlp(res, g, wup, wdn, gf, tm, layer, final_norm):
    t = res.shape[0]
    return pl.pallas_call(
        functools.partial(_mlp_kernel, layer=layer, final_norm=final_norm),
        grid=(t // tm,),
        in_specs=[
            pl.BlockSpec((tm, D_MODEL), lambda i: (i, 0)),
            _const_spec((1, D_MODEL)),
            pl.BlockSpec(memory_space=pl.ANY),
            pl.BlockSpec(memory_space=pl.ANY),
            _const_spec((1, D_MODEL)),
        ],
        out_specs=pl.BlockSpec((tm, D_MODEL), lambda i: (i, 0)),
        out_shape=jax.ShapeDtypeStruct((t, D_MODEL), F32),
        scratch_shapes=[pltpu.VMEM((D_MODEL, D_FF), BF16),
                        pltpu.VMEM((D_FF, D_MODEL), BF16),
                        pltpu.VMEM((2, D_MODEL, D_MODEL), F32),
                        pltpu.SemaphoreType.DMA((2,))],
        compiler_params=pltpu.CompilerParams(
            dimension_semantics=("arbitrary",), vmem_limit_bytes=VMEM_LIMIT),
        name="mlp",
    )(res, g, wup, wdn, gf)


def kernel(x, mix_norm_g, w_in, rel_bias, w_gate_lr, b_gate, gla_norm_g, w_branch, w_out,
           mlp_norm_g, w_up, w_down, final_norm_g):
    bsz, seq, d = x.shape
    t = bsz * seq
    tm = GLA_GROUP * CHUNK
    ab_cols = 3 * A_WIDTH + 2 * B_KEY_WIDTH + 2 * B_WIDTH
    w_in_t = jnp.swapaxes(w_in, 1, 2).astype(BF16)
    w_gates = w_in_t[:, ab_cols + GATE_RANK:]
    w_glr = w_gate_lr.astype(BF16)
    w_br = w_branch.astype(BF16)
    w_o = w_out.astype(BF16)
    bias = _bias_tables(rel_bias)

    res = x.reshape(t, d)
    for l in range(DEPTH):
        qkva, gq, gate, decay, vt = _inproj(res, mix_norm_g[l][None], w_in_t, w_glr,
                                            b_gate[l][None], tm, l)
        ya = _attention(qkva.reshape(bsz, seq, -1), bias, l)
        yb = _gla(gq.reshape(bsz, seq, -1), vt.reshape(bsz, seq // tm, B_WIDTH, tm),
                  gate.reshape(bsz, seq, -1), decay.reshape(bsz, seq // CHUNK, -1),
                  gla_norm_g[l][None])
        res = _merge(res, mix_norm_g[l][None], ya.reshape(t, -1), yb.reshape(t, -1),
                     w_gates, w_br, w_o, DENSE_TM, l)
        res = _mlp(res, mlp_norm_g[l][None], w_up, w_down, final_norm_g[None], DENSE_TM, l,
                   final_norm=(l == DEPTH - 1))
    return res.reshape(bsz, seq, d)
```

```python
import functools

import jax
import jax.numpy as jnp
from jax import lax
from jax.experimental import pallas as pl
from jax.experimental.pallas import tpu as pltpu

D_MODEL = 1024
DEPTH = 4
CHUNK = 64
EPS = 1e-6

A_HEADS = 8
A_HEAD_DIM = 64
A_WIDTH = A_HEADS * A_HEAD_DIM
N_PREV_CHUNKS = 8
PREV = N_PREV_CHUNKS * CHUNK
REL_CLIP = 128
N_REL = 2 * REL_CLIP + 1

B_HEADS = 4
B_KEY_DIM = 64
B_VAL_DIM = 128
B_KEY_WIDTH = B_HEADS * B_KEY_DIM
B_WIDTH = B_HEADS * B_VAL_DIM
GATE_RANK = 16
GATE_TAU = 16.0
D_FF = 4 * D_MODEL

LANES = 128
Q_BLOCK = 2 * CHUNK
K_WINDOW = PREV + Q_BLOCK
N_WIN_TILES = K_WINDOW // 128
VT_ROWS = 128 + 16
STAGE_LAG = 2
N_SLOTS = STAGE_LAG + 1
ZERO_BIAS_TILES = tuple(range(1, (PREV - REL_CLIP) // 128))
LOG2E = 1.4426950408889634
ROW_SLABS = 2
GLA_GROUP = 16
DENSE_TM = 1024
NEAR_START = PREV - REL_CLIP
NEAR_SEG = K_WINDOW - NEAR_START + Q_BLOCK
VMEM_LIMIT = 56 * 1024 * 1024

F32 = jnp.float32
BF16 = jnp.bfloat16
NEG = -1e30


def _rmsnorm_bf16(x, g):
    ms = jnp.mean(x * x, axis=-1, keepdims=True)
    return ((x * lax.rsqrt(ms + EPS)) * g).astype(BF16)


def _sigmoid(x):
    return 1.0 / (1.0 + jnp.exp(-x))


def _dot(a, b):
    return jnp.dot(a, b, preferred_element_type=F32)


def _dot_nt(a, b):
    return lax.dot_general(a, b, (((1,), (1,)), ((), ())), preferred_element_type=F32)


def _const_spec(shape):
    nd = len(shape)
    return pl.BlockSpec(shape, lambda *_: (0,) * nd)


def _layer_spec(stacked, layer):
    rest = stacked.shape[1:]
    return pl.BlockSpec((None,) + rest, lambda *_: (layer,) + (0,) * len(rest),
                        pipeline_mode=pl.Buffered(1))


def _bias_kernel(t_ref, o_ref):
    t = t_ref[...]
    t = (t - t[:, 0:1]) * LOG2E
    x = jnp.broadcast_to(t, (Q_BLOCK, NEAR_SEG))
    row = lax.broadcasted_iota(jnp.int32, (Q_BLOCK, NEAR_SEG), 0)
    shift = 1
    while shift < Q_BLOCK:
        x = jnp.where((row & shift) != 0, pltpu.roll(x, shift, axis=1), x)
        shift *= 2
    near = x[:, Q_BLOCK:]
    r = lax.broadcasted_iota(jnp.int32, near.shape, 0)
    c = lax.broadcasted_iota(jnp.int32, near.shape, 1)
    near = jnp.where((r < CHUNK) & (c >= near.shape[1] - CHUNK), NEG, near)
    for j in range(near.shape[1] // LANES):
        o_ref[NEAR_START + j * LANES:NEAR_START + (j + 1) * LANES, :] = (
            near[:, j * LANES:(j + 1) * LANES].T)
    kr = lax.broadcasted_iota(jnp.int32, (NEAR_START, Q_BLOCK), 0)
    qc = lax.broadcasted_iota(jnp.int32, (NEAR_START, Q_BLOCK), 1)
    o_ref[:NEAR_START, :] = jnp.where((kr < CHUNK) & (qc >= CHUNK), NEG, 0.0)


def _bias_tables(rel_bias):
    far = rel_bias[..., N_REL - 1:]
    n_far = NEAR_SEG - (N_REL - 2)
    seg = jnp.concatenate([
        jnp.broadcast_to(far, rel_bias.shape[:-1] + (n_far,)),
        rel_bias[..., N_REL - 2:0:-1],
    ], axis=-1)
    n = DEPTH * A_HEADS
    seg = seg.reshape(n, 1, NEAR_SEG)
    return pl.pallas_call(
        _bias_kernel,
        grid=(n,),
        in_specs=[pl.BlockSpec((None, 1, NEAR_SEG), lambda i: (i, 0, 0))],
        out_specs=pl.BlockSpec((None, K_WINDOW, Q_BLOCK), lambda i: (i // 2, 0, i % 2)),
        out_shape=jax.ShapeDtypeStruct((n // 2, K_WINDOW, 2 * Q_BLOCK), F32),
        name="bias_table",
    )(seg)


def _inproj_kernel(res_ref, g_ref, wa_ref, wb_ref, wlr_ref, wg_ref, bg_ref,
                   qkva_ref, gla_ref, gate_ref, decay_ref, vt_ref):
    tm = res_ref.shape[0]
    n_chunk = tm // CHUNK
    kw = B_KEY_WIDTH
    even = (lax.broadcasted_iota(jnp.int32, (1, kw), 1) // B_KEY_DIM) % 2 == 0
    h = _rmsnorm_bf16(res_ref[...], g_ref[...])
    lr = _dot_nt(h, wlr_ref[...])
    z = _dot(lr.astype(BF16), wg_ref[...]) + bg_ref[...]
    log_sig = jnp.minimum(z, 0.0) - jnp.log(1.0 + jnp.exp(-jnp.abs(z)))
    b = _chunk_cumsum(log_sig / GATE_TAU)
    b3 = b.reshape(n_chunk, CHUNK, kw)
    b_last = b3[:, CHUNK - 1:CHUNK, :]
    decay_ref[...] = jnp.exp(b_last).reshape(n_chunk, kw)
    to_end = jnp.exp(b_last - b3).reshape(tm, kw)
    pb = _dot_nt(h, wb_ref[...])
    qt = pb[:, :kw] * (B_KEY_DIM ** -0.5) * jnp.exp(b)
    k = pb[:, kw:2 * kw]
    gla_ref[:, :kw] = jnp.where(even, qt, 0.0).astype(BF16)
    gla_ref[:, kw:2 * kw] = jnp.where(even, 0.0, qt).astype(BF16)
    gla_ref[:, 2 * kw:3 * kw] = (k * jnp.exp(-b)).astype(BF16)
    gla_ref[:, 3 * kw:4 * kw] = (k * to_end).astype(BF16)
    v = pb[:, 2 * kw:2 * kw + B_WIDTH]
    gla_ref[:, 4 * kw:] = v.astype(BF16)
    vt_ref[...] = v.T.astype(BF16)
    r = pb[:, 2 * kw + B_WIDTH:]
    gate_ref[...] = r * _sigmoid(r)
    qkva_ref[...] = _dot_nt(h, wa_ref[...]).astype(BF16)


def _inproj(res, g, w_t, wg, bg, tm, layer):
    t = res.shape[0]
    gla_cols = 4 * B_KEY_WIDTH + B_WIDTH
    half_rows = 3 * A_WIDTH
    assert half_rows == 2 * B_KEY_WIDTH + 2 * B_WIDTH
    half = (None, half_rows, D_MODEL)
    lr_block = 2 * half_rows // GATE_RANK
    return pl.pallas_call(
        _inproj_kernel,
        grid=(t // tm,),
        in_specs=[
            pl.BlockSpec((tm, D_MODEL), lambda i: (i, 0)),
            _const_spec((1, D_MODEL)),
            pl.BlockSpec(half, lambda i: (layer, 0, 0)),
            pl.BlockSpec(half, lambda i: (layer, 1, 0)),
            pl.BlockSpec((None, GATE_RANK, D_MODEL), lambda i: (layer, lr_block, 0)),
            _layer_spec(wg, layer),
            _const_spec((1, B_KEY_WIDTH)),
        ],
        out_specs=[
            pl.BlockSpec((tm, 3 * A_WIDTH), lambda i: (i, 0)),
            pl.BlockSpec((tm, gla_cols), lambda i: (i, 0)),
            pl.BlockSpec((tm, B_WIDTH), lambda i: (i, 0)),
            pl.BlockSpec((tm // CHUNK, B_KEY_WIDTH), lambda i: (i, 0)),
            pl.BlockSpec((None, B_WIDTH, tm), lambda i: (i, 0, 0)),
        ],
        out_shape=[
            jax.ShapeDtypeStruct((t, 3 * A_WIDTH), BF16),
            jax.ShapeDtypeStruct((t, gla_cols), BF16),
            jax.ShapeDtypeStruct((t, B_WIDTH), F32),
            jax.ShapeDtypeStruct((t // CHUNK, B_KEY_WIDTH), F32),
            jax.ShapeDtypeStruct((t // tm, B_WIDTH, tm), BF16),
        ],
        compiler_params=pltpu.CompilerParams(
            dimension_semantics=("parallel",), vmem_limit_bytes=VMEM_LIMIT),
        name="inproj",
    )(res, g, w_t, w_t, w_t, wg, bg)


def _attn_kernel(q_ref, k_ref, v_ref, bias_ref, o_ref, vt_scr, s_scr, m_scr, e_scr):
    seq = q_ref.shape[0]
    n_blk = seq // Q_BLOCK
    n_prev = PREV // Q_BLOCK
    lo = lax.broadcasted_iota(jnp.int32, (1, LANES), 1) < A_HEAD_DIM

    for r in range(n_blk):
        vt_scr[r, :LANES, :] = v_ref[r * LANES:(r + 1) * LANES, :].astype(F32).T.astype(BF16)
        vt_scr[r, LANES:, :] = jnp.ones((VT_ROWS - LANES, LANES), BF16)

    def window(t):
        return max(t - n_prev, 0), min(t + 1, N_WIN_TILES)

    def scores(t):
        kt, n_tiles = window(t)
        rows = n_tiles * LANES
        q2 = q_ref[t * Q_BLOCK:(t + 1) * Q_BLOCK, :].astype(F32) * (A_HEAD_DIM ** -0.5 * LOG2E)
        q4 = jnp.concatenate(
            [jnp.where(lo, q2, 0.0), jnp.where(lo, 0.0, q2)], axis=0).astype(BF16)
        kw = k_ref[kt * LANES:kt * LANES + rows, :]
        s = _dot_nt(kw, q4)
        m = None
        for i in range(n_tiles):
            tile = s[i * LANES:(i + 1) * LANES, :]
            w = N_WIN_TILES - n_tiles + i
            if w not in ZERO_BIAS_TILES:
                tile = tile + bias_ref[w * LANES:(w + 1) * LANES, :]
            s_scr[t % N_SLOTS, i * LANES:(i + 1) * LANES, :] = tile
            tile_max = jnp.max(tile.reshape(LANES // 8, 8, 2 * Q_BLOCK), axis=0)
            m = tile_max if m is None else jnp.maximum(m, tile_max)
        m_scr[t % N_SLOTS] = jnp.max(m, axis=0, keepdims=True)

    def exps(t):
        rows = window(t)[1] * LANES
        d = (s_scr[t % N_SLOTS, :rows, :] - m_scr[t % N_SLOTS]).astype(BF16)
        e_scr[t % N_SLOTS, :rows, :] = jnp.exp2(d)

    def weighted_sum(t):
        kt, n_tiles = window(t)
        vt = jnp.concatenate([vt_scr[kt + i] for i in range(n_tiles)], axis=1)
        ot = _dot(vt, e_scr[t % N_SLOTS, :n_tiles * LANES, :])
        inv = 1.0 / ot[LANES:LANES + 1, :]
        o_pair = jnp.concatenate(
            [ot[:A_HEAD_DIM, :Q_BLOCK] * inv[:, :Q_BLOCK],
             ot[A_HEAD_DIM:LANES, Q_BLOCK:] * inv[:, Q_BLOCK:]], axis=0)
        o_ref[t * Q_BLOCK:(t + 1) * Q_BLOCK, :] = o_pair.T.astype(BF16)

    for step in range(n_blk + 2 * STAGE_LAG):
        if step < n_blk:
            scores(step)
        if 0 <= step - STAGE_LAG < n_blk:
            exps(step - STAGE_LAG)
        if 0 <= step - 2 * STAGE_LAG < n_blk:
            weighted_sum(step - 2 * STAGE_LAG)


def _attention(qkva, bias, layer):
    bsz, seq, _ = qkva.shape
    n_pair = A_HEADS // 2
    blk = (None, seq, LANES)
    return pl.pallas_call(
        _attn_kernel,
        grid=(bsz, n_pair),
        in_specs=[
            pl.BlockSpec(blk, lambda b, p: (b, 0, p)),
            pl.BlockSpec(blk, lambda b, p: (b, 0, n_pair + p)),
            pl.BlockSpec(blk, lambda b, p: (b, 0, 2 * n_pair + p)),
            pl.BlockSpec((None, K_WINDOW, 2 * Q_BLOCK), lambda b, p: (layer * n_pair + p, 0, 0)),
        ],
        out_specs=pl.BlockSpec(blk, lambda b, p: (b, 0, p)),
        out_shape=jax.ShapeDtypeStruct((bsz, seq, A_WIDTH), BF16),
        scratch_shapes=[pltpu.VMEM((seq // LANES, VT_ROWS, LANES), BF16),
                        pltpu.VMEM((N_SLOTS, K_WINDOW, 2 * Q_BLOCK), F32),
                        pltpu.VMEM((N_SLOTS, 1, 2 * Q_BLOCK), F32),
                        pltpu.VMEM((N_SLOTS, K_WINDOW, 2 * Q_BLOCK), BF16)],
        compiler_params=pltpu.CompilerParams(
            dimension_semantics=("parallel", "parallel"), vmem_limit_bytes=VMEM_LIMIT),
        name="band_attention",
    )(qkva, qkva, qkva, bias)


def _chunk_cumsum(x):
    row = lax.broadcasted_iota(jnp.int32, x.shape, 0) % CHUNK
    d = 1
    while d < CHUNK:
        x = x + jnp.where(row >= d, pltpu.roll(x, d, axis=0), 0.0)
        d *= 2
    return x


def _gla_kernel(qe_ref, qo_ref, kt_ref, kend_ref, v_ref, vt_ref, gate_ref, decay_ref, g_ref,
                o_ref):
    seq = qe_ref.shape[0]
    n_pair = B_HEADS // 2
    lo = lax.broadcasted_iota(jnp.int32, (1, LANES), 1) < B_KEY_DIM
    ri = lax.broadcasted_iota(jnp.int32, (2 * CHUNK, CHUNK), 0)
    ci = lax.broadcasted_iota(jnp.int32, (2 * CHUNK, CHUNK), 1)
    causal = (ri % CHUNK) >= ci
    group = GLA_GROUP * CHUNK

    def body(i, states):
        rows = pl.ds(pl.multiple_of(i * group, group), group)
        q_even = qe_ref[rows, :]
        q_odd = qo_ref[rows, :]
        kt = kt_ref[rows, :]
        kend = kend_ref[rows, :]
        v = v_ref[rows, :]
        decay = decay_ref[pl.ds(pl.multiple_of(i * GLA_GROUP, GLA_GROUP), GLA_GROUP), :]
        vt = vt_ref[i]
        pad = jnp.zeros((CHUNK, LANES), BF16)
        states = list(states)
        outs = []
        for c in range(GLA_GROUP):
            sl = slice(c * CHUNK, (c + 1) * CHUNK)
            pair_cols = slice((c // 2) * LANES, (c // 2 + 1) * LANES)
            o_c = []
            for p in range(n_pair):
                kl = slice(p * LANES, (p + 1) * LANES)
                ve = slice(2 * p * B_VAL_DIM, (2 * p + 1) * B_VAL_DIM)
                vo = slice((2 * p + 1) * B_VAL_DIM, (2 * p + 2) * B_VAL_DIM)
                st = states[p]
                q4 = jnp.concatenate([q_even[sl, kl], q_odd[sl, kl]], axis=0)
                att = jnp.where(causal, _dot_nt(q4, kt[sl, kl]), 0.0).astype(BF16)
                inter = _dot_nt(q4, st.astype(BF16))
                o_c += [_dot(att[:CHUNK], v[sl, ve]) + inter[:CHUNK],
                        _dot(att[CHUNK:], v[sl, vo]) + inter[CHUNK:]]
                kend_c = [kend[sl, kl], pad] if c % 2 == 0 else [pad, kend[sl, kl]]
                d_full = _dot(vt[2 * p * B_VAL_DIM:(2 * p + 2) * B_VAL_DIM, pair_cols],
                              jnp.concatenate(kend_c, axis=0))
                d_st = jnp.where(lo, d_full[:B_VAL_DIM], d_full[B_VAL_DIM:])
                states[p] = st * decay[c:c + 1, kl] + d_st
            outs.append(jnp.concatenate(o_c, axis=1))
        o = jnp.concatenate(outs, axis=0)
        gate = gate_ref[rows, :]
        for h in range(B_HEADS):
            hl = slice(h * B_VAL_DIM, (h + 1) * B_VAL_DIM)
            oh = o[:, hl]
            oh = oh * lax.rsqrt(jnp.mean(oh * oh, axis=-1, keepdims=True) + EPS) * g_ref[:, hl]
            o_ref[rows, hl] = (oh * gate[:, hl]).astype(BF16)
        return tuple(states)

    zero = jnp.zeros((B_VAL_DIM, LANES), F32)
    lax.fori_loop(0, seq // group, body, (zero,) * n_pair)


def _gla(gq, vt, gate, decay, g):
    bsz, seq, _ = gq.shape
    kblk = (None, seq, B_KEY_WIDTH)
    vblk = (None, seq, B_WIDTH)
    return pl.pallas_call(
        _gla_kernel,
        grid=(bsz,),
        in_specs=[pl.BlockSpec(kblk, lambda b, j=j: (b, 0, j)) for j in range(4)] + [
            pl.BlockSpec(vblk, lambda b: (b, 0, 4 * B_KEY_WIDTH // B_WIDTH)),
            pl.BlockSpec((None,) + vt.shape[1:], lambda b: (b, 0, 0, 0)),
            pl.BlockSpec(vblk, lambda b: (b, 0, 0)),
            pl.BlockSpec((None, seq // CHUNK, B_KEY_WIDTH), lambda b: (b, 0, 0)),
            _const_spec((1, B_WIDTH)),
        ],
        out_specs=pl.BlockSpec(vblk, lambda b: (b, 0, 0)),
        out_shape=jax.ShapeDtypeStruct((bsz, seq, B_WIDTH), BF16),
        compiler_params=pltpu.CompilerParams(
            dimension_semantics=("parallel",), vmem_limit_bytes=VMEM_LIMIT),
        name="gla",
    )(gq, gq, gq, gq, gq, vt, gate, decay, g)


def _merge_kernel(res_ref, g_ref, ya_ref, yb_ref, wgate_ref, wbr_ref, wout_ref, o_ref):
    slab = res_ref.shape[0] // ROW_SLABS
    for r in range(ROW_SLABS):
        rows = slice(r * slab, (r + 1) * slab)
        res = res_ref[rows, :]
        h = _rmsnorm_bf16(res, g_ref[...])
        gates = _dot_nt(h, wgate_ref[...])
        u_a = _dot(ya_ref[rows, :], wbr_ref[0])
        u_b = _dot(yb_ref[rows, :], wbr_ref[1])
        merged = _sigmoid(gates[:, :D_MODEL]) * u_a + _sigmoid(gates[:, D_MODEL:]) * u_b
        o_ref[rows, :] = res + _dot(merged.astype(BF16), wout_ref[...])


def _merge(res, g, ya, yb, wgate, wbr, wout, tm, layer):
    t = res.shape[0]
    return pl.pallas_call(
        _merge_kernel,
        grid=(t // tm,),
        in_specs=[
            pl.BlockSpec((tm, D_MODEL), lambda i: (i, 0)),
            _const_spec((1, D_MODEL)),
            pl.BlockSpec((tm, A_WIDTH), lambda i: (i, 0)),
            pl.BlockSpec((tm, B_WIDTH), lambda i: (i, 0)),
            _layer_spec(wgate, layer),
            _layer_spec(wbr, layer),
            _layer_spec(wout, layer),
        ],
        out_specs=pl.BlockSpec((tm, D_MODEL), lambda i: (i, 0)),
        out_shape=jax.ShapeDtypeStruct((t, D_MODEL), F32),
        compiler_params=pltpu.CompilerParams(
            dimension_semantics=("parallel",), vmem_limit_bytes=VMEM_LIMIT),
        name="merge",
    )(res, g, ya, yb, wgate, wbr, wout)


def _mlp_kernel(res_ref, g_ref, wup_hbm, wdn_hbm, gf_ref, o_ref,
                wup_bf, wdn_bf, stage, sem, *, layer, final_norm):
    first_step = pl.program_id(0) == 0
    n_ff = D_FF // D_MODEL

    def chunk_copy(k):
        c = k // 2
        cols = pl.ds(c * D_MODEL, D_MODEL)
        src = wup_hbm.at[layer, :, cols] if k % 2 == 0 else wdn_hbm.at[layer, cols, :]
        return pltpu.make_async_copy(src, stage.at[k % 2], sem.at[k % 2])

    def land_chunk(k):
        c = k // 2
        cols = slice(c * D_MODEL, (c + 1) * D_MODEL)
        chunk_copy(k).wait()
        if k % 2 == 0:
            wup_bf[:, cols] = stage[k % 2].astype(BF16)
        else:
            wdn_bf[cols, :] = stage[k % 2].astype(BF16)
        if k + 2 < 2 * n_ff:
            chunk_copy(k + 2).start()

    slab = res_ref.shape[0] // ROW_SLABS

    def tile(land_weights):
        for r in range(ROW_SLABS):
            rows = slice(r * slab, (r + 1) * slab)
            res = res_ref[rows, :]
            h = _rmsnorm_bf16(res, g_ref[...])
            acc = res
            for c in range(n_ff):
                cols = slice(c * D_MODEL, (c + 1) * D_MODEL)
                if land_weights and r == 0:
                    land_chunk(2 * c)
                up = jnp.maximum(_dot(h, wup_bf[:, cols]), 0.0)
                if land_weights and r == 0:
                    land_chunk(2 * c + 1)
                acc = acc + _dot((up * up).astype(BF16), wdn_bf[cols, :])
            if final_norm:
                ms = jnp.mean(acc * acc, axis=-1, keepdims=True)
                acc = (acc * lax.rsqrt(ms + EPS)) * gf_ref[...]
            o_ref[rows, :] = acc

    @pl.when(first_step)
    def _():
        chunk_copy(0).start()
        chunk_copy(1).start()
        tile(True)

    @pl.when(jnp.logical_not(first_step))
    def _():
        tile(False)


def _mlp(res, g, wup, wdn, gf, tm, layer, final_norm):
    t = res.shape[0]
    return pl.pallas_call(
        functools.partial(_mlp_kernel, layer=layer, final_norm=final_norm),
        grid=(t // tm,),
        in_specs=[
            pl.BlockSpec((tm, D_MODEL), lambda i: (i, 0)),
            _const_spec((1, D_MODEL)),
            pl.BlockSpec(memory_space=pl.ANY),
            pl.BlockSpec(memory_space=pl.ANY),
            _const_spec((1, D_MODEL)),
        ],
        out_specs=pl.BlockSpec((tm, D_MODEL), lambda i: (i, 0)),
        out_shape=jax.ShapeDtypeStruct((t, D_MODEL), F32),
        scratch_shapes=[pltpu.VMEM((D_MODEL, D_FF), BF16),
                        pltpu.VMEM((D_FF, D_MODEL), BF16),
                        pltpu.VMEM((2, D_MODEL, D_MODEL), F32),
                        pltpu.SemaphoreType.DMA((2,))],
        compiler_params=pltpu.CompilerParams(
            dimension_semantics=("arbitrary",), vmem_limit_bytes=VMEM_LIMIT),
        name="mlp",
    )(res, g, wup, wdn, gf)


def kernel(x, mix_norm_g, w_in, rel_bias, w_gate_lr, b_gate, gla_norm_g, w_branch, w_out,
           mlp_norm_g, w_up, w_down, final_norm_g):
    bsz, seq, d = x.shape
    t = bsz * seq
    tm = GLA_GROUP * CHUNK
    ab_cols = 3 * A_WIDTH + 2 * B_KEY_WIDTH + 2 * B_WIDTH
    w_in_t = jnp.swapaxes(w_in, 1, 2).astype(BF16)
    w_gates = w_in_t[:, ab_cols + GATE_RANK:]
    w_glr = w_gate_lr.astype(BF16)
    w_br = w_branch.astype(BF16)
    w_o = w_out.astype(BF16)
    bias = _bias_tables(rel_bias)

    res = x.reshape(t, d)
    for l in range(DEPTH):
        qkva, gq, gate, decay, vt = _inproj(res, mix_norm_g[l][None], w_in_t, w_glr,
                                            b_gate[l][None], tm, l)
        ya = _attention(qkva.reshape(bsz, seq, -1), bias, l)
        yb = _gla(gq.reshape(bsz, seq, -1), vt.reshape(bsz, seq // tm, B_WIDTH, tm),
                  gate.reshape(bsz, seq, -1), decay.reshape(bsz, seq // CHUNK, -1),
                  gla_norm_g[l][None])
        res = _merge(res, mix_norm_g[l][None], ya.reshape(t, -1), yb.reshape(t, -1),
                     w_gates, w_br, w_o, DENSE_TM, l)
        res = _mlp(res, mlp_norm_g[l][None], w_up, w_down, final_norm_g[None], DENSE_TM, l,
                   final_norm=(l == DEPTH - 1))
    return res.reshape(bsz, seq, d)
```

```python
import functools

import jax
import jax.numpy as jnp
from jax import lax
from jax.experimental import pallas as pl
from jax.experimental.pallas import tpu as pltpu

D_MODEL = 1024
DEPTH = 4
CHUNK = 64
EPS = 1e-6

A_HEADS = 8
A_HEAD_DIM = 64
A_WIDTH = A_HEADS * A_HEAD_DIM
N_PREV_CHUNKS = 8
PREV = N_PREV_CHUNKS * CHUNK
REL_CLIP = 128
N_REL = 2 * REL_CLIP + 1

B_HEADS = 4
B_KEY_DIM = 64
B_VAL_DIM = 128
B_KEY_WIDTH = B_HEADS * B_KEY_DIM
B_WIDTH = B_HEADS * B_VAL_DIM
GATE_RANK = 16
GATE_TAU = 16.0
D_FF = 4 * D_MODEL

LANES = 128
Q_BLOCK = 2 * CHUNK
K_WINDOW = PREV + Q_BLOCK
N_WIN_TILES = K_WINDOW // 128
VT_ROWS = 128 + 16
STAGE_LAG = 2
N_SLOTS = STAGE_LAG + 1
ZERO_BIAS_TILES = tuple(range(1, (PREV - REL_CLIP) // 128))
LOG2E = 1.4426950408889634
ROW_SLABS = 2
GLA_GROUP = 16
DENSE_TM = 1024
NEAR_START = PREV - REL_CLIP
NEAR_SEG = K_WINDOW - NEAR_START + Q_BLOCK
VMEM_LIMIT = 56 * 1024 * 1024

F32 = jnp.float32
BF16 = jnp.bfloat16
NEG = -1e30


def _rmsnorm_bf16(x, g):
    ms = jnp.mean(x * x, axis=-1, keepdims=True)
    return ((x * lax.rsqrt(ms + EPS)) * g).astype(BF16)


def _sigmoid(x):
    return 1.0 / (1.0 + jnp.exp(-x))


def _dot(a, b):
    return jnp.dot(a, b, preferred_element_type=F32)


def _dot_nt(a, b):
    return lax.dot_general(a, b, (((1,), (1,)), ((), ())), preferred_element_type=F32)


def _const_spec(shape):
    nd = len(shape)
    return pl.BlockSpec(shape, lambda *_: (0,) * nd)


def _layer_spec(stacked, layer):
    rest = stacked.shape[1:]
    return pl.BlockSpec((None,) + rest, lambda *_: (layer,) + (0,) * len(rest),
                        pipeline_mode=pl.Buffered(1))


def _bias_kernel(t_ref, o_ref):
    t = t_ref[...]
    t = (t - t[:, 0:1]) * LOG2E
    x = jnp.broadcast_to(t, (Q_BLOCK, NEAR_SEG))
    x = pltpu.roll(x, 0, 1, stride=1, stride_axis=0)
    near = x[:, Q_BLOCK:]
    r = lax.broadcasted_iota(jnp.int32, near.shape, 0)
    c = lax.broadcasted_iota(jnp.int32, near.shape, 1)
    near = jnp.where((r < CHUNK) & (c >= near.shape[1] - CHUNK), NEG, near)
    for j in range(near.shape[1] // LANES):
        o_ref[NEAR_START + j * LANES:NEAR_START + (j + 1) * LANES, :] = (
            near[:, j * LANES:(j + 1) * LANES].T)
    kr = lax.broadcasted_iota(jnp.int32, (NEAR_START, Q_BLOCK), 0)
    qc = lax.broadcasted_iota(jnp.int32, (NEAR_START, Q_BLOCK), 1)
    o_ref[:NEAR_START, :] = jnp.where((kr < CHUNK) & (qc >= CHUNK), NEG, 0.0)


def _bias_tables(rel_bias):
    far = rel_bias[..., N_REL - 1:]
    n_far = NEAR_SEG - (N_REL - 2)
    seg = jnp.concatenate([
        jnp.broadcast_to(far, rel_bias.shape[:-1] + (n_far,)),
        rel_bias[..., N_REL - 2:0:-1],
    ], axis=-1)
    n = DEPTH * A_HEADS
    seg = seg.reshape(n, 1, NEAR_SEG)
    return pl.pallas_call(
        _bias_kernel,
        grid=(n,),
        in_specs=[pl.BlockSpec((None, 1, NEAR_SEG), lambda i: (i, 0, 0))],
        out_specs=pl.BlockSpec((None, K_WINDOW, Q_BLOCK), lambda i: (i // 2, 0, i % 2)),
        out_shape=jax.ShapeDtypeStruct((n // 2, K_WINDOW, 2 * Q_BLOCK), F32),
        name="bias_table",
    )(seg)


def _inproj_kernel(res_ref, g_ref, wa_ref, wb_ref, wlr_ref, wg_ref, bg_ref,
                   qkva_ref, gla_ref, gate_ref, decay_ref, vt_ref):
    tm = res_ref.shape[0]
    n_chunk = tm // CHUNK
    kw = B_KEY_WIDTH
    even = (lax.broadcasted_iota(jnp.int32, (1, kw), 1) // B_KEY_DIM) % 2 == 0
    h = _rmsnorm_bf16(res_ref[...], g_ref[...])
    lr = _dot_nt(h, wlr_ref[...])
    z = _dot(lr.astype(BF16), wg_ref[...]) + bg_ref[...]
    log_sig = jnp.minimum(z, 0.0) - jnp.log(1.0 + jnp.exp(-jnp.abs(z)))
    b = _chunk_cumsum(log_sig / GATE_TAU)
    b3 = b.reshape(n_chunk, CHUNK, kw)
    b_last = b3[:, CHUNK - 1:CHUNK, :]
    decay_ref[...] = jnp.exp(b_last).reshape(n_chunk, kw)
    to_end = jnp.exp(b_last - b3).reshape(tm, kw)
    pb = _dot_nt(h, wb_ref[...])
    qt = pb[:, :kw] * (B_KEY_DIM ** -0.5) * jnp.exp(b)
    k = pb[:, kw:2 * kw]
    gla_ref[:, :kw] = jnp.where(even, qt, 0.0).astype(BF16)
    gla_ref[:, kw:2 * kw] = jnp.where(even, 0.0, qt).astype(BF16)
    gla_ref[:, 2 * kw:3 * kw] = (k * jnp.exp(-b)).astype(BF16)
    gla_ref[:, 3 * kw:4 * kw] = (k * to_end).astype(BF16)
    v = pb[:, 2 * kw:2 * kw + B_WIDTH]
    gla_ref[:, 4 * kw:] = v.astype(BF16)
    vt_ref[...] = v.T.astype(BF16)
    r = pb[:, 2 * kw + B_WIDTH:]
    gate_ref[...] = r * _sigmoid(r)
    qkva_ref[...] = _dot_nt(h, wa_ref[...]).astype(BF16)


def _inproj(res, g, w_t, wg, bg, tm, layer):
    t = res.shape[0]
    gla_cols = 4 * B_KEY_WIDTH + B_WIDTH
    half_rows = 3 * A_WIDTH
    assert half_rows == 2 * B_KEY_WIDTH + 2 * B_WIDTH
    half = (None, half_rows, D_MODEL)
    lr_block = 2 * half_rows // GATE_RANK
    return pl.pallas_call(
        _inproj_kernel,
        grid=(t // tm,),
        in_specs=[
            pl.BlockSpec((tm, D_MODEL), lambda i: (i, 0)),
            _const_spec((1, D_MODEL)),
            pl.BlockSpec(half, lambda i: (layer, 0, 0)),
            pl.BlockSpec(half, lambda i: (layer, 1, 0)),
            pl.BlockSpec((None, GATE_RANK, D_MODEL), lambda i: (layer, lr_block, 0)),
            _layer_spec(wg, layer),
            _const_spec((1, B_KEY_WIDTH)),
        ],
        out_specs=[
            pl.BlockSpec((tm, 3 * A_WIDTH), lambda i: (i, 0)),
            pl.BlockSpec((tm, gla_cols), lambda i: (i, 0)),
            pl.BlockSpec((tm, B_WIDTH), lambda i: (i, 0)),
            pl.BlockSpec((tm // CHUNK, B_KEY_WIDTH), lambda i: (i, 0)),
            pl.BlockSpec((None, B_WIDTH, tm), lambda i: (i, 0, 0)),
        ],
        out_shape=[
            jax.ShapeDtypeStruct((t, 3 * A_WIDTH), BF16),
            jax.ShapeDtypeStruct((t, gla_cols), BF16),
            jax.ShapeDtypeStruct((t, B_WIDTH), F32),
            jax.ShapeDtypeStruct((t // CHUNK, B_KEY_WIDTH), F32),
            jax.ShapeDtypeStruct((t // tm, B_WIDTH, tm), BF16),
        ],
        compiler_params=pltpu.CompilerParams(
            dimension_semantics=("parallel",), vmem_limit_bytes=VMEM_LIMIT),
        name="inproj",
    )(res, g, w_t, w_t, w_t, wg, bg)


def _attn_kernel(q_ref, k_ref, v_ref, bias_ref, o_ref, vt_scr, s_scr, m_scr, e_scr):
    seq = q_ref.shape[0]
    n_blk = seq // Q_BLOCK
    n_prev = PREV // Q_BLOCK
    lo = lax.broadcasted_iota(jnp.int32, (1, LANES), 1) < A_HEAD_DIM

    for r in range(n_blk):
        vt_scr[r, :LANES, :] = v_ref[r * LANES:(r + 1) * LANES, :].astype(F32).T.astype(BF16)
        vt_scr[r, LANES:, :] = jnp.ones((VT_ROWS - LANES, LANES), BF16)

    def window(t):
        return max(t - n_prev, 0), min(t + 1, N_WIN_TILES)

    def scores(t):
        kt, n_tiles = window(t)
        rows = n_tiles * LANES
        q2 = q_ref[t * Q_BLOCK:(t + 1) * Q_BLOCK, :].astype(F32) * (A_HEAD_DIM ** -0.5 * LOG2E)
        q4 = jnp.concatenate(
            [jnp.where(lo, q2, 0.0), jnp.where(lo, 0.0, q2)], axis=0).astype(BF16)
        kw = k_ref[kt * LANES:kt * LANES + rows, :]
        s = _dot_nt(kw, q4)
        m = None
        for i in range(n_tiles):
            tile = s[i * LANES:(i + 1) * LANES, :]
            w = N_WIN_TILES - n_tiles + i
            if w not in ZERO_BIAS_TILES:
                tile = tile + bias_ref[w * LANES:(w + 1) * LANES, :]
            s_scr[t % N_SLOTS, i * LANES:(i + 1) * LANES, :] = tile
            tile_max = jnp.max(tile.reshape(LANES // 8, 8, 2 * Q_BLOCK), axis=0)
            m = tile_max if m is None else jnp.maximum(m, tile_max)
        m_scr[t % N_SLOTS] = jnp.max(m, axis=0, keepdims=True)

    def exps(t):
        rows = window(t)[1] * LANES
        d = (s_scr[t % N_SLOTS, :rows, :] - m_scr[t % N_SLOTS]).astype(BF16)
        e_scr[t % N_SLOTS, :rows, :] = jnp.exp2(d)

    def weighted_sum(t):
        kt, n_tiles = window(t)
        vt = jnp.concatenate([vt_scr[kt + i] for i in range(n_tiles)], axis=1)
        ot = _dot(vt, e_scr[t % N_SLOTS, :n_tiles * LANES, :])
        inv = 1.0 / ot[LANES:LANES + 1, :]
        o_pair = jnp.concatenate(
            [ot[:A_HEAD_DIM, :Q_BLOCK] * inv[:, :Q_BLOCK],
             ot[A_HEAD_DIM:LANES, Q_BLOCK:] * inv[:, Q_BLOCK:]], axis=0)
        o_ref[t * Q_BLOCK:(t + 1) * Q_BLOCK, :] = o_pair.T.astype(BF16)

    for step in range(n_blk + 2 * STAGE_LAG):
        if step < n_blk:
            scores(step)
        if 0 <= step - STAGE_LAG < n_blk:
            exps(step - STAGE_LAG)
        if 0 <= step - 2 * STAGE_LAG < n_blk:
            weighted_sum(step - 2 * STAGE_LAG)


def _attention(qkva, bias, layer):
    bsz, seq, _ = qkva.shape
    n_pair = A_HEADS // 2
    blk = (None, seq, LANES)
    return pl.pallas_call(
        _attn_kernel,
        grid=(bsz, n_pair),
        in_specs=[
            pl.BlockSpec(blk, lambda b, p: (b, 0, p)),
            pl.BlockSpec(blk, lambda b, p: (b, 0, n_pair + p)),
            pl.BlockSpec(blk, lambda b, p: (b, 0, 2 * n_pair + p)),
            pl.BlockSpec((None, K_WINDOW, 2 * Q_BLOCK), lambda b, p: (layer * n_pair + p, 0, 0)),
        ],
        out_specs=pl.BlockSpec(blk, lambda b, p: (b, 0, p)),
        out_shape=jax.ShapeDtypeStruct((bsz, seq, A_WIDTH), BF16),
        scratch_shapes=[pltpu.VMEM((seq // LANES, VT_ROWS, LANES), BF16),
                        pltpu.VMEM((N_SLOTS, K_WINDOW, 2 * Q_BLOCK), F32),
                        pltpu.VMEM((N_SLOTS, 1, 2 * Q_BLOCK), F32),
                        pltpu.VMEM((N_SLOTS, K_WINDOW, 2 * Q_BLOCK), BF16)],
        compiler_params=pltpu.CompilerParams(
            dimension_semantics=("parallel", "parallel"), vmem_limit_bytes=VMEM_LIMIT),
        name="band_attention",
    )(qkva, qkva, qkva, bias)


def _chunk_cumsum(x):
    row = lax.broadcasted_iota(jnp.int32, x.shape, 0) % CHUNK
    d = 1
    while d < CHUNK:
        x = x + jnp.where(row >= d, pltpu.roll(x, d, axis=0), 0.0)
        d *= 2
    return x


def _gla_kernel(qe_ref, qo_ref, kt_ref, kend_ref, v_ref, vt_ref, gate_ref, decay_ref, g_ref,
                o_ref):
    seq = qe_ref.shape[0]
    n_pair = B_HEADS // 2
    lo = lax.broadcasted_iota(jnp.int32, (1, LANES), 1) < B_KEY_DIM
    ri = lax.broadcasted_iota(jnp.int32, (2 * CHUNK, CHUNK), 0)
    ci = lax.broadcasted_iota(jnp.int32, (2 * CHUNK, CHUNK), 1)
    causal = (ri % CHUNK) >= ci
    group = GLA_GROUP * CHUNK

    def body(i, states):
        rows = pl.ds(pl.multiple_of(i * group, group), group)
        q_even = qe_ref[rows, :]
        q_odd = qo_ref[rows, :]
        kt = kt_ref[rows, :]
        kend = kend_ref[rows, :]
        v = v_ref[rows, :]
        decay = decay_ref[pl.ds(pl.multiple_of(i * GLA_GROUP, GLA_GROUP), GLA_GROUP), :]
        vt = vt_ref[i]
        pad = jnp.zeros((CHUNK, LANES), BF16)
        states = list(states)
        outs = []
        for c in range(GLA_GROUP):
            sl = slice(c * CHUNK, (c + 1) * CHUNK)
            pair_cols = slice((c // 2) * LANES, (c // 2 + 1) * LANES)
            o_c = []
            for p in range(n_pair):
                kl = slice(p * LANES, (p + 1) * LANES)
                ve = slice(2 * p * B_VAL_DIM, (2 * p + 1) * B_VAL_DIM)
                vo = slice((2 * p + 1) * B_VAL_DIM, (2 * p + 2) * B_VAL_DIM)
                st = states[p]
                q4 = jnp.concatenate([q_even[sl, kl], q_odd[sl, kl]], axis=0)
                att = jnp.where(causal, _dot_nt(q4, kt[sl, kl]), 0.0).astype(BF16)
                inter = _dot_nt(q4, st.astype(BF16))
                o_c += [_dot(att[:CHUNK], v[sl, ve]) + inter[:CHUNK],
                        _dot(att[CHUNK:], v[sl, vo]) + inter[CHUNK:]]
                kend_c = [kend[sl, kl], pad] if c % 2 == 0 else [pad, kend[sl, kl]]
                d_full = _dot(vt[2 * p * B_VAL_DIM:(2 * p + 2) * B_VAL_DIM, pair_cols],
                              jnp.concatenate(kend_c, axis=0))
                d_st = jnp.where(lo, d_full[:B_VAL_DIM], d_full[B_VAL_DIM:])
                states[p] = st * decay[c:c + 1, kl] + d_st
            outs.append(jnp.concatenate(o_c, axis=1))
        o = jnp.concatenate(outs, axis=0)
        gate = gate_ref[rows, :]
        for h in range(B_HEADS):
            hl = slice(h * B_VAL_DIM, (h + 1) * B_VAL_DIM)
            oh = o[:, hl]
            oh = oh * lax.rsqrt(jnp.mean(oh * oh, axis=-1, keepdims=True) + EPS) * g_ref[:, hl]
            o_ref[rows, hl] = (oh * gate[:, hl]).astype(BF16)
        return tuple(states)

    zero = jnp.zeros((B_VAL_DIM, LANES), F32)
    lax.fori_loop(0, seq // group, body, (zero,) * n_pair)


def _gla(gq, vt, gate, decay, g):
    bsz, seq, _ = gq.shape
    kblk = (None, seq, B_KEY_WIDTH)
    vblk = (None, seq, B_WIDTH)
    return pl.pallas_call(
        _gla_kernel,
        grid=(bsz,),
        in_specs=[pl.BlockSpec(kblk, lambda b, j=j: (b, 0, j)) for j in range(4)] + [
            pl.BlockSpec(vblk, lambda b: (b, 0, 4 * B_KEY_WIDTH // B_WIDTH)),
            pl.BlockSpec((None,) + vt.shape[1:], lambda b: (b, 0, 0, 0)),
            pl.BlockSpec(vblk, lambda b: (b, 0, 0)),
            pl.BlockSpec((None, seq // CHUNK, B_KEY_WIDTH), lambda b: (b, 0, 0)),
            _const_spec((1, B_WIDTH)),
        ],
        out_specs=pl.BlockSpec(vblk, lambda b: (b, 0, 0)),
        out_shape=jax.ShapeDtypeStruct((bsz, seq, B_WIDTH), BF16),
        compiler_params=pltpu.CompilerParams(
            dimension_semantics=("parallel",), vmem_limit_bytes=VMEM_LIMIT),
        name="gla",
    )(gq, gq, gq, gq, gq, vt, gate, decay, g)


def _merge_kernel(res_ref, g_ref, ya_ref, yb_ref, wgate_ref, wbr_ref, wout_ref, o_ref):
    slab = res_ref.shape[0] // ROW_SLABS
    for r in range(ROW_SLABS):
        rows = slice(r * slab, (r + 1) * slab)
        res = res_ref[rows, :]
        h = _rmsnorm_bf16(res, g_ref[...])
        gates = _dot_nt(h, wgate_ref[0])
        u_a = _dot(ya_ref[rows, :], wbr_ref[0])
        u_b = _dot(yb_ref[rows, :], wbr_ref[1])
        merged = _sigmoid(gates[:, :D_MODEL]) * u_a + _sigmoid(gates[:, D_MODEL:]) * u_b
        o_ref[rows, :] = res + _dot(merged.astype(BF16), wout_ref[...])


def _merge(res, g, ya, yb, w_t, gate_row0, wbr, wout, tm, layer):
    t = res.shape[0]
    return pl.pallas_call(
        _merge_kernel,
        grid=(t // tm,),
        in_specs=[
            pl.BlockSpec((tm, D_MODEL), lambda i: (i, 0)),
            _const_spec((1, D_MODEL)),
            pl.BlockSpec((tm, A_WIDTH), lambda i: (i, 0)),
            pl.BlockSpec((tm, B_WIDTH), lambda i: (i, 0)),
            pl.BlockSpec((pl.Element(1), pl.Element(2 * D_MODEL), pl.Element(D_MODEL)),
                         lambda i: (layer, gate_row0, 0), pipeline_mode=pl.Buffered(1)),
            _layer_spec(wbr, layer),
            _layer_spec(wout, layer),
        ],
        out_specs=pl.BlockSpec((tm, D_MODEL), lambda i: (i, 0)),
        out_shape=jax.ShapeDtypeStruct((t, D_MODEL), F32),
        compiler_params=pltpu.CompilerParams(
            dimension_semantics=("parallel",), vmem_limit_bytes=VMEM_LIMIT),
        name="merge",
    )(res, g, ya, yb, w_t, wbr, wout)


def _mlp_kernel(res_ref, g_ref, wup_ref, wdn_ref, gf_ref, o_ref, *, final_norm):
    slab = res_ref.shape[0] // ROW_SLABS
    for r in range(ROW_SLABS):
        rows = slice(r * slab, (r + 1) * slab)
        res = res_ref[rows, :]
        h = _rmsnorm_bf16(res, g_ref[...])
        acc = res
        for c in range(D_FF // D_MODEL):
            cols = slice(c * D_MODEL, (c + 1) * D_MODEL)
            up = jnp.maximum(_dot(h, wup_ref[:, cols]), 0.0)
            acc = acc + _dot((up * up).astype(BF16), wdn_ref[cols, :])
        if final_norm:
            ms = jnp.mean(acc * acc, axis=-1, keepdims=True)
            acc = (acc * lax.rsqrt(ms + EPS)) * gf_ref[...]
        o_ref[rows, :] = acc


def _mlp(res, g, wup, wdn, gf, tm, layer, final_norm):
    t = res.shape[0]
    return pl.pallas_call(
        functools.partial(_mlp_kernel, final_norm=final_norm),
        grid=(t // tm,),
        in_specs=[
            pl.BlockSpec((tm, D_MODEL), lambda i: (i, 0)),
            _const_spec((1, D_MODEL)),
            _layer_spec(wup, layer),
            _layer_spec(wdn, layer),
            _const_spec((1, D_MODEL)),
        ],
        out_specs=pl.BlockSpec((tm, D_MODEL), lambda i: (i, 0)),
        out_shape=jax.ShapeDtypeStruct((t, D_MODEL), F32),
        compiler_params=pltpu.CompilerParams(
            dimension_semantics=("parallel",), vmem_limit_bytes=VMEM_LIMIT),
        name="mlp",
    )(res, g, wup, wdn, gf)


def kernel(x, mix_norm_g, w_in, rel_bias, w_gate_lr, b_gate, gla_norm_g, w_branch, w_out,
           mlp_norm_g, w_up, w_down, final_norm_g):
    bsz, seq, d = x.shape
    t = bsz * seq
    tm = GLA_GROUP * CHUNK
    ab_cols = 3 * A_WIDTH + 2 * B_KEY_WIDTH + 2 * B_WIDTH
    w_in_t = jnp.swapaxes(w_in, 1, 2).astype(BF16)
    w_glr = w_gate_lr.astype(BF16)
    w_br = w_branch.astype(BF16)
    w_o = w_out.astype(BF16)
    w_u = w_up.astype(BF16)
    w_d = w_down.astype(BF16)
    bias = _bias_tables(rel_bias)

    res = x.reshape(t, d)
    for l in range(DEPTH):
        qkva, gq, gate, decay, vt = _inproj(res, mix_norm_g[l][None], w_in_t, w_glr,
                                            b_gate[l][None], tm, l)
        ya = _attention(qkva.reshape(bsz, seq, -1), bias, l)
        yb = _gla(gq.reshape(bsz, seq, -1), vt.reshape(bsz, seq // tm, B_WIDTH, tm),
                  gate.reshape(bsz, seq, -1), decay.reshape(bsz, seq // CHUNK, -1),
                  gla_norm_g[l][None])
        res = _merge(res, mix_norm_g[l][None], ya.reshape(t, -1), yb.reshape(t, -1),
                     w_in_t, ab_cols + GATE_RANK, w_br, w_o, DENSE_TM, l)
        res = _mlp(res, mlp_norm_g[l][None], w_u, w_d, final_norm_g[None], DENSE_TM, l,
                   final_norm=(l == DEPTH - 1))
    return res.reshape(bsz, seq, d)
```

```python
import functools

import jax
import jax.numpy as jnp
from jax import lax
from jax.experimental import pallas as pl
from jax.experimental.pallas import tpu as pltpu

D_MODEL = 1024
DEPTH = 4
CHUNK = 64
EPS = 1e-6

A_HEADS = 8
A_HEAD_DIM = 64
A_WIDTH = A_HEADS * A_HEAD_DIM
N_PREV_CHUNKS = 8
PREV = N_PREV_CHUNKS * CHUNK
REL_CLIP = 128
N_REL = 2 * REL_CLIP + 1

B_HEADS = 4
B_KEY_DIM = 64
B_VAL_DIM = 128
B_KEY_WIDTH = B_HEADS * B_KEY_DIM
B_WIDTH = B_HEADS * B_VAL_DIM
GATE_RANK = 16
GATE_TAU = 16.0
D_FF = 4 * D_MODEL

LANES = 128
Q_BLOCK = 2 * CHUNK
K_WINDOW = PREV + Q_BLOCK
N_WIN_TILES = K_WINDOW // 128
VT_ROWS = 128 + 16
STAGE_LAG = 3
N_SLOTS = STAGE_LAG + 1
ZERO_BIAS_TILES = tuple(range(1, (PREV - REL_CLIP) // 128))
LOG2E = 1.4426950408889634
ROW_SLABS = 4
GLA_GROUP = 16
DENSE_TM = 1024
NEAR_START = PREV - REL_CLIP
NEAR_SEG = K_WINDOW - NEAR_START + Q_BLOCK
VMEM_LIMIT = 56 * 1024 * 1024

F32 = jnp.float32
BF16 = jnp.bfloat16
NEG = -1e30


def _rmsnorm_bf16(x, g):
    ms = jnp.mean(x * x, axis=-1, keepdims=True)
    return ((x * lax.rsqrt(ms + EPS)) * g).astype(BF16)


def _sigmoid(x):
    return 1.0 / (1.0 + jnp.exp(-x))


def _dot(a, b):
    return jnp.dot(a, b, preferred_element_type=F32)


def _dot_nt(a, b):
    return lax.dot_general(a, b, (((1,), (1,)), ((), ())), preferred_element_type=F32)


def _const_spec(shape):
    nd = len(shape)
    return pl.BlockSpec(shape, lambda *_: (0,) * nd)


def _layer_spec(stacked, layer):
    rest = stacked.shape[1:]
    return pl.BlockSpec((None,) + rest, lambda *_: (layer,) + (0,) * len(rest),
                        pipeline_mode=pl.Buffered(1))


def _bias_kernel(t_ref, o_ref):
    near_w = K_WINDOW - NEAR_START
    r = lax.broadcasted_iota(jnp.int32, (Q_BLOCK, near_w), 0)
    c = lax.broadcasted_iota(jnp.int32, (Q_BLOCK, near_w), 1)
    kr = lax.broadcasted_iota(jnp.int32, (NEAR_START, Q_BLOCK), 0)
    qc = lax.broadcasted_iota(jnp.int32, (NEAR_START, Q_BLOCK), 1)
    for h in range(t_ref.shape[0]):
        lanes = slice((h % 2) * Q_BLOCK, (h % 2 + 1) * Q_BLOCK)
        t = t_ref[h]
        t = (t - t[:, 0:1]) * LOG2E
        x = jnp.broadcast_to(t, (Q_BLOCK, NEAR_SEG))
        x = pltpu.roll(x, 0, 1, stride=1, stride_axis=0)
        near = x[:, Q_BLOCK:]
        near = jnp.where((r < CHUNK) & (c >= near_w - CHUNK), NEG, near)
        for j in range(near_w // LANES):
            o_ref[h // 2, NEAR_START + j * LANES:NEAR_START + (j + 1) * LANES, lanes] = (
                near[:, j * LANES:(j + 1) * LANES].T)
        o_ref[h // 2, :NEAR_START, lanes] = jnp.where((kr < CHUNK) & (qc >= CHUNK), NEG, 0.0)


def _bias_tables(rel_bias):
    far = rel_bias[..., N_REL - 1:]
    n_far = NEAR_SEG - (N_REL - 2)
    seg = jnp.concatenate([
        jnp.broadcast_to(far, rel_bias.shape[:-1] + (n_far,)),
        rel_bias[..., N_REL - 2:0:-1],
    ], axis=-1)
    seg = seg.reshape(DEPTH * A_HEADS, 1, NEAR_SEG)
    n_pair = A_HEADS // 2
    return pl.pallas_call(
        _bias_kernel,
        grid=(DEPTH,),
        in_specs=[pl.BlockSpec((A_HEADS, 1, NEAR_SEG), lambda l: (l, 0, 0))],
        out_specs=pl.BlockSpec((n_pair, K_WINDOW, 2 * Q_BLOCK), lambda l: (l, 0, 0)),
        out_shape=jax.ShapeDtypeStruct((DEPTH * n_pair, K_WINDOW, 2 * Q_BLOCK), F32),
        name="bias_table",
    )(seg)


def _inproj_kernel(res_ref, g_ref, wa_ref, wb_ref, wlr_ref, wg_ref, bg_ref,
                   qkva_ref, gla_ref, gate_ref, decay_ref, vt_ref):
    tm = res_ref.shape[0]
    n_chunk = tm // CHUNK
    kw = B_KEY_WIDTH
    even = (lax.broadcasted_iota(jnp.int32, (1, kw), 1) // B_KEY_DIM) % 2 == 0
    h = _rmsnorm_bf16(res_ref[...], g_ref[...])
    lr = _dot_nt(h, wlr_ref[...])
    z = _dot(lr.astype(BF16), wg_ref[...]) + bg_ref[...]
    log_sig = jnp.minimum(z, 0.0) - jnp.log(1.0 + jnp.exp(-jnp.abs(z)))
    b = _chunk_cumsum(log_sig / GATE_TAU)
    b3 = b.reshape(n_chunk, CHUNK, kw)
    b_last = b3[:, CHUNK - 1:CHUNK, :]
    decay_ref[...] = jnp.exp(b_last).reshape(n_chunk, kw)
    to_end = jnp.exp(b_last - b3).reshape(tm, kw)
    pb = _dot_nt(h, wb_ref[...])
    qt = pb[:, :kw] * (B_KEY_DIM ** -0.5) * jnp.exp(b)
    k = pb[:, kw:2 * kw]
    gla_ref[:, :kw] = jnp.where(even, qt, 0.0).astype(BF16)
    gla_ref[:, kw:2 * kw] = jnp.where(even, 0.0, qt).astype(BF16)
    gla_ref[:, 2 * kw:3 * kw] = (k * jnp.exp(-b)).astype(BF16)
    gla_ref[:, 3 * kw:4 * kw] = (k * to_end).astype(BF16)
    v = pb[:, 2 * kw:2 * kw + B_WIDTH]
    gla_ref[:, 4 * kw:] = v.astype(BF16)
    vt_ref[...] = v.T.astype(BF16)
    r = pb[:, 2 * kw + B_WIDTH:]
    gate_ref[...] = r * _sigmoid(r)
    qkva_ref[...] = _dot_nt(h, wa_ref[...]).astype(BF16)


def _inproj(res, g, w_t, wg, bg, tm, layer):
    t = res.shape[0]
    gla_cols = 4 * B_KEY_WIDTH + B_WIDTH
    half_rows = 3 * A_WIDTH
    assert half_rows == 2 * B_KEY_WIDTH + 2 * B_WIDTH
    half = (None, half_rows, D_MODEL)
    lr_block = 2 * half_rows // GATE_RANK
    return pl.pallas_call(
        _inproj_kernel,
        grid=(t // tm,),
        in_specs=[
            pl.BlockSpec((tm, D_MODEL), lambda i: (i, 0)),
            _const_spec((1, D_MODEL)),
            pl.BlockSpec(half, lambda i: (layer, 0, 0)),
            pl.BlockSpec(half, lambda i: (layer, 1, 0)),
            pl.BlockSpec((None, GATE_RANK, D_MODEL), lambda i: (layer, lr_block, 0)),
            _layer_spec(wg, layer),
            _const_spec((1, B_KEY_WIDTH)),
        ],
        out_specs=[
            pl.BlockSpec((tm, 3 * A_WIDTH), lambda i: (i, 0)),
            pl.BlockSpec((tm, gla_cols), lambda i: (i, 0)),
            pl.BlockSpec((tm, B_WIDTH), lambda i: (i, 0)),
            pl.BlockSpec((tm // CHUNK, B_KEY_WIDTH), lambda i: (i, 0)),
            pl.BlockSpec((None, B_WIDTH, tm), lambda i: (i, 0, 0)),
        ],
        out_shape=[
            jax.ShapeDtypeStruct((t, 3 * A_WIDTH), BF16),
            jax.ShapeDtypeStruct((t, gla_cols), BF16),
            jax.ShapeDtypeStruct((t, B_WIDTH), F32),
            jax.ShapeDtypeStruct((t // CHUNK, B_KEY_WIDTH), F32),
            jax.ShapeDtypeStruct((t // tm, B_WIDTH, tm), BF16),
        ],
        compiler_params=pltpu.CompilerParams(
            dimension_semantics=("parallel",), vmem_limit_bytes=VMEM_LIMIT),
        name="inproj",
    )(res, g, w_t, w_t, w_t, wg, bg)


def _attn_kernel(q_ref, k_ref, v_ref, bias_ref, o_ref, vt_scr, s_scr, m_scr, e_scr):
    seq = q_ref.shape[0]
    n_blk = seq // Q_BLOCK
    n_prev = PREV // Q_BLOCK
    lo = lax.broadcasted_iota(jnp.int32, (1, LANES), 1) < A_HEAD_DIM

    for r in range(n_blk):
        vt_scr[r, :LANES, :] = v_ref[r * LANES:(r + 1) * LANES, :].astype(F32).T.astype(BF16)
        vt_scr[r, LANES:, :] = jnp.ones((VT_ROWS - LANES, LANES), BF16)

    def window(t):
        return max(t - n_prev, 0), min(t + 1, N_WIN_TILES)

    def scores(t):
        kt, n_tiles = window(t)
        rows = n_tiles * LANES
        q2 = q_ref[t * Q_BLOCK:(t + 1) * Q_BLOCK, :].astype(F32) * (A_HEAD_DIM ** -0.5 * LOG2E)
        q4 = jnp.concatenate(
            [jnp.where(lo, q2, 0.0), jnp.where(lo, 0.0, q2)], axis=0).astype(BF16)
        kw = k_ref[kt * LANES:kt * LANES + rows, :]
        s = _dot_nt(kw, q4)
        m = None
        for i in range(n_tiles):
            tile = s[i * LANES:(i + 1) * LANES, :]
            w = N_WIN_TILES - n_tiles + i
            if w not in ZERO_BIAS_TILES:
                tile = tile + bias_ref[w * LANES:(w + 1) * LANES, :]
            s_scr[t % N_SLOTS, i * LANES:(i + 1) * LANES, :] = tile
            tile_max = jnp.max(tile.reshape(LANES // 8, 8, 2 * Q_BLOCK), axis=0)
            m = tile_max if m is None else jnp.maximum(m, tile_max)
        m_scr[t % N_SLOTS] = jnp.max(m, axis=0, keepdims=True)

    def exps(t):
        rows = window(t)[1] * LANES
        d = (s_scr[t % N_SLOTS, :rows, :] - m_scr[t % N_SLOTS]).astype(BF16)
        e_scr[t % N_SLOTS, :rows, :] = jnp.exp2(d)

    def weighted_sum(t):
        kt, n_tiles = window(t)
        vt = jnp.concatenate([vt_scr[kt + i] for i in range(n_tiles)], axis=1)
        ot = _dot(vt, e_scr[t % N_SLOTS, :n_tiles * LANES, :])
        inv = 1.0 / ot[LANES:LANES + 1, :]
        o_pair = jnp.concatenate(
            [ot[:A_HEAD_DIM, :Q_BLOCK] * inv[:, :Q_BLOCK],
             ot[A_HEAD_DIM:LANES, Q_BLOCK:] * inv[:, Q_BLOCK:]], axis=0)
        o_ref[t * Q_BLOCK:(t + 1) * Q_BLOCK, :] = o_pair.T.astype(BF16)

    for step in range(n_blk + 2 * STAGE_LAG):
        if step < n_blk:
            scores(step)
        if 0 <= step - STAGE_LAG < n_blk:
            exps(step - STAGE_LAG)
        if 0 <= step - 2 * STAGE_LAG < n_blk:
            weighted_sum(step - 2 * STAGE_LAG)


def _attention(qkva, bias, layer):
    bsz, seq, _ = qkva.shape
    n_pair = A_HEADS // 2
    blk = (None, seq, LANES)
    return pl.pallas_call(
        _attn_kernel,
        grid=(bsz, n_pair),
        in_specs=[
            pl.BlockSpec(blk, lambda b, p: (b, 0, p)),
            pl.BlockSpec(blk, lambda b, p: (b, 0, n_pair + p)),
            pl.BlockSpec(blk, lambda b, p: (b, 0, 2 * n_pair + p)),
            pl.BlockSpec((None, K_WINDOW, 2 * Q_BLOCK), lambda b, p: (layer * n_pair + p, 0, 0)),
        ],
        out_specs=pl.BlockSpec(blk, lambda b, p: (b, 0, p)),
        out_shape=jax.ShapeDtypeStruct((bsz, seq, A_WIDTH), BF16),
        scratch_shapes=[pltpu.VMEM((seq // LANES, VT_ROWS, LANES), BF16),
                        pltpu.VMEM((N_SLOTS, K_WINDOW, 2 * Q_BLOCK), F32),
                        pltpu.VMEM((N_SLOTS, 1, 2 * Q_BLOCK), F32),
                        pltpu.VMEM((N_SLOTS, K_WINDOW, 2 * Q_BLOCK), BF16)],
        compiler_params=pltpu.CompilerParams(
            dimension_semantics=("parallel", "parallel"), vmem_limit_bytes=VMEM_LIMIT),
        name="band_attention",
    )(qkva, qkva, qkva, bias)


def _chunk_cumsum(x):
    row = lax.broadcasted_iota(jnp.int32, x.shape, 0) % CHUNK
    d = 1
    while d < CHUNK:
        x = x + jnp.where(row >= d, pltpu.roll(x, d, axis=0), 0.0)
        d *= 2
    return x


def _gla_kernel(qe_ref, qo_ref, kt_ref, kend_ref, v_ref, vt_ref, gate_ref, decay_ref, g_ref,
                o_ref):
    seq = qe_ref.shape[0]
    n_pair = B_HEADS // 2
    lo = lax.broadcasted_iota(jnp.int32, (1, LANES), 1) < B_KEY_DIM
    ri = lax.broadcasted_iota(jnp.int32, (2 * CHUNK, CHUNK), 0)
    ci = lax.broadcasted_iota(jnp.int32, (2 * CHUNK, CHUNK), 1)
    causal = (ri % CHUNK) >= ci
    group = GLA_GROUP * CHUNK

    def body(i, states):
        rows = pl.ds(pl.multiple_of(i * group, group), group)
        q_even = qe_ref[rows, :]
        q_odd = qo_ref[rows, :]
        kt = kt_ref[rows, :]
        kend = kend_ref[rows, :]
        v = v_ref[rows, :]
        decay = decay_ref[pl.ds(pl.multiple_of(i * GLA_GROUP, GLA_GROUP), GLA_GROUP), :]
        vt = vt_ref[i]
        pad = jnp.zeros((CHUNK, LANES), BF16)
        states = list(states)
        outs = []
        for c in range(GLA_GROUP):
            sl = slice(c * CHUNK, (c + 1) * CHUNK)
            pair_cols = slice((c // 2) * LANES, (c // 2 + 1) * LANES)
            o_c = []
            for p in range(n_pair):
                kl = slice(p * LANES, (p + 1) * LANES)
                ve = slice(2 * p * B_VAL_DIM, (2 * p + 1) * B_VAL_DIM)
                vo = slice((2 * p + 1) * B_VAL_DIM, (2 * p + 2) * B_VAL_DIM)
                st = states[p]
                q4 = jnp.concatenate([q_even[sl, kl], q_odd[sl, kl]], axis=0)
                att = jnp.where(causal, _dot_nt(q4, kt[sl, kl]), 0.0).astype(BF16)
                inter = _dot_nt(q4, st.astype(BF16))
                o_c += [_dot(att[:CHUNK], v[sl, ve]) + inter[:CHUNK],
                        _dot(att[CHUNK:], v[sl, vo]) + inter[CHUNK:]]
                kend_c = [kend[sl, kl], pad] if c % 2 == 0 else [pad, kend[sl, kl]]
                d_full = _dot(vt[2 * p * B_VAL_DIM:(2 * p + 2) * B_VAL_DIM, pair_cols],
                              jnp.concatenate(kend_c, axis=0))
                d_st = jnp.where(lo, d_full[:B_VAL_DIM], d_full[B_VAL_DIM:])
                states[p] = st * decay[c:c + 1, kl] + d_st
            outs.append(jnp.concatenate(o_c, axis=1))
        o = jnp.concatenate(outs, axis=0)
        gate = gate_ref[rows, :]
        for h in range(B_HEADS):
            hl = slice(h * B_VAL_DIM, (h + 1) * B_VAL_DIM)
            oh = o[:, hl]
            oh = oh * lax.rsqrt(jnp.mean(oh * oh, axis=-1, keepdims=True) + EPS) * g_ref[:, hl]
            o_ref[rows, hl] = (oh * gate[:, hl]).astype(BF16)
        return tuple(states)

    zero = jnp.zeros((B_VAL_DIM, LANES), F32)
    lax.fori_loop(0, seq // group, body, (zero,) * n_pair)


def _gla(gq, vt, gate, decay, g):
    bsz, seq, _ = gq.shape
    kblk = (None, seq, B_KEY_WIDTH)
    vblk = (None, seq, B_WIDTH)
    return pl.pallas_call(
        _gla_kernel,
        grid=(bsz,),
        in_specs=[pl.BlockSpec(kblk, lambda b, j=j: (b, 0, j)) for j in range(4)] + [
            pl.BlockSpec(vblk, lambda b: (b, 0, 4 * B_KEY_WIDTH // B_WIDTH)),
            pl.BlockSpec((None,) + vt.shape[1:], lambda b: (b, 0, 0, 0)),
            pl.BlockSpec(vblk, lambda b: (b, 0, 0)),
            pl.BlockSpec((None, seq // CHUNK, B_KEY_WIDTH), lambda b: (b, 0, 0)),
            _const_spec((1, B_WIDTH)),
        ],
        out_specs=pl.BlockSpec(vblk, lambda b: (b, 0, 0)),
        out_shape=jax.ShapeDtypeStruct((bsz, seq, B_WIDTH), BF16),
        compiler_params=pltpu.CompilerParams(
            dimension_semantics=("parallel",), vmem_limit_bytes=VMEM_LIMIT),
        name="gla",
    )(gq, gq, gq, gq, gq, vt, gate, decay, g)


def _merge_kernel(res_ref, g_ref, ya_ref, yb_ref, wgate_ref, wbr_ref, wout_ref, o_ref):
    slab = res_ref.shape[0] // ROW_SLABS
    for r in range(ROW_SLABS):
        rows = slice(r * slab, (r + 1) * slab)
        res = res_ref[rows, :]
        h = _rmsnorm_bf16(res, g_ref[...])
        gates = _dot_nt(h, wgate_ref[0])
        u_a = _dot(ya_ref[rows, :], wbr_ref[0])
        u_b = _dot(yb_ref[rows, :], wbr_ref[1])
        merged = _sigmoid(gates[:, :D_MODEL]) * u_a + _sigmoid(gates[:, D_MODEL:]) * u_b
        o_ref[rows, :] = res + _dot(merged.astype(BF16), wout_ref[...])


def _merge(res, g, ya, yb, w_t, gate_row0, wbr, wout, tm, layer):
    t = res.shape[0]
    return pl.pallas_call(
        _merge_kernel,
        grid=(t // tm,),
        in_specs=[
            pl.BlockSpec((tm, D_MODEL), lambda i: (i, 0)),
            _const_spec((1, D_MODEL)),
            pl.BlockSpec((tm, A_WIDTH), lambda i: (i, 0)),
            pl.BlockSpec((tm, B_WIDTH), lambda i: (i, 0)),
            pl.BlockSpec((pl.Element(1), pl.Element(2 * D_MODEL), pl.Element(D_MODEL)),
                         lambda i: (layer, gate_row0, 0), pipeline_mode=pl.Buffered(1)),
            _layer_spec(wbr, layer),
            _layer_spec(wout, layer),
        ],
        out_specs=pl.BlockSpec((tm, D_MODEL), lambda i: (i, 0)),
        out_shape=jax.ShapeDtypeStruct((t, D_MODEL), F32),
        compiler_params=pltpu.CompilerParams(
            dimension_semantics=("parallel",), vmem_limit_bytes=VMEM_LIMIT),
        name="merge",
    )(res, g, ya, yb, w_t, wbr, wout)


def _mlp_kernel(res_ref, g_ref, wup_ref, wdn_ref, gf_ref, o_ref, *, final_norm):
    slab = res_ref.shape[0] // ROW_SLABS
    for r in range(ROW_SLABS):
        rows = slice(r * slab, (r + 1) * slab)
        res = res_ref[rows, :]
        h = _rmsnorm_bf16(res, g_ref[...])
        acc = res
        for c in range(D_FF // D_MODEL):
            cols = slice(c * D_MODEL, (c + 1) * D_MODEL)
            up = jnp.maximum(_dot(h, wup_ref[:, cols]), 0.0)
            acc = acc + _dot((up * up).astype(BF16), wdn_ref[cols, :])
        if final_norm:
            ms = jnp.mean(acc * acc, axis=-1, keepdims=True)
            acc = (acc * lax.rsqrt(ms + EPS)) * gf_ref[...]
        o_ref[rows, :] = acc


def _mlp(res, g, wup, wdn, gf, tm, layer, final_norm):
    t = res.shape[0]
    return pl.pallas_call(
        functools.partial(_mlp_kernel, final_norm=final_norm),
        grid=(t // tm,),
        in_specs=[
            pl.BlockSpec((tm, D_MODEL), lambda i: (i, 0)),
            _const_spec((1, D_MODEL)),
            _layer_spec(wup, layer),
            _layer_spec(wdn, layer),
            _const_spec((1, D_MODEL)),
        ],
        out_specs=pl.BlockSpec((tm, D_MODEL), lambda i: (i, 0)),
        out_shape=jax.ShapeDtypeStruct((t, D_MODEL), F32),
        compiler_params=pltpu.CompilerParams(
            dimension_semantics=("parallel",), vmem_limit_bytes=VMEM_LIMIT),
        name="mlp",
    )(res, g, wup, wdn, gf)


def kernel(x, mix_norm_g, w_in, rel_bias, w_gate_lr, b_gate, gla_norm_g, w_branch, w_out,
           mlp_norm_g, w_up, w_down, final_norm_g):
    bsz, seq, d = x.shape
    t = bsz * seq
    tm = GLA_GROUP * CHUNK
    ab_cols = 3 * A_WIDTH + 2 * B_KEY_WIDTH + 2 * B_WIDTH
    w_in_t = jnp.swapaxes(w_in, 1, 2).astype(BF16)
    w_glr = w_gate_lr.astype(BF16)
    w_br = w_branch.astype(BF16)
    w_o = w_out.astype(BF16)
    w_u = w_up.astype(BF16)
    w_d = w_down.astype(BF16)
    bias = _bias_tables(rel_bias)

    res = x.reshape(t, d)
    for l in range(DEPTH):
        qkva, gq, gate, decay, vt = _inproj(res, mix_norm_g[l][None], w_in_t, w_glr,
                                            b_gate[l][None], tm, l)
        ya = _attention(qkva.reshape(bsz, seq, -1), bias, l)
        yb = _gla(gq.reshape(bsz, seq, -1), vt.reshape(bsz, seq // tm, B_WIDTH, tm),
                  gate.reshape(bsz, seq, -1), decay.reshape(bsz, seq // CHUNK, -1),
                  gla_norm_g[l][None])
        res = _merge(res, mix_norm_g[l][None], ya.reshape(t, -1), yb.reshape(t, -1),
                     w_in_t, ab_cols + GATE_RANK, w_br, w_o, DENSE_TM, l)
        res = _mlp(res, mlp_norm_g[l][None], w_u, w_d, final_norm_g[None], DENSE_TM, l,
                   final_norm=(l == DEPTH - 1))
    return res.reshape(bsz, seq, d)
```

```python
import functools

import jax
import jax.numpy as jnp
from jax import lax
from jax.experimental import pallas as pl
from jax.experimental.pallas import tpu as pltpu

D_MODEL = 1024
DEPTH = 4
CHUNK = 64
EPS = 1e-6

A_HEADS = 8
A_HEAD_DIM = 64
A_WIDTH = A_HEADS * A_HEAD_DIM
N_PREV_CHUNKS = 8
PREV = N_PREV_CHUNKS * CHUNK
REL_CLIP = 128
N_REL = 2 * REL_CLIP + 1

B_HEADS = 4
B_KEY_DIM = 64
B_VAL_DIM = 128
B_KEY_WIDTH = B_HEADS * B_KEY_DIM
B_WIDTH = B_HEADS * B_VAL_DIM
GATE_RANK = 16
GATE_TAU = 16.0
D_FF = 4 * D_MODEL

LANES = 128
Q_BLOCK = 2 * CHUNK
K_WINDOW = PREV + Q_BLOCK
N_WIN_TILES = K_WINDOW // 128
VT_ROWS = 128 + 16
ATTN_PAIRS = 2
STAGE_LAG = 3
N_SLOTS = STAGE_LAG + 1
ZERO_BIAS_TILES = tuple(range(1, (PREV - REL_CLIP) // 128))
LOG2E = 1.4426950408889634
SLAB_ROWS = 256
GLA_GROUP = 16
MERGE_TM = 2048
MLP_TM = 1024
NEAR_START = PREV - REL_CLIP
NEAR_SEG = K_WINDOW - NEAR_START + Q_BLOCK
VMEM_LIMIT = 56 * 1024 * 1024

F32 = jnp.float32
BF16 = jnp.bfloat16
NEG = -1e30


def _rmsnorm_bf16(x, g):
    ms = jnp.mean(x * x, axis=-1, keepdims=True)
    return ((x * lax.rsqrt(ms + EPS)) * g).astype(BF16)


def _sigmoid(x):
    return 1.0 / (1.0 + jnp.exp(-x))


def _dot(a, b):
    return jnp.dot(a, b, preferred_element_type=F32)


def _dot_nt(a, b):
    return lax.dot_general(a, b, (((1,), (1,)), ((), ())), preferred_element_type=F32)


def _const_spec(shape):
    nd = len(shape)
    return pl.BlockSpec(shape, lambda *_: (0,) * nd)


def _layer_spec(stacked, layer):
    rest = stacked.shape[1:]
    return pl.BlockSpec((None,) + rest, lambda *_: (layer,) + (0,) * len(rest),
                        pipeline_mode=pl.Buffered(1))


def _bias_kernel(t_ref, o_ref):
    near_w = K_WINDOW - NEAR_START
    r = lax.broadcasted_iota(jnp.int32, (Q_BLOCK, near_w), 0)
    c = lax.broadcasted_iota(jnp.int32, (Q_BLOCK, near_w), 1)
    kr = lax.broadcasted_iota(jnp.int32, (NEAR_START, Q_BLOCK), 0)
    qc = lax.broadcasted_iota(jnp.int32, (NEAR_START, Q_BLOCK), 1)
    for h in range(t_ref.shape[0]):
        lanes = slice((h % 2) * Q_BLOCK, (h % 2 + 1) * Q_BLOCK)
        t = t_ref[h]
        t = (t - t[:, 0:1]) * LOG2E
        x = jnp.broadcast_to(t, (Q_BLOCK, NEAR_SEG))
        x = pltpu.roll(x, 0, 1, stride=1, stride_axis=0)
        near = x[:, Q_BLOCK:]
        near = jnp.where((r < CHUNK) & (c >= near_w - CHUNK), NEG, near)
        for j in range(near_w // LANES):
            o_ref[h // 2, NEAR_START + j * LANES:NEAR_START + (j + 1) * LANES, lanes] = (
                near[:, j * LANES:(j + 1) * LANES].T)
        o_ref[h // 2, :NEAR_START, lanes] = jnp.where((kr < CHUNK) & (qc >= CHUNK), NEG, 0.0)


def _bias_tables(rel_bias):
    far = rel_bias[..., N_REL - 1:]
    n_far = NEAR_SEG - (N_REL - 2)
    seg = jnp.concatenate([
        jnp.broadcast_to(far, rel_bias.shape[:-1] + (n_far,)),
        rel_bias[..., N_REL - 2:0:-1],
    ], axis=-1)
    seg = seg.reshape(DEPTH * A_HEADS, 1, NEAR_SEG)
    n_pair = A_HEADS // 2
    return pl.pallas_call(
        _bias_kernel,
        grid=(DEPTH,),
        in_specs=[pl.BlockSpec((A_HEADS, 1, NEAR_SEG), lambda l: (l, 0, 0))],
        out_specs=pl.BlockSpec((n_pair, K_WINDOW, 2 * Q_BLOCK), lambda l: (l, 0, 0)),
        out_shape=jax.ShapeDtypeStruct((DEPTH * n_pair, K_WINDOW, 2 * Q_BLOCK), F32),
        name="bias_table",
    )(seg)


def _inproj_kernel(res_ref, g_ref, wa_ref, wb_ref, wlr_ref, wg_ref, bg_ref,
                   qkva_ref, gla_ref, gate_ref, decay_ref, vt_ref):
    tm = res_ref.shape[0]
    n_chunk = tm // CHUNK
    kw = B_KEY_WIDTH
    even = (lax.broadcasted_iota(jnp.int32, (1, kw), 1) // B_KEY_DIM) % 2 == 0
    h = _rmsnorm_bf16(res_ref[...], g_ref[...])
    lr = _dot_nt(h, wlr_ref[...])
    z = _dot(lr.astype(BF16), wg_ref[...]) + bg_ref[...]
    log_sig = jnp.minimum(z, 0.0) - jnp.log(1.0 + jnp.exp(-jnp.abs(z)))
    b = _chunk_cumsum(log_sig / GATE_TAU)
    b3 = b.reshape(n_chunk, CHUNK, kw)
    b_last = b3[:, CHUNK - 1:CHUNK, :]
    decay_ref[...] = jnp.exp(b_last).reshape(n_chunk, kw)
    to_end = jnp.exp(b_last - b3).reshape(tm, kw)
    pb = _dot_nt(h, wb_ref[...])
    qt = pb[:, :kw] * (B_KEY_DIM ** -0.5) * jnp.exp(b)
    k = pb[:, kw:2 * kw]
    gla_ref[:, :kw] = jnp.where(even, qt, 0.0).astype(BF16)
    gla_ref[:, kw:2 * kw] = jnp.where(even, 0.0, qt).astype(BF16)
    gla_ref[:, 2 * kw:3 * kw] = (k * jnp.exp(-b)).astype(BF16)
    gla_ref[:, 3 * kw:4 * kw] = (k * to_end).astype(BF16)
    v = pb[:, 2 * kw:2 * kw + B_WIDTH]
    gla_ref[:, 4 * kw:] = v.astype(BF16)
    vt_ref[...] = v.T.astype(BF16)
    r = pb[:, 2 * kw + B_WIDTH:]
    gate_ref[...] = r * _sigmoid(r)
    qkva_ref[...] = _dot_nt(h, wa_ref[...]).astype(BF16)


def _inproj(res, g, w_t, wg, bg, tm, layer):
    t = res.shape[0]
    gla_cols = 4 * B_KEY_WIDTH + B_WIDTH
    half_rows = 3 * A_WIDTH
    assert half_rows == 2 * B_KEY_WIDTH + 2 * B_WIDTH
    half = (None, half_rows, D_MODEL)
    lr_block = 2 * half_rows // GATE_RANK
    return pl.pallas_call(
        _inproj_kernel,
        grid=(t // tm,),
        in_specs=[
            pl.BlockSpec((tm, D_MODEL), lambda i: (i, 0)),
            _const_spec((1, D_MODEL)),
            pl.BlockSpec(half, lambda i: (layer, 0, 0)),
            pl.BlockSpec(half, lambda i: (layer, 1, 0)),
            pl.BlockSpec((None, GATE_RANK, D_MODEL), lambda i: (layer, lr_block, 0)),
            _layer_spec(wg, layer),
            _const_spec((1, B_KEY_WIDTH)),
        ],
        out_specs=[
            pl.BlockSpec((tm, 3 * A_WIDTH), lambda i: (i, 0)),
            pl.BlockSpec((tm, gla_cols), lambda i: (i, 0)),
            pl.BlockSpec((tm, B_WIDTH), lambda i: (i, 0)),
            pl.BlockSpec((tm // CHUNK, B_KEY_WIDTH), lambda i: (i, 0)),
            pl.BlockSpec((None, B_WIDTH, tm), lambda i: (i, 0, 0)),
        ],
        out_shape=[
            jax.ShapeDtypeStruct((t, 3 * A_WIDTH), BF16),
            jax.ShapeDtypeStruct((t, gla_cols), BF16),
            jax.ShapeDtypeStruct((t, B_WIDTH), F32),
            jax.ShapeDtypeStruct((t // CHUNK, B_KEY_WIDTH), F32),
            jax.ShapeDtypeStruct((t // tm, B_WIDTH, tm), BF16),
        ],
        compiler_params=pltpu.CompilerParams(
            dimension_semantics=("parallel",), vmem_limit_bytes=VMEM_LIMIT),
        name="inproj",
    )(res, g, w_t, w_t, w_t, wg, bg)


def _attn_kernel(q_ref, k_ref, v_ref, bias_ref, o_ref, vt_scr, s_scr, m_scr, e_scr):
    seq = q_ref.shape[0]
    n_blk = seq // Q_BLOCK
    n_prev = PREV // Q_BLOCK
    lo = lax.broadcasted_iota(jnp.int32, (1, LANES), 1) < A_HEAD_DIM

    def lanes(pp):
        return slice(pp * LANES, (pp + 1) * LANES)

    for pp in range(ATTN_PAIRS):
        for r in range(n_blk):
            vt_scr[pp, r, :LANES, :] = (
                v_ref[r * LANES:(r + 1) * LANES, lanes(pp)].astype(F32).T.astype(BF16))
            vt_scr[pp, r, LANES:, :] = jnp.ones((VT_ROWS - LANES, LANES), BF16)

    def window(t):
        return max(t - n_prev, 0), min(t + 1, N_WIN_TILES)

    def scores(pp, t):
        kt, n_tiles = window(t)
        rows = n_tiles * LANES
        q2 = (q_ref[t * Q_BLOCK:(t + 1) * Q_BLOCK, lanes(pp)].astype(F32)
              * (A_HEAD_DIM ** -0.5 * LOG2E))
        q4 = jnp.concatenate(
            [jnp.where(lo, q2, 0.0), jnp.where(lo, 0.0, q2)], axis=0).astype(BF16)
        kw = k_ref[kt * LANES:kt * LANES + rows, lanes(pp)]
        s = _dot_nt(kw, q4)
        m = None
        for i in range(n_tiles):
            tile = s[i * LANES:(i + 1) * LANES, :]
            w = N_WIN_TILES - n_tiles + i
            if w not in ZERO_BIAS_TILES:
                tile = tile + bias_ref[pp, w * LANES:(w + 1) * LANES, :]
            s_scr[pp, t % N_SLOTS, i * LANES:(i + 1) * LANES, :] = tile
            tile_max = jnp.max(tile.reshape(LANES // 8, 8, 2 * Q_BLOCK), axis=0)
            m = tile_max if m is None else jnp.maximum(m, tile_max)
        m_scr[pp, t % N_SLOTS] = jnp.max(m, axis=0, keepdims=True)

    def exps(pp, t):
        rows = window(t)[1] * LANES
        d = (s_scr[pp, t % N_SLOTS, :rows, :] - m_scr[pp, t % N_SLOTS]).astype(BF16)
        e_scr[pp, t % N_SLOTS, :rows, :] = jnp.exp2(d)

    def weighted_sum(pp, t):
        kt, n_tiles = window(t)
        vt = jnp.concatenate([vt_scr[pp, kt + i] for i in range(n_tiles)], axis=1)
        ot = _dot(vt, e_scr[pp, t % N_SLOTS, :n_tiles * LANES, :])
        inv = 1.0 / ot[LANES:LANES + 1, :]
        o_pair = jnp.concatenate(
            [ot[:A_HEAD_DIM, :Q_BLOCK] * inv[:, :Q_BLOCK],
             ot[A_HEAD_DIM:LANES, Q_BLOCK:] * inv[:, Q_BLOCK:]], axis=0)
        o_ref[t * Q_BLOCK:(t + 1) * Q_BLOCK, lanes(pp)] = o_pair.T.astype(BF16)

    for step in range(n_blk + 2 * STAGE_LAG):
        for pp in range(ATTN_PAIRS):
            if step < n_blk:
                scores(pp, step)
            if 0 <= step - STAGE_LAG < n_blk:
                exps(pp, step - STAGE_LAG)
            if 0 <= step - 2 * STAGE_LAG < n_blk:
                weighted_sum(pp, step - 2 * STAGE_LAG)


def _attention(qkva, bias, layer):
    bsz, seq, _ = qkva.shape
    n_grp = A_HEADS // 2 // ATTN_PAIRS
    blk = (None, seq, ATTN_PAIRS * LANES)
    return pl.pallas_call(
        _attn_kernel,
        grid=(bsz, n_grp),
        in_specs=[
            pl.BlockSpec(blk, lambda b, p: (b, 0, p)),
            pl.BlockSpec(blk, lambda b, p: (b, 0, n_grp + p)),
            pl.BlockSpec(blk, lambda b, p: (b, 0, 2 * n_grp + p)),
            pl.BlockSpec((ATTN_PAIRS, K_WINDOW, 2 * Q_BLOCK),
                         lambda b, p: (layer * n_grp + p, 0, 0)),
        ],
        out_specs=pl.BlockSpec(blk, lambda b, p: (b, 0, p)),
        out_shape=jax.ShapeDtypeStruct((bsz, seq, A_WIDTH), BF16),
        scratch_shapes=[pltpu.VMEM((ATTN_PAIRS, seq // LANES, VT_ROWS, LANES), BF16),
                        pltpu.VMEM((ATTN_PAIRS, N_SLOTS, K_WINDOW, 2 * Q_BLOCK), F32),
                        pltpu.VMEM((ATTN_PAIRS, N_SLOTS, 1, 2 * Q_BLOCK), F32),
                        pltpu.VMEM((ATTN_PAIRS, N_SLOTS, K_WINDOW, 2 * Q_BLOCK), BF16)],
        compiler_params=pltpu.CompilerParams(
            dimension_semantics=("parallel", "parallel"), vmem_limit_bytes=VMEM_LIMIT),
        name="band_attention",
    )(qkva, qkva, qkva, bias)


def _chunk_cumsum(x):
    row = lax.broadcasted_iota(jnp.int32, x.shape, 0) % CHUNK
    d = 1
    while d < CHUNK:
        x = x + jnp.where(row >= d, pltpu.roll(x, d, axis=0), 0.0)
        d *= 2
    return x


def _gla_kernel(qe_ref, qo_ref, kt_ref, kend_ref, v_ref, vt_ref, gate_ref, decay_ref, g_ref,
                o_ref):
    seq = qe_ref.shape[0]
    n_pair = B_HEADS // 2
    lo = lax.broadcasted_iota(jnp.int32, (1, LANES), 1) < B_KEY_DIM
    ri = lax.broadcasted_iota(jnp.int32, (2 * CHUNK, CHUNK), 0)
    ci = lax.broadcasted_iota(jnp.int32, (2 * CHUNK, CHUNK), 1)
    causal = (ri % CHUNK) >= ci
    group = GLA_GROUP * CHUNK

    def body(i, states):
        rows = pl.ds(pl.multiple_of(i * group, group), group)
        q_even = qe_ref[rows, :]
        q_odd = qo_ref[rows, :]
        kt = kt_ref[rows, :]
        kend = kend_ref[rows, :]
        v = v_ref[rows, :]
        decay = decay_ref[pl.ds(pl.multiple_of(i * GLA_GROUP, GLA_GROUP), GLA_GROUP), :]
        vt = vt_ref[i]
        pad = jnp.zeros((CHUNK, LANES), BF16)
        states = list(states)
        outs = []
        for c in range(GLA_GROUP):
            sl = slice(c * CHUNK, (c + 1) * CHUNK)
            pair_cols = slice((c // 2) * LANES, (c // 2 + 1) * LANES)
            o_c = []
            for p in range(n_pair):
                kl = slice(p * LANES, (p + 1) * LANES)
                ve = slice(2 * p * B_VAL_DIM, (2 * p + 1) * B_VAL_DIM)
                vo = slice((2 * p + 1) * B_VAL_DIM, (2 * p + 2) * B_VAL_DIM)
                st = states[p]
                q4 = jnp.concatenate([q_even[sl, kl], q_odd[sl, kl]], axis=0)
                att = jnp.where(causal, _dot_nt(q4, kt[sl, kl]), 0.0).astype(BF16)
                inter = _dot_nt(q4, st.astype(BF16))
                o_c += [_dot(att[:CHUNK], v[sl, ve]) + inter[:CHUNK],
                        _dot(att[CHUNK:], v[sl, vo]) + inter[CHUNK:]]
                kend_c = [kend[sl, kl], pad] if c % 2 == 0 else [pad, kend[sl, kl]]
                d_full = _dot(vt[2 * p * B_VAL_DIM:(2 * p + 2) * B_VAL_DIM, pair_cols],
                              jnp.concatenate(kend_c, axis=0))
                d_st = jnp.where(lo, d_full[:B_VAL_DIM], d_full[B_VAL_DIM:])
                states[p] = st * decay[c:c + 1, kl] + d_st
            outs.append(jnp.concatenate(o_c, axis=1))
        o = jnp.concatenate(outs, axis=0)
        gate = gate_ref[rows, :]
        for h in range(B_HEADS):
            hl = slice(h * B_VAL_DIM, (h + 1) * B_VAL_DIM)
            oh = o[:, hl]
            oh = oh * lax.rsqrt(jnp.mean(oh * oh, axis=-1, keepdims=True) + EPS) * g_ref[:, hl]
            o_ref[rows, hl] = (oh * gate[:, hl]).astype(BF16)
        return tuple(states)

    zero = jnp.zeros((B_VAL_DIM, LANES), F32)
    lax.fori_loop(0, seq // group, body, (zero,) * n_pair)


def _gla(gq, vt, gate, decay, g):
    bsz, seq, _ = gq.shape
    kblk = (None, seq, B_KEY_WIDTH)
    vblk = (None, seq, B_WIDTH)
    return pl.pallas_call(
        _gla_kernel,
        grid=(bsz,),
        in_specs=[pl.BlockSpec(kblk, lambda b, j=j: (b, 0, j)) for j in range(4)] + [
            pl.BlockSpec(vblk, lambda b: (b, 0, 4 * B_KEY_WIDTH // B_WIDTH)),
            pl.BlockSpec((None,) + vt.shape[1:], lambda b: (b, 0, 0, 0)),
            pl.BlockSpec(vblk, lambda b: (b, 0, 0)),
            pl.BlockSpec((None, seq // CHUNK, B_KEY_WIDTH), lambda b: (b, 0, 0)),
            _const_spec((1, B_WIDTH)),
        ],
        out_specs=pl.BlockSpec(vblk, lambda b: (b, 0, 0)),
        out_shape=jax.ShapeDtypeStruct((bsz, seq, B_WIDTH), BF16),
        compiler_params=pltpu.CompilerParams(
            dimension_semantics=("parallel",), vmem_limit_bytes=VMEM_LIMIT),
        name="gla",
    )(gq, gq, gq, gq, gq, vt, gate, decay, g)


def _merge_kernel(res_ref, g_ref, ya_ref, yb_ref, wgate_ref, wbr_ref, wout_ref, o_ref):
    for r in range(res_ref.shape[0] // SLAB_ROWS):
        rows = slice(r * SLAB_ROWS, (r + 1) * SLAB_ROWS)
        res = res_ref[rows, :]
        h = _rmsnorm_bf16(res, g_ref[...])
        gates = _dot_nt(h, wgate_ref[0])
        u_a = _dot(ya_ref[rows, :], wbr_ref[0])
        u_b = _dot(yb_ref[rows, :], wbr_ref[1])
        merged = _sigmoid(gates[:, :D_MODEL]) * u_a + _sigmoid(gates[:, D_MODEL:]) * u_b
        o_ref[rows, :] = res + _dot(merged.astype(BF16), wout_ref[...])


def _merge(res, g, ya, yb, w_t, gate_row0, wbr, wout, tm, layer):
    t = res.shape[0]
    return pl.pallas_call(
        _merge_kernel,
        grid=(t // tm,),
        in_specs=[
            pl.BlockSpec((tm, D_MODEL), lambda i: (i, 0)),
            _const_spec((1, D_MODEL)),
            pl.BlockSpec((tm, A_WIDTH), lambda i: (i, 0)),
            pl.BlockSpec((tm, B_WIDTH), lambda i: (i, 0)),
            pl.BlockSpec((pl.Element(1), pl.Element(2 * D_MODEL), pl.Element(D_MODEL)),
                         lambda i: (layer, gate_row0, 0), pipeline_mode=pl.Buffered(1)),
            _layer_spec(wbr, layer),
            _layer_spec(wout, layer),
        ],
        out_specs=pl.BlockSpec((tm, D_MODEL), lambda i: (i, 0)),
        out_shape=jax.ShapeDtypeStruct((t, D_MODEL), F32),
        compiler_params=pltpu.CompilerParams(
            dimension_semantics=("parallel",), vmem_limit_bytes=VMEM_LIMIT),
        name="merge",
    )(res, g, ya, yb, w_t, wbr, wout)


def _mlp_kernel(res_ref, g_ref, wup_ref, wdn_ref, gf_ref, o_ref, *, final_norm):
    for r in range(res_ref.shape[0] // SLAB_ROWS):
        rows = slice(r * SLAB_ROWS, (r + 1) * SLAB_ROWS)
        res = res_ref[rows, :]
        h = _rmsnorm_bf16(res, g_ref[...])
        acc = res
        for c in range(D_FF // D_MODEL):
            cols = slice(c * D_MODEL, (c + 1) * D_MODEL)
            up = jnp.maximum(_dot(h, wup_ref[:, cols]), 0.0)
            acc = acc + _dot((up * up).astype(BF16), wdn_ref[cols, :])
        if final_norm:
            ms = jnp.mean(acc * acc, axis=-1, keepdims=True)
            acc = (acc * lax.rsqrt(ms + EPS)) * gf_ref[...]
        o_ref[rows, :] = acc


def _mlp(res, g, wup, wdn, gf, tm, layer, final_norm):
    t = res.shape[0]
    return pl.pallas_call(
        functools.partial(_mlp_kernel, final_norm=final_norm),
        grid=(t // tm,),
        in_specs=[
            pl.BlockSpec((tm, D_MODEL), lambda i: (i, 0)),
            _const_spec((1, D_MODEL)),
            _layer_spec(wup, layer),
            _layer_spec(wdn, layer),
            _const_spec((1, D_MODEL)),
        ],
        out_specs=pl.BlockSpec((tm, D_MODEL), lambda i: (i, 0)),
        out_shape=jax.ShapeDtypeStruct((t, D_MODEL), F32),
        compiler_params=pltpu.CompilerParams(
            dimension_semantics=("parallel",), vmem_limit_bytes=VMEM_LIMIT),
        name="mlp",
    )(res, g, wup, wdn, gf)


def kernel(x, mix_norm_g, w_in, rel_bias, w_gate_lr, b_gate, gla_norm_g, w_branch, w_out,
           mlp_norm_g, w_up, w_down, final_norm_g):
    bsz, seq, d = x.shape
    t = bsz * seq
    tm = GLA_GROUP * CHUNK
    ab_cols = 3 * A_WIDTH + 2 * B_KEY_WIDTH + 2 * B_WIDTH
    w_in_t = jnp.swapaxes(w_in, 1, 2).astype(BF16)
    w_glr = w_gate_lr.astype(BF16)
    w_br = w_branch.astype(BF16)
    w_o = w_out.astype(BF16)
    w_u = w_up.astype(BF16)
    w_d = w_down.astype(BF16)
    bias = _bias_tables(rel_bias)

    res = x.reshape(t, d)
    for l in range(DEPTH):
        qkva, gq, gate, decay, vt = _inproj(res, mix_norm_g[l][None], w_in_t, w_glr,
                                            b_gate[l][None], tm, l)
        ya = _attention(qkva.reshape(bsz, seq, -1), bias, l)
        yb = _gla(gq.reshape(bsz, seq, -1), vt.reshape(bsz, seq // tm, B_WIDTH, tm),
                  gate.reshape(bsz, seq, -1), decay.reshape(bsz, seq // CHUNK, -1),
                  gla_norm_g[l][None])
        res = _merge(res, mix_norm_g[l][None], ya.reshape(t, -1), yb.reshape(t, -1),
                     w_in_t, ab_cols + GATE_RANK, w_br, w_o, MERGE_TM, l)
        res = _mlp(res, mlp_norm_g[l][None], w_u, w_d, final_norm_g[None], MLP_TM, l,
                   final_norm=(l == DEPTH - 1))
    return res.reshape(bsz, seq, d)
```

```python
import functools

import jax
import jax.numpy as jnp
from jax import lax
from jax.experimental import pallas as pl
from jax.experimental.pallas import tpu as pltpu

D_MODEL = 1024
DEPTH = 4
CHUNK = 64
EPS = 1e-6

A_HEADS = 8
A_HEAD_DIM = 64
A_WIDTH = A_HEADS * A_HEAD_DIM
N_PREV_CHUNKS = 8
PREV = N_PREV_CHUNKS * CHUNK
REL_CLIP = 128
N_REL = 2 * REL_CLIP + 1

B_HEADS = 4
B_KEY_DIM = 64
B_VAL_DIM = 128
B_KEY_WIDTH = B_HEADS * B_KEY_DIM
B_WIDTH = B_HEADS * B_VAL_DIM
GATE_RANK = 16
GATE_TAU = 16.0
D_FF = 4 * D_MODEL

LANES = 128
Q_BLOCK = 2 * CHUNK
K_WINDOW = PREV + Q_BLOCK
N_WIN_TILES = K_WINDOW // 128
VT_ROWS = 128 + 16
ATTN_PAIRS = 2
STAGE_LAG = 3
N_SLOTS = STAGE_LAG + 1
ZERO_BIAS_TILES = tuple(range(1, (PREV - REL_CLIP) // 128))
LOG2E = 1.4426950408889634
SLAB_ROWS = 256
GLA_GROUP = 16
MERGE_TM = 1024
MLP_TM = 1024
NEAR_START = PREV - REL_CLIP
NEAR_SEG = K_WINDOW - NEAR_START + Q_BLOCK
VMEM_LIMIT = 56 * 1024 * 1024

F32 = jnp.float32
BF16 = jnp.bfloat16
NEG = -1e30


def _rmsnorm_bf16(x, g):
    ms = jnp.mean(x * x, axis=-1, keepdims=True)
    return ((x * lax.rsqrt(ms + EPS)) * g).astype(BF16)


def _sigmoid(x):
    return 1.0 / (1.0 + jnp.exp(-x))


def _dot(a, b):
    return jnp.dot(a, b, preferred_element_type=F32)


def _dot_nt(a, b):
    return lax.dot_general(a, b, (((1,), (1,)), ((), ())), preferred_element_type=F32)


def _const_spec(shape):
    nd = len(shape)
    return pl.BlockSpec(shape, lambda *_: (0,) * nd)


def _layer_spec(stacked, layer):
    rest = stacked.shape[1:]
    return pl.BlockSpec((None,) + rest, lambda *_: (layer,) + (0,) * len(rest),
                        pipeline_mode=pl.Buffered(1))


def _bias_kernel(t_ref, o_ref):
    near_w = K_WINDOW - NEAR_START
    r = lax.broadcasted_iota(jnp.int32, (Q_BLOCK, near_w), 0)
    c = lax.broadcasted_iota(jnp.int32, (Q_BLOCK, near_w), 1)
    kr = lax.broadcasted_iota(jnp.int32, (NEAR_START, Q_BLOCK), 0)
    qc = lax.broadcasted_iota(jnp.int32, (NEAR_START, Q_BLOCK), 1)
    for h in range(t_ref.shape[0]):
        lanes = slice((h % 2) * Q_BLOCK, (h % 2 + 1) * Q_BLOCK)
        t = t_ref[h]
        t = (t - t[:, 0:1]) * LOG2E
        x = jnp.broadcast_to(t, (Q_BLOCK, NEAR_SEG))
        x = pltpu.roll(x, 0, 1, stride=1, stride_axis=0)
        near = x[:, Q_BLOCK:]
        near = jnp.where((r < CHUNK) & (c >= near_w - CHUNK), NEG, near)
        for j in range(near_w // LANES):
            o_ref[h // 2, NEAR_START + j * LANES:NEAR_START + (j + 1) * LANES, lanes] = (
                near[:, j * LANES:(j + 1) * LANES].T)
        o_ref[h // 2, :NEAR_START, lanes] = jnp.where((kr < CHUNK) & (qc >= CHUNK), NEG, 0.0)


def _bias_tables(rel_bias):
    far = rel_bias[..., N_REL - 1:]
    n_far = NEAR_SEG - (N_REL - 2)
    seg = jnp.concatenate([
        jnp.broadcast_to(far, rel_bias.shape[:-1] + (n_far,)),
        rel_bias[..., N_REL - 2:0:-1],
    ], axis=-1)
    seg = seg.reshape(DEPTH * A_HEADS, 1, NEAR_SEG)
    n_pair = A_HEADS // 2
    return pl.pallas_call(
        _bias_kernel,
        grid=(DEPTH,),
        in_specs=[pl.BlockSpec((A_HEADS, 1, NEAR_SEG), lambda l: (l, 0, 0))],
        out_specs=pl.BlockSpec((n_pair, K_WINDOW, 2 * Q_BLOCK), lambda l: (l, 0, 0)),
        out_shape=jax.ShapeDtypeStruct((DEPTH * n_pair, K_WINDOW, 2 * Q_BLOCK), F32),
        name="bias_table",
    )(seg)


def _inproj_kernel(res_ref, g_ref, wa_ref, wb_ref, wlr_ref, wg_ref, bg_ref,
                   qkva_ref, gla_ref, gate_ref, decay_ref, vt_ref):
    tm = res_ref.shape[0]
    n_chunk = tm // CHUNK
    kw = B_KEY_WIDTH
    even = (lax.broadcasted_iota(jnp.int32, (1, kw), 1) // B_KEY_DIM) % 2 == 0
    h = _rmsnorm_bf16(res_ref[...], g_ref[...])
    lr = _dot_nt(h, wlr_ref[...])
    z = _dot(lr.astype(BF16), wg_ref[...]) + bg_ref[...]
    log_sig = jnp.minimum(z, 0.0) - jnp.log(1.0 + jnp.exp(-jnp.abs(z)))
    b = _chunk_cumsum(log_sig / GATE_TAU)
    b3 = b.reshape(n_chunk, CHUNK, kw)
    b_last = b3[:, CHUNK - 1:CHUNK, :]
    decay_ref[...] = jnp.exp(b_last).reshape(n_chunk, kw)
    to_end = jnp.exp(b_last - b3).reshape(tm, kw)
    pb = _dot_nt(h, wb_ref[...])
    qt = pb[:, :kw] * (B_KEY_DIM ** -0.5) * jnp.exp(b)
    k = pb[:, kw:2 * kw]
    gla_ref[:, :kw] = jnp.where(even, qt, 0.0).astype(BF16)
    gla_ref[:, kw:2 * kw] = jnp.where(even, 0.0, qt).astype(BF16)
    gla_ref[:, 2 * kw:3 * kw] = (k * jnp.exp(-b)).astype(BF16)
    gla_ref[:, 3 * kw:4 * kw] = (k * to_end).astype(BF16)
    v = pb[:, 2 * kw:2 * kw + B_WIDTH]
    gla_ref[:, 4 * kw:] = v.astype(BF16)
    vt_ref[...] = v.T.astype(BF16)
    r = pb[:, 2 * kw + B_WIDTH:]
    gate_ref[...] = r * _sigmoid(r)
    qkva_ref[...] = _dot_nt(h, wa_ref[...]).astype(BF16)


def _inproj(res, g, w_t, wg, bg, tm, layer):
    t = res.shape[0]
    gla_cols = 4 * B_KEY_WIDTH + B_WIDTH
    half_rows = 3 * A_WIDTH
    assert half_rows == 2 * B_KEY_WIDTH + 2 * B_WIDTH
    half = (None, half_rows, D_MODEL)
    lr_block = 2 * half_rows // GATE_RANK
    return pl.pallas_call(
        _inproj_kernel,
        grid=(t // tm,),
        in_specs=[
            pl.BlockSpec((tm, D_MODEL), lambda i: (i, 0)),
            _const_spec((1, D_MODEL)),
            pl.BlockSpec(half, lambda i: (layer, 0, 0)),
            pl.BlockSpec(half, lambda i: (layer, 1, 0)),
            pl.BlockSpec((None, GATE_RANK, D_MODEL), lambda i: (layer, lr_block, 0)),
            _layer_spec(wg, layer),
            _const_spec((1, B_KEY_WIDTH)),
        ],
        out_specs=[
            pl.BlockSpec((tm, 3 * A_WIDTH), lambda i: (i, 0)),
            pl.BlockSpec((tm, gla_cols), lambda i: (i, 0)),
            pl.BlockSpec((tm, B_WIDTH), lambda i: (i, 0)),
            pl.BlockSpec((tm // CHUNK, B_KEY_WIDTH), lambda i: (i, 0)),
            pl.BlockSpec((None, B_WIDTH, tm), lambda i: (i, 0, 0)),
        ],
        out_shape=[
            jax.ShapeDtypeStruct((t, 3 * A_WIDTH), BF16),
            jax.ShapeDtypeStruct((t, gla_cols), BF16),
            jax.ShapeDtypeStruct((t, B_WIDTH), F32),
            jax.ShapeDtypeStruct((t // CHUNK, B_KEY_WIDTH), F32),
            jax.ShapeDtypeStruct((t // tm, B_WIDTH, tm), BF16),
        ],
        compiler_params=pltpu.CompilerParams(
            dimension_semantics=("parallel",), vmem_limit_bytes=VMEM_LIMIT),
        name="inproj",
    )(res, g, w_t, w_t, w_t, wg, bg)


def _attn_kernel(q_ref, k_ref, v_ref, bias_ref, o_ref, vt_scr, s_scr, m_scr, e_scr):
    seq = q_ref.shape[0]
    n_blk = seq // Q_BLOCK
    n_prev = PREV // Q_BLOCK
    lo = lax.broadcasted_iota(jnp.int32, (1, LANES), 1) < A_HEAD_DIM

    def lanes(pp):
        return slice(pp * LANES, (pp + 1) * LANES)

    for pp in range(ATTN_PAIRS):
        for r in range(n_blk):
            vt_scr[pp, r, :LANES, :] = (
                v_ref[r * LANES:(r + 1) * LANES, lanes(pp)].astype(F32).T.astype(BF16))
            vt_scr[pp, r, LANES:, :] = jnp.ones((VT_ROWS - LANES, LANES), BF16)

    def window(t):
        return max(t - n_prev, 0), min(t + 1, N_WIN_TILES)

    def scores(pp, t):
        kt, n_tiles = window(t)
        rows = n_tiles * LANES
        q2 = (q_ref[t * Q_BLOCK:(t + 1) * Q_BLOCK, lanes(pp)].astype(F32)
              * (A_HEAD_DIM ** -0.5 * LOG2E))
        q4 = jnp.concatenate(
            [jnp.where(lo, q2, 0.0), jnp.where(lo, 0.0, q2)], axis=0).astype(BF16)
        kw = k_ref[kt * LANES:kt * LANES + rows, lanes(pp)]
        s = _dot_nt(kw, q4)
        m = None
        for i in range(n_tiles):
            tile = s[i * LANES:(i + 1) * LANES, :]
            w = N_WIN_TILES - n_tiles + i
            if w not in ZERO_BIAS_TILES:
                tile = tile + bias_ref[pp, w * LANES:(w + 1) * LANES, :]
            s_scr[pp, t % N_SLOTS, i * LANES:(i + 1) * LANES, :] = tile
            tile_max = jnp.max(tile.reshape(LANES // 8, 8, 2 * Q_BLOCK), axis=0)
            m = tile_max if m is None else jnp.maximum(m, tile_max)
        m_scr[pp, t % N_SLOTS] = jnp.max(m, axis=0, keepdims=True)

    def exps(pp, t):
        rows = window(t)[1] * LANES
        d = (s_scr[pp, t % N_SLOTS, :rows, :] - m_scr[pp, t % N_SLOTS]).astype(BF16)
        e_scr[pp, t % N_SLOTS, :rows, :] = jnp.exp2(d)

    def weighted_sum(pp, t):
        kt, n_tiles = window(t)
        vt = jnp.concatenate([vt_scr[pp, kt + i] for i in range(n_tiles)], axis=1)
        ot = _dot(vt, e_scr[pp, t % N_SLOTS, :n_tiles * LANES, :])
        inv = 1.0 / ot[LANES:LANES + 1, :]
        o_pair = jnp.concatenate(
            [ot[:A_HEAD_DIM, :Q_BLOCK] * inv[:, :Q_BLOCK],
             ot[A_HEAD_DIM:LANES, Q_BLOCK:] * inv[:, Q_BLOCK:]], axis=0)
        o_ref[t * Q_BLOCK:(t + 1) * Q_BLOCK, lanes(pp)] = o_pair.T.astype(BF16)

    for step in range(n_blk + 2 * STAGE_LAG):
        for pp in range(ATTN_PAIRS):
            if step < n_blk:
                scores(pp, step)
            if 0 <= step - STAGE_LAG < n_blk:
                exps(pp, step - STAGE_LAG)
            if 0 <= step - 2 * STAGE_LAG < n_blk:
                weighted_sum(pp, step - 2 * STAGE_LAG)


def _attention(qkva, bias, layer):
    bsz, seq, _ = qkva.shape
    n_grp = A_HEADS // 2 // ATTN_PAIRS
    blk = (None, seq, ATTN_PAIRS * LANES)
    return pl.pallas_call(
        _attn_kernel,
        grid=(bsz, n_grp),
        in_specs=[
            pl.BlockSpec(blk, lambda b, p: (b, 0, p)),
            pl.BlockSpec(blk, lambda b, p: (b, 0, n_grp + p)),
            pl.BlockSpec(blk, lambda b, p: (b, 0, 2 * n_grp + p)),
            pl.BlockSpec((ATTN_PAIRS, K_WINDOW, 2 * Q_BLOCK),
                         lambda b, p: (layer * n_grp + p, 0, 0)),
        ],
        out_specs=pl.BlockSpec(blk, lambda b, p: (b, 0, p)),
        out_shape=jax.ShapeDtypeStruct((bsz, seq, A_WIDTH), BF16),
        scratch_shapes=[pltpu.VMEM((ATTN_PAIRS, seq // LANES, VT_ROWS, LANES), BF16),
                        pltpu.VMEM((ATTN_PAIRS, N_SLOTS, K_WINDOW, 2 * Q_BLOCK), F32),
                        pltpu.VMEM((ATTN_PAIRS, N_SLOTS, 1, 2 * Q_BLOCK), F32),
                        pltpu.VMEM((ATTN_PAIRS, N_SLOTS, K_WINDOW, 2 * Q_BLOCK), BF16)],
        compiler_params=pltpu.CompilerParams(
            dimension_semantics=("parallel", "parallel"), vmem_limit_bytes=VMEM_LIMIT),
        name="band_attention",
    )(qkva, qkva, qkva, bias)


def _chunk_cumsum(x):
    row = lax.broadcasted_iota(jnp.int32, x.shape, 0) % CHUNK
    d = 1
    while d < CHUNK:
        x = x + jnp.where(row >= d, pltpu.roll(x, d, axis=0), 0.0)
        d *= 2
    return x


def _gla_kernel(qe_ref, qo_ref, kt_ref, kend_ref, v_ref, vt_ref, gate_ref, decay_ref, g_ref,
                o_ref):
    seq = qe_ref.shape[0]
    n_pair = B_HEADS // 2
    lo = lax.broadcasted_iota(jnp.int32, (1, LANES), 1) < B_KEY_DIM
    ri = lax.broadcasted_iota(jnp.int32, (2 * CHUNK, CHUNK), 0)
    ci = lax.broadcasted_iota(jnp.int32, (2 * CHUNK, CHUNK), 1)
    causal = (ri % CHUNK) >= ci
    group = GLA_GROUP * CHUNK

    def body(i, states):
        rows = pl.ds(pl.multiple_of(i * group, group), group)
        q_even = qe_ref[rows, :]
        q_odd = qo_ref[rows, :]
        kt = kt_ref[rows, :]
        kend = kend_ref[rows, :]
        v = v_ref[rows, :]
        decay = decay_ref[pl.ds(pl.multiple_of(i * GLA_GROUP, GLA_GROUP), GLA_GROUP), :]
        vt = vt_ref[i]
        pad = jnp.zeros((CHUNK, LANES), BF16)
        states = list(states)
        outs = []
        for c in range(GLA_GROUP):
            sl = slice(c * CHUNK, (c + 1) * CHUNK)
            pair_cols = slice((c // 2) * LANES, (c // 2 + 1) * LANES)
            o_c = []
            for p in range(n_pair):
                kl = slice(p * LANES, (p + 1) * LANES)
                ve = slice(2 * p * B_VAL_DIM, (2 * p + 1) * B_VAL_DIM)
                vo = slice((2 * p + 1) * B_VAL_DIM, (2 * p + 2) * B_VAL_DIM)
                st = states[p]
                q4 = jnp.concatenate([q_even[sl, kl], q_odd[sl, kl]], axis=0)
                att = jnp.where(causal, _dot_nt(q4, kt[sl, kl]), 0.0).astype(BF16)
                inter = _dot_nt(q4, st.astype(BF16))
                o_c += [_dot(att[:CHUNK], v[sl, ve]) + inter[:CHUNK],
                        _dot(att[CHUNK:], v[sl, vo]) + inter[CHUNK:]]
                kend_c = [kend[sl, kl], pad] if c % 2 == 0 else [pad, kend[sl, kl]]
                d_full = _dot(vt[2 * p * B_VAL_DIM:(2 * p + 2) * B_VAL_DIM, pair_cols],
                              jnp.concatenate(kend_c, axis=0))
                d_st = jnp.where(lo, d_full[:B_VAL_DIM], d_full[B_VAL_DIM:])
                states[p] = st * decay[c:c + 1, kl] + d_st
            outs.append(jnp.concatenate(o_c, axis=1))
        o = jnp.concatenate(outs, axis=0)
        gate = gate_ref[rows, :]
        for h in range(B_HEADS):
            hl = slice(h * B_VAL_DIM, (h + 1) * B_VAL_DIM)
            oh = o[:, hl]
            oh = oh * lax.rsqrt(jnp.mean(oh * oh, axis=-1, keepdims=True) + EPS) * g_ref[:, hl]
            o_ref[rows, hl] = (oh * gate[:, hl]).astype(BF16)
        return tuple(states)

    zero = jnp.zeros((B_VAL_DIM, LANES), F32)
    lax.fori_loop(0, seq // group, body, (zero,) * n_pair)


def _gla(gq, vt, gate, decay, g):
    bsz, seq, _ = gq.shape
    kblk = (None, seq, B_KEY_WIDTH)
    vblk = (None, seq, B_WIDTH)
    return pl.pallas_call(
        _gla_kernel,
        grid=(bsz,),
        in_specs=[pl.BlockSpec(kblk, lambda b, j=j: (b, 0, j)) for j in range(4)] + [
            pl.BlockSpec(vblk, lambda b: (b, 0, 4 * B_KEY_WIDTH // B_WIDTH)),
            pl.BlockSpec((None,) + vt.shape[1:], lambda b: (b, 0, 0, 0)),
            pl.BlockSpec(vblk, lambda b: (b, 0, 0)),
            pl.BlockSpec((None, seq // CHUNK, B_KEY_WIDTH), lambda b: (b, 0, 0)),
            _const_spec((1, B_WIDTH)),
        ],
        out_specs=pl.BlockSpec(vblk, lambda b: (b, 0, 0)),
        out_shape=jax.ShapeDtypeStruct((bsz, seq, B_WIDTH), BF16),
        compiler_params=pltpu.CompilerParams(
            dimension_semantics=("parallel",), vmem_limit_bytes=VMEM_LIMIT),
        name="gla",
    )(gq, gq, gq, gq, gq, vt, gate, decay, g)


def _merge_kernel(res_ref, g_ref, ya_ref, yb_ref, wgate_ref, wbr_ref, wout_ref, o_ref):
    for r in range(res_ref.shape[0] // SLAB_ROWS):
        rows = slice(r * SLAB_ROWS, (r + 1) * SLAB_ROWS)
        res = res_ref[rows, :]
        h = _rmsnorm_bf16(res, g_ref[...])
        gates = _dot_nt(h, wgate_ref[0])
        u_a = _dot(ya_ref[rows, :], wbr_ref[0])
        u_b = _dot(yb_ref[rows, :], wbr_ref[1])
        merged = _sigmoid(gates[:, :D_MODEL]) * u_a + _sigmoid(gates[:, D_MODEL:]) * u_b
        o_ref[rows, :] = res + _dot(merged.astype(BF16), wout_ref[...])


def _merge(res, g, ya, yb, w_t, gate_row0, wbr, wout, tm, layer):
    t = res.shape[0]
    return pl.pallas_call(
        _merge_kernel,
        grid=(t // tm,),
        in_specs=[
            pl.BlockSpec((tm, D_MODEL), lambda i: (i, 0)),
            _const_spec((1, D_MODEL)),
            pl.BlockSpec((tm, A_WIDTH), lambda i: (i, 0)),
            pl.BlockSpec((tm, B_WIDTH), lambda i: (i, 0)),
            pl.BlockSpec((pl.Element(1), pl.Element(2 * D_MODEL), pl.Element(D_MODEL)),
                         lambda i: (layer, gate_row0, 0), pipeline_mode=pl.Buffered(1)),
            _layer_spec(wbr, layer),
            _layer_spec(wout, layer),
        ],
        out_specs=pl.BlockSpec((tm, D_MODEL), lambda i: (i, 0)),
        out_shape=jax.ShapeDtypeStruct((t, D_MODEL), F32),
        compiler_params=pltpu.CompilerParams(
            dimension_semantics=("parallel",), vmem_limit_bytes=VMEM_LIMIT),
        name="merge",
    )(res, g, ya, yb, w_t, wbr, wout)


def _mlp_kernel(res_ref, g_ref, wup_ref, wdn_ref, gf_ref, o_ref, *, final_norm):
    for r in range(res_ref.shape[0] // SLAB_ROWS):
        rows = slice(r * SLAB_ROWS, (r + 1) * SLAB_ROWS)
        res = res_ref[rows, :]
        h = _rmsnorm_bf16(res, g_ref[...])
        acc = res
        for c in range(D_FF // D_MODEL):
            cols = slice(c * D_MODEL, (c + 1) * D_MODEL)
            up = jnp.maximum(_dot(h, wup_ref[:, cols]), 0.0)
            acc = acc + _dot((up * up).astype(BF16), wdn_ref[cols, :])
        if final_norm:
            ms = jnp.mean(acc * acc, axis=-1, keepdims=True)
            acc = (acc * lax.rsqrt(ms + EPS)) * gf_ref[...]
        o_ref[rows, :] = acc


def _mlp(res, g, wup, wdn, gf, tm, layer, final_norm):
    t = res.shape[0]
    return pl.pallas_call(
        functools.partial(_mlp_kernel, final_norm=final_norm),
        grid=(t // tm,),
        in_specs=[
            pl.BlockSpec((tm, D_MODEL), lambda i: (i, 0)),
            _const_spec((1, D_MODEL)),
            _layer_spec(wup, layer),
            _layer_spec(wdn, layer),
            _const_spec((1, D_MODEL)),
        ],
        out_specs=pl.BlockSpec((tm, D_MODEL), lambda i: (i, 0)),
        out_shape=jax.ShapeDtypeStruct((t, D_MODEL), F32),
        compiler_params=pltpu.CompilerParams(
            dimension_semantics=("parallel",), vmem_limit_bytes=VMEM_LIMIT),
        name="mlp",
    )(res, g, wup, wdn, gf)


def kernel(x, mix_norm_g, w_in, rel_bias, w_gate_lr, b_gate, gla_norm_g, w_branch, w_out,
           mlp_norm_g, w_up, w_down, final_norm_g):
    bsz, seq, d = x.shape
    t = bsz * seq
    tm = GLA_GROUP * CHUNK
    ab_cols = 3 * A_WIDTH + 2 * B_KEY_WIDTH + 2 * B_WIDTH
    w_in_t = jnp.swapaxes(w_in, 1, 2).astype(BF16)
    w_glr = w_gate_lr.astype(BF16)
    w_br = w_branch.astype(BF16)
    w_o = w_out.astype(BF16)
    w_u = w_up.astype(BF16)
    w_d = w_down.astype(BF16)
    bias = _bias_tables(rel_bias)

    res = x.reshape(t, d)
    for l in range(DEPTH):
        qkva, gq, gate, decay, vt = _inproj(res, mix_norm_g[l][None], w_in_t, w_glr,
                                            b_gate[l][None], tm, l)
        ya = _attention(qkva.reshape(bsz, seq, -1), bias, l)
        yb = _gla(gq.reshape(bsz, seq, -1), vt.reshape(bsz, seq // tm, B_WIDTH, tm),
                  gate.reshape(bsz, seq, -1), decay.reshape(bsz, seq // CHUNK, -1),
                  gla_norm_g[l][None])
        res = _merge(res, mix_norm_g[l][None], ya.reshape(t, -1), yb.reshape(t, -1),
                     w_in_t, ab_cols + GATE_RANK, w_br, w_o, MERGE_TM, l)
        res = _mlp(res, mlp_norm_g[l][None], w_u, w_d, final_norm_g[None], MLP_TM, l,
                   final_norm=(l == DEPTH - 1))
    return res.reshape(bsz, seq, d)
```

```python
import functools

import jax
import jax.numpy as jnp
from jax import lax
from jax.experimental import pallas as pl
from jax.experimental.pallas import tpu as pltpu

D_MODEL = 1024
DEPTH = 4
CHUNK = 64
EPS = 1e-6

A_HEADS = 8
A_HEAD_DIM = 64
A_WIDTH = A_HEADS * A_HEAD_DIM
N_PREV_CHUNKS = 8
PREV = N_PREV_CHUNKS * CHUNK
REL_CLIP = 128
N_REL = 2 * REL_CLIP + 1

B_HEADS = 4
B_KEY_DIM = 64
B_VAL_DIM = 128
B_KEY_WIDTH = B_HEADS * B_KEY_DIM
B_WIDTH = B_HEADS * B_VAL_DIM
GATE_RANK = 16
GATE_TAU = 16.0
D_FF = 4 * D_MODEL

LANES = 128
Q_BLOCK = 2 * CHUNK
K_WINDOW = PREV + Q_BLOCK
N_WIN_TILES = K_WINDOW // 128
VT_ROWS = 128 + 16
ATTN_PAIRS = 2
STAGE_LAG = 3
N_SLOTS = STAGE_LAG + 1
ZERO_BIAS_TILES = tuple(range(1, (PREV - REL_CLIP) // 128))
LOG2E = 1.4426950408889634
SLAB_ROWS = 256
GLA_GROUP = 16
MERGE_TM = 1024
MLP_TM = 1024
NEAR_START = PREV - REL_CLIP
NEAR_SEG = K_WINDOW - NEAR_START + Q_BLOCK
VMEM_LIMIT = 56 * 1024 * 1024

F32 = jnp.float32
BF16 = jnp.bfloat16
NEG = -1e30


def _rmsnorm_bf16(x, g):
    ms = jnp.mean(x * x, axis=-1, keepdims=True)
    return ((x * lax.rsqrt(ms + EPS)) * g).astype(BF16)


def _sigmoid(x):
    return 1.0 / (1.0 + jnp.exp(-x))


def _dot(a, b):
    return jnp.dot(a, b, preferred_element_type=F32)


def _dot_nt(a, b):
    return lax.dot_general(a, b, (((1,), (1,)), ((), ())), preferred_element_type=F32)


def _const_spec(shape):
    nd = len(shape)
    return pl.BlockSpec(shape, lambda *_: (0,) * nd)


def _layer_spec(stacked, layer):
    rest = stacked.shape[1:]
    return pl.BlockSpec((None,) + rest, lambda *_: (layer,) + (0,) * len(rest),
                        pipeline_mode=pl.Buffered(1))


def _bias_kernel(t_ref, o_ref):
    near_w = K_WINDOW - NEAR_START
    r = lax.broadcasted_iota(jnp.int32, (Q_BLOCK, near_w), 0)
    c = lax.broadcasted_iota(jnp.int32, (Q_BLOCK, near_w), 1)
    kr = lax.broadcasted_iota(jnp.int32, (NEAR_START, Q_BLOCK), 0)
    qc = lax.broadcasted_iota(jnp.int32, (NEAR_START, Q_BLOCK), 1)
    for h in range(t_ref.shape[0]):
        lanes = slice((h % 2) * Q_BLOCK, (h % 2 + 1) * Q_BLOCK)
        t = t_ref[h]
        t = (t - t[:, 0:1]) * LOG2E
        x = jnp.broadcast_to(t, (Q_BLOCK, NEAR_SEG))
        x = pltpu.roll(x, 0, 1, stride=1, stride_axis=0)
        near = x[:, Q_BLOCK:]
        near = jnp.where((r < CHUNK) & (c >= near_w - CHUNK), NEG, near)
        for j in range(near_w // LANES):
            o_ref[h // 2, NEAR_START + j * LANES:NEAR_START + (j + 1) * LANES, lanes] = (
                near[:, j * LANES:(j + 1) * LANES].T)
        o_ref[h // 2, :NEAR_START, lanes] = jnp.where((kr < CHUNK) & (qc >= CHUNK), NEG, 0.0)


def _bias_tables(rel_bias):
    far = rel_bias[..., N_REL - 1:]
    n_far = NEAR_SEG - (N_REL - 2)
    seg = jnp.concatenate([
        jnp.broadcast_to(far, rel_bias.shape[:-1] + (n_far,)),
        rel_bias[..., N_REL - 2:0:-1],
    ], axis=-1)
    seg = seg.reshape(DEPTH * A_HEADS, 1, NEAR_SEG)
    n_pair = A_HEADS // 2
    return pl.pallas_call(
        _bias_kernel,
        grid=(DEPTH,),
        in_specs=[pl.BlockSpec((A_HEADS, 1, NEAR_SEG), lambda l: (l, 0, 0))],
        out_specs=pl.BlockSpec((n_pair, K_WINDOW, 2 * Q_BLOCK), lambda l: (l, 0, 0)),
        out_shape=jax.ShapeDtypeStruct((DEPTH * n_pair, K_WINDOW, 2 * Q_BLOCK), F32),
        name="bias_table",
    )(seg)


def _inproj_kernel(res_ref, g_ref, wa_ref, wb_ref, wlr_ref, wg_ref, bg_ref,
                   qkva_ref, gla_ref, gate_ref, decay_ref, vt_ref):
    tm = res_ref.shape[0]
    n_chunk = tm // CHUNK
    kw = B_KEY_WIDTH
    even = (lax.broadcasted_iota(jnp.int32, (1, kw), 1) // B_KEY_DIM) % 2 == 0
    h = _rmsnorm_bf16(res_ref[...], g_ref[...])
    lr = _dot_nt(h, wlr_ref[...])
    z = _dot(lr.astype(BF16), wg_ref[...]) + bg_ref[...]
    log_sig = jnp.minimum(z, 0.0) - jnp.log(1.0 + jnp.exp(-jnp.abs(z)))
    b = _chunk_cumsum(log_sig / GATE_TAU)
    b3 = b.reshape(n_chunk, CHUNK, kw)
    b_last = b3[:, CHUNK - 1:CHUNK, :]
    decay_ref[...] = jnp.exp(b_last).reshape(n_chunk, kw)
    to_end = jnp.exp(b_last - b3).reshape(tm, kw)
    pb = _dot_nt(h, wb_ref[...])
    qt = pb[:, :kw] * (B_KEY_DIM ** -0.5) * jnp.exp(b)
    k = pb[:, kw:2 * kw]
    gla_ref[:, :kw] = jnp.where(even, qt, 0.0).astype(BF16)
    gla_ref[:, kw:2 * kw] = jnp.where(even, 0.0, qt).astype(BF16)
    gla_ref[:, 2 * kw:3 * kw] = (k * jnp.exp(-b)).astype(BF16)
    gla_ref[:, 3 * kw:4 * kw] = (k * to_end).astype(BF16)
    v = pb[:, 2 * kw:2 * kw + B_WIDTH]
    gla_ref[:, 4 * kw:] = v.astype(BF16)
    vt_ref[...] = v.T.astype(BF16)
    r = pb[:, 2 * kw + B_WIDTH:]
    gate_ref[...] = r * _sigmoid(r)
    qkva_ref[...] = _dot_nt(h, wa_ref[...]).astype(BF16)


def _inproj(res, g, w_t, wg, bg, tm, layer):
    t = res.shape[0]
    gla_cols = 4 * B_KEY_WIDTH + B_WIDTH
    half_rows = 3 * A_WIDTH
    assert half_rows == 2 * B_KEY_WIDTH + 2 * B_WIDTH
    half = (None, half_rows, D_MODEL)
    lr_block = 2 * half_rows // GATE_RANK
    return pl.pallas_call(
        _inproj_kernel,
        grid=(t // tm,),
        in_specs=[
            pl.BlockSpec((tm, D_MODEL), lambda i: (i, 0)),
            _const_spec((1, D_MODEL)),
            pl.BlockSpec(half, lambda i: (layer, 0, 0)),
            pl.BlockSpec(half, lambda i: (layer, 1, 0)),
            pl.BlockSpec((None, GATE_RANK, D_MODEL), lambda i: (layer, lr_block, 0)),
            _layer_spec(wg, layer),
            _const_spec((1, B_KEY_WIDTH)),
        ],
        out_specs=[
            pl.BlockSpec((tm, 3 * A_WIDTH), lambda i: (i, 0)),
            pl.BlockSpec((tm, gla_cols), lambda i: (i, 0)),
            pl.BlockSpec((tm, B_WIDTH), lambda i: (i, 0)),
            pl.BlockSpec((tm // CHUNK, B_KEY_WIDTH), lambda i: (i, 0)),
            pl.BlockSpec((None, B_WIDTH, tm), lambda i: (i, 0, 0)),
        ],
        out_shape=[
            jax.ShapeDtypeStruct((t, 3 * A_WIDTH), BF16),
            jax.ShapeDtypeStruct((t, gla_cols), BF16),
            jax.ShapeDtypeStruct((t, B_WIDTH), F32),
            jax.ShapeDtypeStruct((t // CHUNK, B_KEY_WIDTH), F32),
            jax.ShapeDtypeStruct((t // tm, B_WIDTH, tm), BF16),
        ],
        compiler_params=pltpu.CompilerParams(
            dimension_semantics=("parallel",), vmem_limit_bytes=VMEM_LIMIT),
        name="inproj",
    )(res, g, w_t, w_t, w_t, wg, bg)


def _attn_kernel(q_ref, k_ref, v_ref, bias_ref, o_ref, vt_scr, s_scr, m_scr, e_scr):
    seq = q_ref.shape[0]
    n_blk = seq // Q_BLOCK
    n_prev = PREV // Q_BLOCK
    lo = lax.broadcasted_iota(jnp.int32, (1, LANES), 1) < A_HEAD_DIM

    def lanes(pp):
        return slice(pp * LANES, (pp + 1) * LANES)

    for pp in range(ATTN_PAIRS):
        for r in range(n_blk):
            vt_scr[pp, r, :LANES, :] = (
                v_ref[r * LANES:(r + 1) * LANES, lanes(pp)].astype(F32).T.astype(BF16))
            vt_scr[pp, r, LANES:, :] = jnp.ones((VT_ROWS - LANES, LANES), BF16)

    def window(t):
        return max(t - n_prev, 0), min(t + 1, N_WIN_TILES)

    def scores(pp, t):
        kt, n_tiles = window(t)
        rows = n_tiles * LANES
        q2 = (q_ref[t * Q_BLOCK:(t + 1) * Q_BLOCK, lanes(pp)].astype(F32)
              * (A_HEAD_DIM ** -0.5 * LOG2E))
        q4 = jnp.concatenate(
            [jnp.where(lo, q2, 0.0), jnp.where(lo, 0.0, q2)], axis=0).astype(BF16)
        kw = k_ref[kt * LANES:kt * LANES + rows, lanes(pp)]
        s = _dot_nt(kw, q4)
        m = None
        for i in range(n_tiles):
            tile = s[i * LANES:(i + 1) * LANES, :]
            w = N_WIN_TILES - n_tiles + i
            if w not in ZERO_BIAS_TILES:
                tile = tile + bias_ref[pp, w * LANES:(w + 1) * LANES, :]
            s_scr[pp, t % N_SLOTS, i * LANES:(i + 1) * LANES, :] = tile
            tile_max = jnp.max(tile.reshape(LANES // 8, 8, 2 * Q_BLOCK), axis=0)
            m = tile_max if m is None else jnp.maximum(m, tile_max)
        m_scr[pp, t % N_SLOTS] = jnp.max(m, axis=0, keepdims=True)

    def exps(pp, t):
        rows = window(t)[1] * LANES
        d = (s_scr[pp, t % N_SLOTS, :rows, :] - m_scr[pp, t % N_SLOTS]).astype(BF16)
        e_scr[pp, t % N_SLOTS, :rows, :] = jnp.exp2(d)

    def weighted_sum(pp, t):
        kt, n_tiles = window(t)
        vt = jnp.concatenate([vt_scr[pp, kt + i] for i in range(n_tiles)], axis=1)
        ot = _dot(vt, e_scr[pp, t % N_SLOTS, :n_tiles * LANES, :])
        inv = 1.0 / ot[LANES:LANES + 1, :]
        o_pair = jnp.concatenate(
            [ot[:A_HEAD_DIM, :Q_BLOCK] * inv[:, :Q_BLOCK],
             ot[A_HEAD_DIM:LANES, Q_BLOCK:] * inv[:, Q_BLOCK:]], axis=0)
        o_ref[t * Q_BLOCK:(t + 1) * Q_BLOCK, lanes(pp)] = o_pair.T.astype(BF16)

    for step in range(n_blk + 2 * STAGE_LAG):
        for pp in range(ATTN_PAIRS):
            if step < n_blk:
                scores(pp, step)
            if 0 <= step - STAGE_LAG < n_blk:
                exps(pp, step - STAGE_LAG)
            if 0 <= step - 2 * STAGE_LAG < n_blk:
                weighted_sum(pp, step - 2 * STAGE_LAG)


def _attention(qkva, bias, layer):
    bsz, seq, _ = qkva.shape
    n_grp = A_HEADS // 2 // ATTN_PAIRS
    blk = (None, seq, ATTN_PAIRS * LANES)
    return pl.pallas_call(
        _attn_kernel,
        grid=(bsz, n_grp),
        in_specs=[
            pl.BlockSpec(blk, lambda b, p: (b, 0, p)),
            pl.BlockSpec(blk, lambda b, p: (b, 0, n_grp + p)),
            pl.BlockSpec(blk, lambda b, p: (b, 0, 2 * n_grp + p)),
            pl.BlockSpec((ATTN_PAIRS, K_WINDOW, 2 * Q_BLOCK),
                         lambda b, p: (layer * n_grp + p, 0, 0)),
        ],
        out_specs=pl.BlockSpec(blk, lambda b, p: (b, 0, p)),
        out_shape=jax.ShapeDtypeStruct((bsz, seq, A_WIDTH), BF16),
        scratch_shapes=[pltpu.VMEM((ATTN_PAIRS, seq // LANES, VT_ROWS, LANES), BF16),
                        pltpu.VMEM((ATTN_PAIRS, N_SLOTS, K_WINDOW, 2 * Q_BLOCK), F32),
                        pltpu.VMEM((ATTN_PAIRS, N_SLOTS, 1, 2 * Q_BLOCK), F32),
                        pltpu.VMEM((ATTN_PAIRS, N_SLOTS, K_WINDOW, 2 * Q_BLOCK), BF16)],
        compiler_params=pltpu.CompilerParams(
            dimension_semantics=("parallel", "parallel"), vmem_limit_bytes=VMEM_LIMIT),
        name="band_attention",
    )(qkva, qkva, qkva, bias)


def _chunk_cumsum(x):
    row = lax.broadcasted_iota(jnp.int32, x.shape, 0) % CHUNK
    d = 1
    while d < CHUNK:
        x = x + jnp.where(row >= d, pltpu.roll(x, d, axis=0), 0.0)
        d *= 2
    return x


def _gla_kernel(qe_ref, qo_ref, kt_ref, kend_ref, v_ref, vt_ref, gate_ref, decay_ref, g_ref,
                o_ref):
    seq = qe_ref.shape[0]
    n_pair = B_HEADS // 2
    lo = lax.broadcasted_iota(jnp.int32, (1, LANES), 1) < B_KEY_DIM
    ri = lax.broadcasted_iota(jnp.int32, (2 * CHUNK, CHUNK), 0)
    ci = lax.broadcasted_iota(jnp.int32, (2 * CHUNK, CHUNK), 1)
    causal = (ri % CHUNK) >= ci
    group = GLA_GROUP * CHUNK

    def body(i, states):
        rows = slice(i * group, (i + 1) * group)
        q_even = qe_ref[rows, :]
        q_odd = qo_ref[rows, :]
        kt = kt_ref[rows, :]
        kend = kend_ref[rows, :]
        v = v_ref[rows, :]
        decay = decay_ref[i * GLA_GROUP:(i + 1) * GLA_GROUP, :]
        vt = vt_ref[i]
        pad = jnp.zeros((CHUNK, LANES), BF16)
        states = list(states)
        outs = []
        for c in range(GLA_GROUP):
            sl = slice(c * CHUNK, (c + 1) * CHUNK)
            pair_cols = slice((c // 2) * LANES, (c // 2 + 1) * LANES)
            o_c = []
            for p in range(n_pair):
                kl = slice(p * LANES, (p + 1) * LANES)
                ve = slice(2 * p * B_VAL_DIM, (2 * p + 1) * B_VAL_DIM)
                vo = slice((2 * p + 1) * B_VAL_DIM, (2 * p + 2) * B_VAL_DIM)
                st = states[p]
                q4 = jnp.concatenate([q_even[sl, kl], q_odd[sl, kl]], axis=0)
                att = jnp.where(causal, _dot_nt(q4, kt[sl, kl]), 0.0).astype(BF16)
                inter = _dot_nt(q4, st.astype(BF16))
                o_c += [_dot(att[:CHUNK], v[sl, ve]) + inter[:CHUNK],
                        _dot(att[CHUNK:], v[sl, vo]) + inter[CHUNK:]]
                kend_c = [kend[sl, kl], pad] if c % 2 == 0 else [pad, kend[sl, kl]]
                d_full = _dot(vt[2 * p * B_VAL_DIM:(2 * p + 2) * B_VAL_DIM, pair_cols],
                              jnp.concatenate(kend_c, axis=0))
                d_st = jnp.where(lo, d_full[:B_VAL_DIM], d_full[B_VAL_DIM:])
                states[p] = st * decay[c:c + 1, kl] + d_st
            outs.append(jnp.concatenate(o_c, axis=1))
        o = jnp.concatenate(outs, axis=0)
        gate = gate_ref[rows, :]
        for h in range(B_HEADS):
            hl = slice(h * B_VAL_DIM, (h + 1) * B_VAL_DIM)
            oh = o[:, hl]
            oh = oh * lax.rsqrt(jnp.mean(oh * oh, axis=-1, keepdims=True) + EPS) * g_ref[:, hl]
            o_ref[rows, hl] = (oh * gate[:, hl]).astype(BF16)
        return tuple(states)

    states = (jnp.zeros((B_VAL_DIM, LANES), F32),) * n_pair
    for i in range(seq // group):
        states = body(i, states)


def _gla(gq, vt, gate, decay, g):
    bsz, seq, _ = gq.shape
    kblk = (None, seq, B_KEY_WIDTH)
    vblk = (None, seq, B_WIDTH)
    return pl.pallas_call(
        _gla_kernel,
        grid=(bsz,),
        in_specs=[pl.BlockSpec(kblk, lambda b, j=j: (b, 0, j)) for j in range(4)] + [
            pl.BlockSpec(vblk, lambda b: (b, 0, 4 * B_KEY_WIDTH // B_WIDTH)),
            pl.BlockSpec((None,) + vt.shape[1:], lambda b: (b, 0, 0, 0)),
            pl.BlockSpec(vblk, lambda b: (b, 0, 0)),
            pl.BlockSpec((None, seq // CHUNK, B_KEY_WIDTH), lambda b: (b, 0, 0)),
            _const_spec((1, B_WIDTH)),
        ],
        out_specs=pl.BlockSpec(vblk, lambda b: (b, 0, 0)),
        out_shape=jax.ShapeDtypeStruct((bsz, seq, B_WIDTH), BF16),
        compiler_params=pltpu.CompilerParams(
            dimension_semantics=("parallel",), vmem_limit_bytes=VMEM_LIMIT),
        name="gla",
    )(gq, gq, gq, gq, gq, vt, gate, decay, g)


def _merge_kernel(res_ref, g_ref, ya_ref, yb_ref, wgate_ref, wbr_ref, wout_ref, o_ref):
    for r in range(res_ref.shape[0] // SLAB_ROWS):
        rows = slice(r * SLAB_ROWS, (r + 1) * SLAB_ROWS)
        res = res_ref[rows, :]
        h = _rmsnorm_bf16(res, g_ref[...])
        gates = _dot_nt(h, wgate_ref[0])
        u_a = _dot(ya_ref[rows, :], wbr_ref[0])
        u_b = _dot(yb_ref[rows, :], wbr_ref[1])
        merged = _sigmoid(gates[:, :D_MODEL]) * u_a + _sigmoid(gates[:, D_MODEL:]) * u_b
        o_ref[rows, :] = res + _dot(merged.astype(BF16), wout_ref[...])


def _merge(res, g, ya, yb, w_t, gate_row0, wbr, wout, tm, layer):
    t = res.shape[0]
    return pl.pallas_call(
        _merge_kernel,
        grid=(t // tm,),
        in_specs=[
            pl.BlockSpec((tm, D_MODEL), lambda i: (i, 0)),
            _const_spec((1, D_MODEL)),
            pl.BlockSpec((tm, A_WIDTH), lambda i: (i, 0)),
            pl.BlockSpec((tm, B_WIDTH), lambda i: (i, 0)),
            pl.BlockSpec((pl.Element(1), pl.Element(2 * D_MODEL), pl.Element(D_MODEL)),
                         lambda i: (layer, gate_row0, 0), pipeline_mode=pl.Buffered(1)),
            _layer_spec(wbr, layer),
            _layer_spec(wout, layer),
        ],
        out_specs=pl.BlockSpec((tm, D_MODEL), lambda i: (i, 0)),
        out_shape=jax.ShapeDtypeStruct((t, D_MODEL), F32),
        compiler_params=pltpu.CompilerParams(
            dimension_semantics=("parallel",), vmem_limit_bytes=VMEM_LIMIT),
        name="merge",
    )(res, g, ya, yb, w_t, wbr, wout)


def _mlp_kernel(res_ref, g_ref, wup_ref, wdn_ref, gf_ref, o_ref, *, final_norm):
    for r in range(res_ref.shape[0] // SLAB_ROWS):
        rows = slice(r * SLAB_ROWS, (r + 1) * SLAB_ROWS)
        res = res_ref[rows, :]
        h = _rmsnorm_bf16(res, g_ref[...])
        acc = res
        for c in range(D_FF // D_MODEL):
            cols = slice(c * D_MODEL, (c + 1) * D_MODEL)
            up = jnp.maximum(_dot(h, wup_ref[:, cols]), 0.0)
            acc = acc + _dot((up * up).astype(BF16), wdn_ref[cols, :])
        if final_norm:
            ms = jnp.mean(acc * acc, axis=-1, keepdims=True)
            acc = (acc * lax.rsqrt(ms + EPS)) * gf_ref[...]
        o_ref[rows, :] = acc


def _mlp(res, g, wup, wdn, gf, tm, layer, final_norm):
    t = res.shape[0]
    return pl.pallas_call(
        functools.partial(_mlp_kernel, final_norm=final_norm),
        grid=(t // tm,),
        in_specs=[
            pl.BlockSpec((tm, D_MODEL), lambda i: (i, 0)),
            _const_spec((1, D_MODEL)),
            _layer_spec(wup, layer),
            _layer_spec(wdn, layer),
            _const_spec((1, D_MODEL)),
        ],
        out_specs=pl.BlockSpec((tm, D_MODEL), lambda i: (i, 0)),
        out_shape=jax.ShapeDtypeStruct((t, D_MODEL), F32),
        compiler_params=pltpu.CompilerParams(
            dimension_semantics=("parallel",), vmem_limit_bytes=VMEM_LIMIT),
        name="mlp",
    )(res, g, wup, wdn, gf)


def kernel(x, mix_norm_g, w_in, rel_bias, w_gate_lr, b_gate, gla_norm_g, w_branch, w_out,
           mlp_norm_g, w_up, w_down, final_norm_g):
    bsz, seq, d = x.shape
    t = bsz * seq
    tm = GLA_GROUP * CHUNK
    ab_cols = 3 * A_WIDTH + 2 * B_KEY_WIDTH + 2 * B_WIDTH
    w_in_t = jnp.swapaxes(w_in, 1, 2).astype(BF16)
    w_glr = w_gate_lr.astype(BF16)
    w_br = w_branch.astype(BF16)
    w_o = w_out.astype(BF16)
    w_u = w_up.astype(BF16)
    w_d = w_down.astype(BF16)
    bias = _bias_tables(rel_bias)

    res = x.reshape(t, d)
    for l in range(DEPTH):
        qkva, gq, gate, decay, vt = _inproj(res, mix_norm_g[l][None], w_in_t, w_glr,
                                            b_gate[l][None], tm, l)
        ya = _attention(qkva.reshape(bsz, seq, -1), bias, l)
        yb = _gla(gq.reshape(bsz, seq, -1), vt.reshape(bsz, seq // tm, B_WIDTH, tm),
                  gate.reshape(bsz, seq, -1), decay.reshape(bsz, seq // CHUNK, -1),
                  gla_norm_g[l][None])
        res = _merge(res, mix_norm_g[l][None], ya.reshape(t, -1), yb.reshape(t, -1),
                     w_in_t, ab_cols + GATE_RANK, w_br, w_o, MERGE_TM, l)
        res = _mlp(res, mlp_norm_g[l][None], w_u, w_d, final_norm_g[None], MLP_TM, l,
                   final_norm=(l == DEPTH - 1))
    return res.reshape(bsz, seq, d)
```

```python
import functools

import jax
import jax.numpy as jnp
from jax import lax
from jax.experimental import pallas as pl
from jax.experimental.pallas import tpu as pltpu

D_MODEL = 1024
DEPTH = 4
CHUNK = 64
EPS = 1e-6

A_HEADS = 8
A_HEAD_DIM = 64
A_WIDTH = A_HEADS * A_HEAD_DIM
N_PREV_CHUNKS = 8
PREV = N_PREV_CHUNKS * CHUNK
REL_CLIP = 128
N_REL = 2 * REL_CLIP + 1

B_HEADS = 4
B_KEY_DIM = 64
B_VAL_DIM = 128
B_KEY_WIDTH = B_HEADS * B_KEY_DIM
B_WIDTH = B_HEADS * B_VAL_DIM
GATE_RANK = 16
GATE_TAU = 16.0
D_FF = 4 * D_MODEL

LANES = 128
Q_BLOCK = 2 * CHUNK
K_WINDOW = PREV + Q_BLOCK
N_WIN_TILES = K_WINDOW // 128
VT_ROWS = 128 + 16
ATTN_PAIRS = 2
STAGE_LAG = 3
N_SLOTS = STAGE_LAG + 1
ZERO_BIAS_TILES = tuple(range(1, (PREV - REL_CLIP) // 128))
LOG2E = 1.4426950408889634
SLAB_ROWS = 256
GLA_GROUP = 16
MERGE_TM = 1024
MLP_TM = 1024
NEAR_START = PREV - REL_CLIP
NEAR_SEG = K_WINDOW - NEAR_START + Q_BLOCK
VMEM_LIMIT = 56 * 1024 * 1024

F32 = jnp.float32
BF16 = jnp.bfloat16
NEG = -1e30


def _rmsnorm_bf16(x, g):
    ms = jnp.mean(x * x, axis=-1, keepdims=True)
    return ((x * lax.rsqrt(ms + EPS)) * g).astype(BF16)


def _sigmoid(x):
    return 1.0 / (1.0 + jnp.exp(-x))


def _dot(a, b):
    return jnp.dot(a, b, preferred_element_type=F32)


def _dot_nt(a, b):
    return lax.dot_general(a, b, (((1,), (1,)), ((), ())), preferred_element_type=F32)


def _const_spec(shape):
    nd = len(shape)
    return pl.BlockSpec(shape, lambda *_: (0,) * nd)


def _layer_spec(stacked, layer):
    rest = stacked.shape[1:]
    return pl.BlockSpec((None,) + rest, lambda *_: (layer,) + (0,) * len(rest),
                        pipeline_mode=pl.Buffered(1))


def _bias_kernel(t_ref, o_ref):
    near_w = K_WINDOW - NEAR_START
    r = lax.broadcasted_iota(jnp.int32, (Q_BLOCK, near_w), 0)
    c = lax.broadcasted_iota(jnp.int32, (Q_BLOCK, near_w), 1)
    kr = lax.broadcasted_iota(jnp.int32, (NEAR_START, Q_BLOCK), 0)
    qc = lax.broadcasted_iota(jnp.int32, (NEAR_START, Q_BLOCK), 1)
    for h in range(t_ref.shape[0]):
        lanes = slice((h % 2) * Q_BLOCK, (h % 2 + 1) * Q_BLOCK)
        t = t_ref[h]
        t = (t - t[:, 0:1]) * LOG2E
        x = jnp.broadcast_to(t, (Q_BLOCK, NEAR_SEG))
        x = pltpu.roll(x, 0, 1, stride=1, stride_axis=0)
        near = x[:, Q_BLOCK:]
        near = jnp.where((r < CHUNK) & (c >= near_w - CHUNK), NEG, near)
        for j in range(near_w // LANES):
            o_ref[h // 2, NEAR_START + j * LANES:NEAR_START + (j + 1) * LANES, lanes] = (
                near[:, j * LANES:(j + 1) * LANES].T)
        o_ref[h // 2, :NEAR_START, lanes] = jnp.where((kr < CHUNK) & (qc >= CHUNK), NEG, 0.0)


def _bias_tables(rel_bias):
    far = rel_bias[..., N_REL - 1:]
    n_far = NEAR_SEG - (N_REL - 2)
    seg = jnp.concatenate([
        jnp.broadcast_to(far, rel_bias.shape[:-1] + (n_far,)),
        rel_bias[..., N_REL - 2:0:-1],
    ], axis=-1)
    seg = seg.reshape(DEPTH * A_HEADS, 1, NEAR_SEG)
    n_pair = A_HEADS // 2
    return pl.pallas_call(
        _bias_kernel,
        grid=(DEPTH,),
        in_specs=[pl.BlockSpec((A_HEADS, 1, NEAR_SEG), lambda l: (l, 0, 0))],
        out_specs=pl.BlockSpec((n_pair, K_WINDOW, 2 * Q_BLOCK), lambda l: (l, 0, 0)),
        out_shape=jax.ShapeDtypeStruct((DEPTH * n_pair, K_WINDOW, 2 * Q_BLOCK), F32),
        name="bias_table",
    )(seg)


def _inproj_kernel(res_ref, g_ref, wa_ref, wb_ref, wlr_ref, wg_ref, bg_ref,
                   qkva_ref, gla_ref, gate_ref, decay_ref, vt_ref):
    tm = res_ref.shape[0]
    n_chunk = tm // CHUNK
    kw = B_KEY_WIDTH
    even = (lax.broadcasted_iota(jnp.int32, (1, kw), 1) // B_KEY_DIM) % 2 == 0
    h = _rmsnorm_bf16(res_ref[...], g_ref[...])
    lr = _dot_nt(h, wlr_ref[...])
    z = _dot(lr.astype(BF16), wg_ref[...]) + bg_ref[...]
    log_sig = jnp.minimum(z, 0.0) - jnp.log(1.0 + jnp.exp(-jnp.abs(z)))
    b = _chunk_cumsum(log_sig / GATE_TAU)
    b3 = b.reshape(n_chunk, CHUNK, kw)
    b_last = b3[:, CHUNK - 1:CHUNK, :]
    decay_ref[...] = jnp.exp(b_last).reshape(n_chunk, kw)
    to_end = jnp.exp(b_last - b3).reshape(tm, kw)
    pb = _dot_nt(h, wb_ref[...])
    qt = pb[:, :kw] * (B_KEY_DIM ** -0.5) * jnp.exp(b)
    k = pb[:, kw:2 * kw]
    gla_ref[:, :kw] = jnp.where(even, qt, 0.0).astype(BF16)
    gla_ref[:, kw:2 * kw] = jnp.where(even, 0.0, qt).astype(BF16)
    gla_ref[:, 2 * kw:3 * kw] = (k * jnp.exp(-b)).astype(BF16)
    gla_ref[:, 3 * kw:4 * kw] = (k * to_end).astype(BF16)
    v = pb[:, 2 * kw:2 * kw + B_WIDTH]
    gla_ref[:, 4 * kw:] = v.astype(BF16)
    vt_ref[...] = v.T.astype(BF16)
    r = pb[:, 2 * kw + B_WIDTH:]
    gate_ref[...] = r * _sigmoid(r)
    qkva_ref[...] = _dot_nt(h, wa_ref[...]).astype(BF16)


def _inproj(res, g, w_t, wg, bg, tm, layer):
    t = res.shape[0]
    gla_cols = 4 * B_KEY_WIDTH + B_WIDTH
    half_rows = 3 * A_WIDTH
    assert half_rows == 2 * B_KEY_WIDTH + 2 * B_WIDTH
    half = (None, half_rows, D_MODEL)
    lr_block = 2 * half_rows // GATE_RANK
    return pl.pallas_call(
        _inproj_kernel,
        grid=(t // tm,),
        in_specs=[
            pl.BlockSpec((tm, D_MODEL), lambda i: (i, 0)),
            _const_spec((1, D_MODEL)),
            pl.BlockSpec(half, lambda i: (layer, 0, 0)),
            pl.BlockSpec(half, lambda i: (layer, 1, 0)),
            pl.BlockSpec((None, GATE_RANK, D_MODEL), lambda i: (layer, lr_block, 0)),
            _layer_spec(wg, layer),
            _const_spec((1, B_KEY_WIDTH)),
        ],
        out_specs=[
            pl.BlockSpec((tm, 3 * A_WIDTH), lambda i: (i, 0)),
            pl.BlockSpec((tm, gla_cols), lambda i: (i, 0)),
            pl.BlockSpec((tm, B_WIDTH), lambda i: (i, 0)),
            pl.BlockSpec((tm // CHUNK, B_KEY_WIDTH), lambda i: (i, 0)),
            pl.BlockSpec((None, B_WIDTH, tm), lambda i: (i, 0, 0)),
        ],
        out_shape=[
            jax.ShapeDtypeStruct((t, 3 * A_WIDTH), BF16),
            jax.ShapeDtypeStruct((t, gla_cols), BF16),
            jax.ShapeDtypeStruct((t, B_WIDTH), F32),
            jax.ShapeDtypeStruct((t // CHUNK, B_KEY_WIDTH), F32),
            jax.ShapeDtypeStruct((t // tm, B_WIDTH, tm), BF16),
        ],
        compiler_params=pltpu.CompilerParams(
            dimension_semantics=("parallel",), vmem_limit_bytes=VMEM_LIMIT),
        name="inproj",
    )(res, g, w_t, w_t, w_t, wg, bg)


def _attn_kernel(q_ref, k_ref, v_ref, bias_ref, o_ref, vt_scr, s_scr, m_scr, e_scr):
    seq = q_ref.shape[0]
    n_blk = seq // Q_BLOCK
    n_prev = PREV // Q_BLOCK
    lo = lax.broadcasted_iota(jnp.int32, (1, LANES), 1) < A_HEAD_DIM

    def lanes(pp):
        return slice(pp * LANES, (pp + 1) * LANES)

    for pp in range(ATTN_PAIRS):
        for r in range(n_blk):
            vt_scr[pp, r, :LANES, :] = (
                v_ref[r * LANES:(r + 1) * LANES, lanes(pp)].astype(F32).T.astype(BF16))
            vt_scr[pp, r, LANES:, :] = jnp.ones((VT_ROWS - LANES, LANES), BF16)

    def window(t):
        return max(t - n_prev, 0), min(t + 1, N_WIN_TILES)

    def scores(pp, t):
        kt, n_tiles = window(t)
        rows = n_tiles * LANES
        q2 = (q_ref[t * Q_BLOCK:(t + 1) * Q_BLOCK, lanes(pp)].astype(F32)
              * (A_HEAD_DIM ** -0.5 * LOG2E))
        q4 = jnp.concatenate(
            [jnp.where(lo, q2, 0.0), jnp.where(lo, 0.0, q2)], axis=0).astype(BF16)
        kw = k_ref[kt * LANES:kt * LANES + rows, lanes(pp)]
        s = _dot_nt(kw, q4)
        m = None
        for i in range(n_tiles):
            tile = s[i * LANES:(i + 1) * LANES, :]
            w = N_WIN_TILES - n_tiles + i
            if w not in ZERO_BIAS_TILES:
                tile = tile + bias_ref[pp, w * LANES:(w + 1) * LANES, :]
            s_scr[pp, t % N_SLOTS, i * LANES:(i + 1) * LANES, :] = tile
            tile_max = jnp.max(tile.reshape(LANES // 8, 8, 2 * Q_BLOCK), axis=0)
            m = tile_max if m is None else jnp.maximum(m, tile_max)
        m_scr[pp, t % N_SLOTS] = jnp.max(m, axis=0, keepdims=True)

    def exps(pp, t):
        rows = window(t)[1] * LANES
        d = (s_scr[pp, t % N_SLOTS, :rows, :] - m_scr[pp, t % N_SLOTS]).astype(BF16)
        e_scr[pp, t % N_SLOTS, :rows, :] = jnp.exp2(d)

    def weighted_sum(pp, t):
        kt, n_tiles = window(t)
        vt = jnp.concatenate([vt_scr[pp, kt + i] for i in range(n_tiles)], axis=1)
        ot = _dot(vt, e_scr[pp, t % N_SLOTS, :n_tiles * LANES, :])
        inv = 1.0 / ot[LANES:LANES + 1, :]
        o_pair = jnp.concatenate(
            [ot[:A_HEAD_DIM, :Q_BLOCK] * inv[:, :Q_BLOCK],
             ot[A_HEAD_DIM:LANES, Q_BLOCK:] * inv[:, Q_BLOCK:]], axis=0)
        o_ref[t * Q_BLOCK:(t + 1) * Q_BLOCK, lanes(pp)] = o_pair.T.astype(BF16)

    for step in range(n_blk + 2 * STAGE_LAG):
        for pp in range(ATTN_PAIRS):
            if step < n_blk:
                scores(pp, step)
            if 0 <= step - STAGE_LAG < n_blk:
                exps(pp, step - STAGE_LAG)
            if 0 <= step - 2 * STAGE_LAG < n_blk:
                weighted_sum(pp, step - 2 * STAGE_LAG)


def _attention(qkva, bias, layer):
    bsz, seq, _ = qkva.shape
    n_grp = A_HEADS // 2 // ATTN_PAIRS
    blk = (None, seq, ATTN_PAIRS * LANES)
    return pl.pallas_call(
        _attn_kernel,
        grid=(bsz, n_grp),
        in_specs=[
            pl.BlockSpec(blk, lambda b, p: (b, 0, p)),
            pl.BlockSpec(blk, lambda b, p: (b, 0, n_grp + p)),
            pl.BlockSpec(blk, lambda b, p: (b, 0, 2 * n_grp + p)),
            pl.BlockSpec((ATTN_PAIRS, K_WINDOW, 2 * Q_BLOCK),
                         lambda b, p: (layer * n_grp + p, 0, 0)),
        ],
        out_specs=pl.BlockSpec(blk, lambda b, p: (b, 0, p)),
        out_shape=jax.ShapeDtypeStruct((bsz, seq, A_WIDTH), BF16),
        scratch_shapes=[pltpu.VMEM((ATTN_PAIRS, seq // LANES, VT_ROWS, LANES), BF16),
                        pltpu.VMEM((ATTN_PAIRS, N_SLOTS, K_WINDOW, 2 * Q_BLOCK), F32),
                        pltpu.VMEM((ATTN_PAIRS, N_SLOTS, 1, 2 * Q_BLOCK), F32),
                        pltpu.VMEM((ATTN_PAIRS, N_SLOTS, K_WINDOW, 2 * Q_BLOCK), BF16)],
        compiler_params=pltpu.CompilerParams(
            dimension_semantics=("parallel", "parallel"), vmem_limit_bytes=VMEM_LIMIT),
        name="band_attention",
    )(qkva, qkva, qkva, bias)


def _chunk_cumsum(x):
    row = lax.broadcasted_iota(jnp.int32, x.shape, 0) % CHUNK
    d = 1
    while d < CHUNK:
        x = x + jnp.where(row >= d, pltpu.roll(x, d, axis=0), 0.0)
        d *= 2
    return x


def _gla_kernel(qe_ref, qo_ref, kt_ref, kend_ref, v_ref, vt_ref, gate_ref, decay_ref, g_ref,
                o_ref):
    seq = qe_ref.shape[0]
    n_pair = B_HEADS // 2
    lo = lax.broadcasted_iota(jnp.int32, (1, LANES), 1) < B_KEY_DIM
    ri = lax.broadcasted_iota(jnp.int32, (2 * CHUNK, CHUNK), 0)
    ci = lax.broadcasted_iota(jnp.int32, (2 * CHUNK, CHUNK), 1)
    causal = (ri % CHUNK) >= ci
    group = GLA_GROUP * CHUNK

    def body(i, states):
        rows = slice(i * group, (i + 1) * group)
        q_even = qe_ref[rows, :]
        q_odd = qo_ref[rows, :]
        kt = kt_ref[rows, :]
        kend = kend_ref[rows, :]
        v = v_ref[rows, :]
        decay = decay_ref[i * GLA_GROUP:(i + 1) * GLA_GROUP, :]
        vt = vt_ref[i]
        pad = jnp.zeros((CHUNK, LANES), BF16)
        states = list(states)
        local = {}
        for c in range(GLA_GROUP):
            sl = slice(c * CHUNK, (c + 1) * CHUNK)
            pair_cols = slice((c // 2) * LANES, (c // 2 + 1) * LANES)
            for p in range(n_pair):
                kl = slice(p * LANES, (p + 1) * LANES)
                ve = slice(2 * p * B_VAL_DIM, (2 * p + 1) * B_VAL_DIM)
                vo = slice((2 * p + 1) * B_VAL_DIM, (2 * p + 2) * B_VAL_DIM)
                q4 = jnp.concatenate([q_even[sl, kl], q_odd[sl, kl]], axis=0)
                att = jnp.where(causal, _dot_nt(q4, kt[sl, kl]), 0.0).astype(BF16)
                kend_c = [kend[sl, kl], pad] if c % 2 == 0 else [pad, kend[sl, kl]]
                d_full = _dot(vt[2 * p * B_VAL_DIM:(2 * p + 2) * B_VAL_DIM, pair_cols],
                              jnp.concatenate(kend_c, axis=0))
                local[c, p] = (q4, _dot(att[:CHUNK], v[sl, ve]), _dot(att[CHUNK:], v[sl, vo]),
                               jnp.where(lo, d_full[:B_VAL_DIM], d_full[B_VAL_DIM:]))
        outs = []
        for c in range(GLA_GROUP):
            o_c = []
            for p in range(n_pair):
                kl = slice(p * LANES, (p + 1) * LANES)
                q4, intra_even, intra_odd, d_st = local[c, p]
                inter = _dot_nt(q4, states[p].astype(BF16))
                o_c += [intra_even + inter[:CHUNK], intra_odd + inter[CHUNK:]]
                states[p] = states[p] * decay[c:c + 1, kl] + d_st
            outs.append(jnp.concatenate(o_c, axis=1))
        o = jnp.concatenate(outs, axis=0)
        gate = gate_ref[rows, :]
        for h in range(B_HEADS):
            hl = slice(h * B_VAL_DIM, (h + 1) * B_VAL_DIM)
            oh = o[:, hl]
            oh = oh * lax.rsqrt(jnp.mean(oh * oh, axis=-1, keepdims=True) + EPS) * g_ref[:, hl]
            o_ref[rows, hl] = (oh * gate[:, hl]).astype(BF16)
        return tuple(states)

    states = (jnp.zeros((B_VAL_DIM, LANES), F32),) * n_pair
    for i in range(seq // group):
        states = body(i, states)


def _gla(gq, vt, gate, decay, g):
    bsz, seq, _ = gq.shape
    kblk = (None, seq, B_KEY_WIDTH)
    vblk = (None, seq, B_WIDTH)
    return pl.pallas_call(
        _gla_kernel,
        grid=(bsz,),
        in_specs=[pl.BlockSpec(kblk, lambda b, j=j: (b, 0, j)) for j in range(4)] + [
            pl.BlockSpec(vblk, lambda b: (b, 0, 4 * B_KEY_WIDTH // B_WIDTH)),
            pl.BlockSpec((None,) + vt.shape[1:], lambda b: (b, 0, 0, 0)),
            pl.BlockSpec(vblk, lambda b: (b, 0, 0)),
            pl.BlockSpec((None, seq // CHUNK, B_KEY_WIDTH), lambda b: (b, 0, 0)),
            _const_spec((1, B_WIDTH)),
        ],
        out_specs=pl.BlockSpec(vblk, lambda b: (b, 0, 0)),
        out_shape=jax.ShapeDtypeStruct((bsz, seq, B_WIDTH), BF16),
        compiler_params=pltpu.CompilerParams(
            dimension_semantics=("parallel",), vmem_limit_bytes=VMEM_LIMIT),
        name="gla",
    )(gq, gq, gq, gq, gq, vt, gate, decay, g)


def _merge_kernel(res_ref, g_ref, ya_ref, yb_ref, wgate_ref, wbr_ref, wout_ref, o_ref):
    for r in range(res_ref.shape[0] // SLAB_ROWS):
        rows = slice(r * SLAB_ROWS, (r + 1) * SLAB_ROWS)
        res = res_ref[rows, :]
        h = _rmsnorm_bf16(res, g_ref[...])
        gates = _dot_nt(h, wgate_ref[0])
        u_a = _dot(ya_ref[rows, :], wbr_ref[0])
        u_b = _dot(yb_ref[rows, :], wbr_ref[1])
        merged = _sigmoid(gates[:, :D_MODEL]) * u_a + _sigmoid(gates[:, D_MODEL:]) * u_b
        o_ref[rows, :] = res + _dot(merged.astype(BF16), wout_ref[...])


def _merge(res, g, ya, yb, w_t, gate_row0, wbr, wout, tm, layer):
    t = res.shape[0]
    return pl.pallas_call(
        _merge_kernel,
        grid=(t // tm,),
        in_specs=[
            pl.BlockSpec((tm, D_MODEL), lambda i: (i, 0)),
            _const_spec((1, D_MODEL)),
            pl.BlockSpec((tm, A_WIDTH), lambda i: (i, 0)),
            pl.BlockSpec((tm, B_WIDTH), lambda i: (i, 0)),
            pl.BlockSpec((pl.Element(1), pl.Element(2 * D_MODEL), pl.Element(D_MODEL)),
                         lambda i: (layer, gate_row0, 0), pipeline_mode=pl.Buffered(1)),
            _layer_spec(wbr, layer),
            _layer_spec(wout, layer),
        ],
        out_specs=pl.BlockSpec((tm, D_MODEL), lambda i: (i, 0)),
        out_shape=jax.ShapeDtypeStruct((t, D_MODEL), F32),
        compiler_params=pltpu.CompilerParams(
            dimension_semantics=("parallel",), vmem_limit_bytes=VMEM_LIMIT),
        name="merge",
    )(res, g, ya, yb, w_t, wbr, wout)


def _mlp_kernel(res_ref, g_ref, wup_ref, wdn_ref, gf_ref, o_ref, *, final_norm):
    for r in range(res_ref.shape[0] // SLAB_ROWS):
        rows = slice(r * SLAB_ROWS, (r + 1) * SLAB_ROWS)
        res = res_ref[rows, :]
        h = _rmsnorm_bf16(res, g_ref[...])
        acc = res
        for c in range(D_FF // D_MODEL):
            cols = slice(c * D_MODEL, (c + 1) * D_MODEL)
            up = jnp.maximum(_dot(h, wup_ref[:, cols]), 0.0)
            acc = acc + _dot((up * up).astype(BF16), wdn_ref[cols, :])
        if final_norm:
            ms = jnp.mean(acc * acc, axis=-1, keepdims=True)
            acc = (acc * lax.rsqrt(ms + EPS)) * gf_ref[...]
        o_ref[rows, :] = acc


def _mlp(res, g, wup, wdn, gf, tm, layer, final_norm):
    t = res.shape[0]
    return pl.pallas_call(
        functools.partial(_mlp_kernel, final_norm=final_norm),
        grid=(t // tm,),
        in_specs=[
            pl.BlockSpec((tm, D_MODEL), lambda i: (i, 0)),
            _const_spec((1, D_MODEL)),
            _layer_spec(wup, layer),
            _layer_spec(wdn, layer),
            _const_spec((1, D_MODEL)),
        ],
        out_specs=pl.BlockSpec((tm, D_MODEL), lambda i: (i, 0)),
        out_shape=jax.ShapeDtypeStruct((t, D_MODEL), F32),
        compiler_params=pltpu.CompilerParams(
            dimension_semantics=("parallel",), vmem_limit_bytes=VMEM_LIMIT),
        name="mlp",
    )(res, g, wup, wdn, gf)


def kernel(x, mix_norm_g, w_in, rel_bias, w_gate_lr, b_gate, gla_norm_g, w_branch, w_out,
           mlp_norm_g, w_up, w_down, final_norm_g):
    bsz, seq, d = x.shape
    t = bsz * seq
    tm = GLA_GROUP * CHUNK
    ab_cols = 3 * A_WIDTH + 2 * B_KEY_WIDTH + 2 * B_WIDTH
    w_in_t = jnp.swapaxes(w_in, 1, 2).astype(BF16)
    w_glr = w_gate_lr.astype(BF16)
    w_br = w_branch.astype(BF16)
    w_o = w_out.astype(BF16)
    w_u = w_up.astype(BF16)
    w_d = w_down.astype(BF16)
    bias = _bias_tables(rel_bias)

    res = x.reshape(t, d)
    for l in range(DEPTH):
        qkva, gq, gate, decay, vt = _inproj(res, mix_norm_g[l][None], w_in_t, w_glr,
                                            b_gate[l][None], tm, l)
        ya = _attention(qkva.reshape(bsz, seq, -1), bias, l)
        yb = _gla(gq.reshape(bsz, seq, -1), vt.reshape(bsz, seq // tm, B_WIDTH, tm),
                  gate.reshape(bsz, seq, -1), decay.reshape(bsz, seq // CHUNK, -1),
                  gla_norm_g[l][None])
        res = _merge(res, mix_norm_g[l][None], ya.reshape(t, -1), yb.reshape(t, -1),
                     w_in_t, ab_cols + GATE_RANK, w_br, w_o, MERGE_TM, l)
        res = _mlp(res, mlp_norm_g[l][None], w_u, w_d, final_norm_g[None], MLP_TM, l,
                   final_norm=(l == DEPTH - 1))
    return res.reshape(bsz, seq, d)
```
